```python
import math
import jax, jax.numpy as jnp
from jax import lax
import numpy as np

D_MODEL = 1024
BATCH = 1
SEQ = 16384
DEPTH = 2
DEC_BATCH = 32
DEC_SEQ = 4
PAST_LEN = 16384
PAGE_SIZE = 128

N_META = 16
HEAD_DIM = 64
A_WIDTH = 3 * D_MODEL // 8
A_HEADS = A_WIDTH // HEAD_DIM
LORA_W = 64
LORA_A = 64
LORA_G = 128
A_PROJ = 3 * A_WIDTH + LORA_W + LORA_A + LORA_G
A_SPLITS = (A_WIDTH, 2 * A_WIDTH, 3 * A_WIDTH, 3 * A_WIDTH + LORA_W, 3 * A_WIDTH + LORA_W + LORA_A)
GN_EPS = 64e-5
B_WIDTH = 3 * D_MODEL // 8
B_HEADS = B_WIDTH // HEAD_DIM
B_KV_HEADS = 2
B_REP = B_HEADS // B_KV_HEADS
KV_WIDTH = B_KV_HEADS * HEAD_DIM
IDX_HEADS = 4
IDX_DIM = 64
IDX_SCALE = (IDX_HEADS * IDX_DIM) ** -0.5
TOPK_MAX = 256
Q_BLOCK = 128
B_PROJ = B_WIDTH + 2 * KV_WIDTH + IDX_HEADS * IDX_DIM + IDX_DIM + IDX_HEADS
B_SPLITS = (B_WIDTH, B_WIDTH + KV_WIDTH, B_WIDTH + 2 * KV_WIDTH, B_WIDTH + 2 * KV_WIDTH + IDX_HEADS * IDX_DIM, B_WIDTH + 2 * KV_WIDTH + IDX_HEADS * IDX_DIM + IDX_DIM)
C_WIDTH = D_MODEL - A_WIDTH - B_WIDTH
C_GROUPS = 4
C_GROUP_DIM = C_WIDTH // C_GROUPS
POOL_WINDOWS = (2, 4, 8, 16)
POOL_BUF = 15
MIX_WIDTH = A_WIDTH + B_WIDTH + C_WIDTH
IN_PROJ = A_PROJ + B_PROJ + C_WIDTH
N_BUCKETS = 32
MAX_DISTANCE = 128
N_EXPERTS = 32
TOP_K = 4
D_FF = D_MODEL
SWIGLU_LIMIT = 7.0
SWIGLU_ALPHA = 1.702
MOE_BLOCK = 128
RMS_EPS = 1e-5

kernel_name = 'hymba_rwkv7_dsa_pool_moe_step'


def rms_norm(x, g):
    xf = x.astype(jnp.float32)
    y = xf * lax.rsqrt(jnp.mean(xf * xf, axis=-1, keepdims=True) + RMS_EPS)
    return (y * g.astype(jnp.float32)).astype(x.dtype)


def rel_bucket(dist):
    n = jnp.maximum(dist, 0)
    max_exact = N_BUCKETS // 2
    log_ratio = jnp.log(jnp.maximum(n, 1).astype(jnp.float32) / max_exact) / math.log(MAX_DISTANCE / max_exact)
    large = jnp.minimum(max_exact + (log_ratio * (N_BUCKETS - max_exact)).astype(jnp.int32), N_BUCKETS - 1)
    return jnp.where(n < max_exact, n, large)


def gather_rows(rows, idx):
    return jax.vmap(lambda r, i: r[i])(rows, idx)


def rwkv7_mix(pa, shift_prev, wkv_prev, mu, w0, w2, a0, a2, g2, k_k, k_a, r_k, ln_w, ln_b):
    bsz, t = pa.shape[:2]
    prev = jnp.concatenate([shift_prev[:, None].astype(pa.dtype), pa[:, :-1]], axis=1)
    xm = pa + (prev - pa) * mu
    r, k, v, lw, la, lg = jnp.split(xm, A_SPLITS, axis=-1)
    w_log = -jax.nn.softplus(-(w0 + jnp.tanh(lw) @ w2)) - 0.5
    decay = jnp.exp(-jnp.exp(w_log.astype(jnp.float32)))
    a = jax.nn.sigmoid(a0 + la @ a2)
    g = jax.nn.sigmoid(lg) @ g2

    def heads(z):
        return z.astype(jnp.float32).reshape(bsz, t, A_HEADS, HEAD_DIM)

    kk = heads(k * k_k)
    kk = kk / jnp.maximum(jnp.sqrt(jnp.sum(kk * kk, axis=-1, keepdims=True)), 1e-12)
    k = k * (1.0 + (a - 1.0) * k_a)
    rh, kh, vh, wh, ah = heads(r), heads(k), heads(v), heads(decay), heads(a)

    def step(S, inp):
        r_t, k_t, v_t, w_t, kk_t, a_t = inp
        S = (S * w_t[:, :, None, :]
             - jnp.einsum('bhvk,bhk->bhv', S, kk_t)[..., None] * (kk_t * a_t)[:, :, None, :]
             + v_t[..., None] * k_t[:, :, None, :])
        return S, jnp.einsum('bhvk,bhk->bhv', S, r_t)

    def seq_first(z):
        return jnp.moveaxis(z, 1, 0)

    S, o = lax.scan(step, wkv_prev.astype(jnp.float32),
                    (seq_first(rh), seq_first(kh), seq_first(vh), seq_first(wh), seq_first(kk), seq_first(ah)))
    o = jnp.moveaxis(o, 0, 1)
    mean = jnp.mean(o, axis=-1, keepdims=True)
    var = jnp.mean(jnp.square(o - mean), axis=-1, keepdims=True)
    o = ((o - mean) * lax.rsqrt(var + GN_EPS)).reshape(bsz, t, A_WIDTH) * ln_w + ln_b
    rk = r_k.astype(jnp.float32).reshape(A_HEADS, HEAD_DIM)
    bonus = (jnp.sum(rh * kh * rk, axis=-1, keepdims=True) * vh).reshape(bsz, t, A_WIDTH)
    o = (o + bonus) * g
    return o.astype(pa.dtype), pa[:, -1], S


def split_dsa(pb):
    bsz, t = pb.shape[:2]
    q, k, v, qi, ki, wi = jnp.split(pb, B_SPLITS, axis=-1)
    return (q.reshape(bsz, t, B_HEADS, HEAD_DIM), k.reshape(bsz, t, B_KV_HEADS, HEAD_DIM),
            v.reshape(bsz, t, B_KV_HEADS, HEAD_DIM), qi.reshape(bsz, t, IDX_HEADS, IDX_DIM), ki, wi)


def indexer_topk(qi, wi, ki, qpos, kpos, topk):
    s = jax.nn.relu(jnp.einsum('bqhd,bkd->bqhk', qi.astype(jnp.float32), ki.astype(jnp.float32)))
    score = jnp.einsum('bqhk,bqh->bqk', s, wi.astype(jnp.float32)) * IDX_SCALE
    score = jnp.where(kpos[None, None, :] <= qpos[None, :, None], score, -jnp.inf)
    return lax.top_k(score, topk)[1]


def sparse_attend(q, k_sel, v_sel, idx, qpos, rel_bias):
    bsz, nq = q.shape[:2]
    topk = idx.shape[-1]
    qg = q.astype(jnp.float32).reshape(bsz, nq, B_KV_HEADS, B_REP, HEAD_DIM)
    logits = jnp.einsum('bqgrd,bqkgd->bqgrk', qg, k_sel.astype(jnp.float32)) * HEAD_DIM ** -0.5
    dist = qpos[None, :, None] - idx
    bias = rel_bias.astype(jnp.float32)[rel_bucket(dist)]
    bias = jnp.transpose(bias.reshape(bsz, nq, topk, B_KV_HEADS, B_REP), (0, 1, 3, 4, 2))
    logits = jnp.where((dist >= 0)[:, :, None, None, :], logits + bias, -jnp.inf)
    p = jax.nn.softmax(logits, axis=-1)
    o = jnp.einsum('bqgrk,bqkgd->bqgrd', p, v_sel.astype(jnp.float32))
    return o.reshape(bsz, nq, B_WIDTH).astype(q.dtype)


def dsa_prompt(pb, rel_bias):
    q, k, v, qi, ki, wi = split_dsa(pb)
    bsz, t = pb.shape[:2]
    topk = min(TOPK_MAX, SEQ // 4)
    n_blk = -(-t // Q_BLOCK)
    t_pad = n_blk * Q_BLOCK

    def blocks(z):
        z = jnp.pad(z, [(0, 0), (0, t_pad - t)] + [(0, 0)] * (z.ndim - 2))
        return jnp.moveaxis(z.reshape((bsz, n_blk, Q_BLOCK) + z.shape[2:]), 1, 0)

    qpos = jnp.arange(t_pad, dtype=jnp.int32).reshape(n_blk, Q_BLOCK)
    kpos = jnp.arange(t, dtype=jnp.int32)

    def one_block(args):
        qb, qib, wib, qp = args
        idx = indexer_topk(qib, wib, ki, qp, kpos, topk)
        return sparse_attend(qb, gather_rows(k, idx), gather_rows(v, idx), idx, qp, rel_bias)

    o = lax.map(one_block, (blocks(q), blocks(qi), blocks(wi), qpos))
    o = jnp.moveaxis(o, 0, 1).reshape(bsz, t_pad, B_WIDTH)[:, :t]
    return o, k, v, ki


def dsa_sample(pb, ck, cv, cki, page_table, rel_bias):
    q, k, v, qi, ki, wi = split_dsa(pb)
    nb, t = pb.shape[:2]
    n_keys = PAST_LEN + t
    topk = min(TOPK_MAX, n_keys // 4)
    ki_past = cki[page_table].reshape(nb, PAST_LEN, IDX_DIM)
    ki_all = jnp.concatenate([ki_past, ki.astype(ki_past.dtype)], axis=1)
    qpos = PAST_LEN + jnp.arange(t, dtype=jnp.int32)
    kpos = jnp.arange(n_keys, dtype=jnp.int32)
    idx = indexer_topk(qi, wi, ki_all, qpos, kpos, topk)
    is_new = (idx >= PAST_LEN)[..., None, None]
    past_idx = jnp.minimum(idx, PAST_LEN - 1)
    phys = jax.vmap(lambda pt, i: pt[i])(page_table, past_idx // PAGE_SIZE)
    off = past_idx % PAGE_SIZE
    new_idx = jnp.maximum(idx - PAST_LEN, 0)
    k_sel = jnp.where(is_new, gather_rows(k, new_idx).astype(ck.dtype), ck[phys, off])
    v_sel = jnp.where(is_new, gather_rows(v, new_idx).astype(cv.dtype), cv[phys, off])
    o = sparse_attend(q, k_sel, v_sel, idx, qpos, rel_bias)
    return o, k, v, ki


def pool_mix(c, buf, w_pool, scale):
    bsz, t, _ = c.shape
    n_past = buf.shape[1]
    full = jnp.concatenate([buf.astype(c.dtype), c], axis=1)
    cs = jnp.concatenate([jnp.zeros((bsz, 1, C_WIDTH), jnp.float32), jnp.cumsum(full.astype(jnp.float32), axis=1)], axis=1)
    end = n_past + 1 + jnp.arange(t)
    cf = c.astype(jnp.float32).reshape(bsz, t, C_GROUPS, C_GROUP_DIM)
    diffs = []
    for g, win in enumerate(POOL_WINDOWS):
        start = jnp.maximum(end - win, 0)
        cs_g = cs[..., g * C_GROUP_DIM:(g + 1) * C_GROUP_DIM]
        mean = (cs_g[:, end] - cs_g[:, start]) / (end - start).astype(jnp.float32)[None, :, None]
        diffs.append(mean - cf[:, :, g])
    d = jnp.stack(diffs, axis=2)
    y = jnp.einsum('btgc,gcd->btgd', d, w_pool.astype(jnp.float32)).reshape(bsz, t, C_WIDTH) * scale
    return y.astype(c.dtype), full[:, -POOL_BUF:]


def moe_ffn(h, w_router, b_router, w1, b1, w2, b2):
    shape = h.shape
    hf = h.reshape(-1, D_MODEL)
    n_tok = hf.shape[0]
    logits = (hf @ w_router + b_router).astype(jnp.float32)
    top_val, top_idx = lax.top_k(logits, TOP_K)
    gate = jax.nn.softmax(top_val, axis=-1)
    flat_e = top_idx.reshape(-1)
    n_asg = n_tok * TOP_K
    order = jnp.argsort(flat_e)
    e_sorted = flat_e[order]
    counts = jnp.bincount(flat_e, length=N_EXPERTS)
    padded = (counts + MOE_BLOCK - 1) // MOE_BLOCK * MOE_BLOCK
    pad_end = jnp.cumsum(padded)
    pad_start = pad_end - padded
    grp_start = jnp.cumsum(counts) - counts
    dest = pad_start[e_sorted] + jnp.arange(n_asg) - grp_start[e_sorted]
    n_blk = -(-n_asg // MOE_BLOCK) + N_EXPERTS
    tok_sorted = order // TOP_K
    rows = jnp.zeros((n_blk * MOE_BLOCK, D_MODEL), hf.dtype).at[dest].set(hf[tok_sorted])
    blk_e = jnp.minimum(jnp.searchsorted(pad_end, jnp.arange(n_blk) * MOE_BLOCK, side='right'), N_EXPERTS - 1)

    def expert_block(args):
        xb, e = args
        u = (xb @ w1[e] + b1[e]).astype(jnp.float32)
        glu = jnp.minimum(u[:, :D_FF], SWIGLU_LIMIT)
        lin = jnp.clip(u[:, D_FF:], -SWIGLU_LIMIT, SWIGLU_LIMIT)
        act = (glu * jax.nn.sigmoid(SWIGLU_ALPHA * glu) * (lin + 1.0)).astype(xb.dtype)
        return act @ w2[e] + b2[e]

    y_rows = lax.map(expert_block, (rows.reshape(n_blk, MOE_BLOCK, D_MODEL), blk_e)).reshape(-1, D_MODEL)
    contrib = y_rows[dest].astype(jnp.float32) * gate.reshape(-1)[order][:, None]
    y = jax.ops.segment_sum(contrib, tok_sorted, num_segments=n_tok)
    return y.astype(h.dtype).reshape(shape)


def mixer_sublayer(x, norm_w, w_in, w_out, rwkv_params, c_w, c_scale, shift_prev, wkv_prev, pool_prev, dsa_fn):
    h = rms_norm(x, norm_w)
    proj = h @ w_in
    pa = proj[..., :A_PROJ]
    pb = proj[..., A_PROJ:A_PROJ + B_PROJ]
    pc = proj[..., A_PROJ + B_PROJ:]
    oa, shift_new, wkv_new = rwkv7_mix(pa, shift_prev, wkv_prev, *rwkv_params)
    ob, k_new, v_new, ki_new = dsa_fn(pb)
    oc, pool_new = pool_mix(pc, pool_prev, c_w, c_scale)
    x = x + jnp.concatenate([oa, ob, oc], axis=-1) @ w_out
    return x, k_new, v_new, ki_new, wkv_new, shift_new, pool_new


def setup_inputs(seed: int = 0) -> dict:
    key = jax.random.key(seed)
    ks = jax.random.split(key, 36)

    def nrm(i, shape, scale):
        return jax.random.normal(ks[i], shape, jnp.float32) * scale

    n_pages = PAST_LEN // PAGE_SIZE
    n_used = DEC_BATCH * n_pages
    n_phys = n_used + max(1, n_used // 4)
    page_table = jax.random.permutation(ks[8], n_phys)[:n_used].reshape(DEC_BATCH, n_pages).astype(jnp.int32)
    return {
        'x_prompt': nrm(0, (BATCH, SEQ, D_MODEL), 1.0),
        'x_sample': nrm(1, (DEC_BATCH, DEC_SEQ, D_MODEL), 1.0),
        'cache_k': nrm(2, (DEPTH, n_phys, PAGE_SIZE, B_KV_HEADS, HEAD_DIM), 1.0),
        'cache_v': nrm(3, (DEPTH, n_phys, PAGE_SIZE, B_KV_HEADS, HEAD_DIM), 1.0),
        'cache_kidx': nrm(4, (DEPTH, n_phys, PAGE_SIZE, IDX_DIM), 1.0),
        'state_wkv': nrm(5, (DEPTH, DEC_BATCH, A_HEADS, HEAD_DIM, HEAD_DIM), 0.3),
        'state_shift': nrm(6, (DEPTH, DEC_BATCH, A_PROJ), 1.0),
        'state_pool': nrm(7, (DEPTH, DEC_BATCH, POOL_BUF, C_WIDTH), 1.0),
        'page_table': page_table,
        'meta_tokens': nrm(9, (N_META, D_MODEL), 1.0),
        'rel_bias': nrm(10, (N_BUCKETS, B_HEADS), 0.5),
        'norm_mix': 1.0 + nrm(11, (DEPTH, D_MODEL), 0.05),
        'norm_ffn': 1.0 + nrm(12, (DEPTH, D_MODEL), 0.05),
        'norm_final': 1.0 + nrm(13, (D_MODEL,), 0.05),
        'w_in': nrm(14, (DEPTH, D_MODEL, IN_PROJ), D_MODEL ** -0.5),
        'w_out': nrm(15, (DEPTH, MIX_WIDTH, D_MODEL), 0.5 * MIX_WIDTH ** -0.5),
        'a_mu': jax.random.uniform(ks[16], (DEPTH, A_PROJ), jnp.float32),
        'a_w0': nrm(17, (DEPTH, A_WIDTH), 0.5),
        'a_w2': nrm(18, (DEPTH, LORA_W, A_WIDTH), LORA_W ** -0.5),
        'a_a0': nrm(19, (DEPTH, A_WIDTH), 0.1),
        'a_a2': nrm(20, (DEPTH, LORA_A, A_WIDTH), LORA_A ** -0.5),
        'a_g2': nrm(21, (DEPTH, LORA_G, A_WIDTH), LORA_G ** -0.5),
        'a_kk': 0.85 + nrm(22, (DEPTH, A_WIDTH), 0.05),
        'a_ka': 1.0 + nrm(23, (DEPTH, A_WIDTH), 0.05),
        'a_rk': nrm(24, (DEPTH, A_WIDTH), 0.1),
        'a_ln_w': 1.0 + nrm(25, (DEPTH, A_WIDTH), 0.05),
        'a_ln_b': nrm(26, (DEPTH, A_WIDTH), 0.02),
        'c_w': nrm(27, (DEPTH, C_GROUPS, C_GROUP_DIM, C_GROUP_DIM), C_GROUP_DIM ** -0.5),
        'c_scale': 1.0 + nrm(28, (DEPTH, C_WIDTH), 0.05),
        'moe_w_router': nrm(29, (DEPTH, D_MODEL, N_EXPERTS), D_MODEL ** -0.5),
        'moe_b_router': nrm(30, (DEPTH, N_EXPERTS), 0.01),
        'moe_w1': nrm(31, (DEPTH, N_EXPERTS, D_MODEL, 2 * D_FF), D_MODEL ** -0.5),
        'moe_b1': nrm(32, (DEPTH, N_EXPERTS, 2 * D_FF), 0.01),
        'moe_w2': nrm(33, (DEPTH, N_EXPERTS, D_FF, D_MODEL), 0.5 * D_FF ** -0.5),
        'moe_b2': nrm(34, (DEPTH, N_EXPERTS, D_MODEL), 0.01),
    }


def reference(x_prompt, x_sample, cache_k, cache_v, cache_kidx, state_wkv, state_shift, state_pool, page_table,
              meta_tokens, rel_bias, norm_mix, norm_ffn, norm_final, w_in, w_out,
              a_mu, a_w0, a_w2, a_a0, a_a2, a_g2, a_kk, a_ka, a_rk, a_ln_w, a_ln_b,
              c_w, c_scale, moe_w_router, moe_b_router, moe_w1, moe_b1, moe_w2, moe_b2):
    bsz = x_prompt.shape[0]
    xp = jnp.concatenate([jnp.broadcast_to(meta_tokens[None].astype(x_prompt.dtype), (bsz, N_META, D_MODEL)), x_prompt], axis=1)
    xs = x_sample
    kp, vp, kip, wkvp, shp, plp = [], [], [], [], [], []
    ksm, vsm, kism, wkvs, shs, pls = [], [], [], [], [], []
    for l in range(DEPTH):
        rwkv_params = (a_mu[l], a_w0[l], a_w2[l], a_a0[l], a_a2[l], a_g2[l], a_kk[l], a_ka[l], a_rk[l], a_ln_w[l], a_ln_b[l])
        moe_params = (moe_w_router[l], moe_b_router[l], moe_w1[l], moe_b1[l], moe_w2[l], moe_b2[l])
        xp, k_n, v_n, ki_n, wkv_n, sh_n, pl_n = mixer_sublayer(
            xp, norm_mix[l], w_in[l], w_out[l], rwkv_params, c_w[l], c_scale[l],
            jnp.zeros((bsz, A_PROJ), xp.dtype), jnp.zeros((bsz, A_HEADS, HEAD_DIM, HEAD_DIM), jnp.float32),
            xp[:, :0, :C_WIDTH], lambda pb: dsa_prompt(pb, rel_bias))
        xp = xp + moe_ffn(rms_norm(xp, norm_ffn[l]), *moe_params)
        kp.append(k_n); vp.append(v_n); kip.append(ki_n)
        wkvp.append(wkv_n.astype(state_wkv.dtype)); shp.append(sh_n); plp.append(pl_n)
        ck, cv, cki = cache_k[l], cache_v[l], cache_kidx[l]
        xs, k_n, v_n, ki_n, wkv_n, sh_n, pl_n = mixer_sublayer(
            xs, norm_mix[l], w_in[l], w_out[l], rwkv_params, c_w[l], c_scale[l],
            state_shift[l], state_wkv[l], state_pool[l],
            lambda pb: dsa_sample(pb, ck, cv, cki, page_table, rel_bias))
        xs = xs + moe_ffn(rms_norm(xs, norm_ffn[l]), *moe_params)
        ksm.append(k_n); vsm.append(v_n); kism.append(ki_n)
        wkvs.append(wkv_n.astype(state_wkv.dtype)); shs.append(sh_n); pls.append(pl_n)
    y_prompt = rms_norm(xp, norm_final)[:, N_META:]
    y_sample = rms_norm(xs, norm_final)
    return (y_prompt, y_sample,
            jnp.stack(kp), jnp.stack(vp), jnp.stack(kip), jnp.stack(wkvp), jnp.stack(shp), jnp.stack(plp),
            jnp.stack(ksm), jnp.stack(vsm), jnp.stack(kism), jnp.stack(wkvs), jnp.stack(shs), jnp.stack(pls))
```

```python
import functools
import math

import jax
import jax.numpy as jnp
import numpy as np
from jax import lax
from jax.experimental import pallas as pl
from jax.experimental.pallas import tpu as pltpu

F32 = jnp.float32
BF16 = jnp.bfloat16
I32 = jnp.int32
HIGHEST = lax.Precision.HIGHEST

D_MODEL = 1024
N_META = 16
HEAD_DIM = 64
A_WIDTH = 384
A_HEADS = 6
LORA_W, LORA_A, LORA_G = 64, 64, 128
A_PROJ = 3 * A_WIDTH + LORA_W + LORA_A + LORA_G
GN_EPS = 64e-5
B_WIDTH = 384
B_HEADS = 6
B_KV_HEADS = 2
B_REP = 3
KV_WIDTH = 128
IDX_HEADS = 4
IDX_DIM = 64
IDX_SCALE = (IDX_HEADS * IDX_DIM) ** -0.5
TOPK_MAX = 256
B_PROJ = B_WIDTH + 2 * KV_WIDTH + IDX_HEADS * IDX_DIM + IDX_DIM + IDX_HEADS
C_WIDTH = 256
C_GROUP_DIM = 64
POOL_WINDOWS = (2, 4, 8, 16)
POOL_BUF = 15
N_BUCKETS = 32
MAX_DISTANCE = 128
N_EXPERTS = 32
TOP_K = 4
D_FF = 1024
SWIGLU_LIMIT = 7.0
SWIGLU_ALPHA = 1.702
RMS_EPS = 1e-5
PAGE_SIZE = 128

LANES = 128
SUBLANES = 8
Q_TILE = 128
KEY_CHUNK = 512
KEY_FRONT_PAD = 128
TOK_TILE = 512
RWKV_BLOCK = 128
POOL_HALO = 16
VMEM_LIMIT = 56 * 1024 * 1024

INT_MIN = -(2 ** 31)
NEG_INF_KEY = np.int32(np.uint32(0xFF800000) ^ np.uint32(0x7FFFFFFF))
NEG_BIG = -1e30


def _round_up(x, m):
    return (x + m - 1) // m * m


def _cparams(sem):
    return pltpu.CompilerParams(dimension_semantics=sem, vmem_limit_bytes=VMEM_LIMIT)


def _rms(x, g):
    return x * lax.rsqrt(jnp.mean(x * x, axis=-1, keepdims=True) + RMS_EPS) * g


def _sigmoid(x):
    return 1.0 / (1.0 + jnp.exp(-x))


IN_COLS = (A_PROJ, B_WIDTH, KV_WIDTH, KV_WIDTH, IDX_HEADS * IDX_DIM, LANES, C_WIDTH)


def _inproj_kernel(x_ref, g_ref, w_ref, *out_refs):
    hb = _rms(x_ref[...], g_ref[...]).astype(BF16)
    off = 0
    for ref in out_refs:
        n = ref.shape[1]
        ref[...] = jnp.dot(hb, w_ref[:, off:off + n], preferred_element_type=F32)
        off += n


def _inproj(x, g, w):
    nt = x.shape[0]
    return pl.pallas_call(
        _inproj_kernel,
        grid=(nt // TOK_TILE,),
        in_specs=[pl.BlockSpec((TOK_TILE, D_MODEL), lambda i: (i, 0)),
                  pl.BlockSpec((1, D_MODEL), lambda i: (0, 0)),
                  pl.BlockSpec(w.shape, lambda i: (0, 0))],
        out_specs=[pl.BlockSpec((TOK_TILE, n), lambda i: (i, 0)) for n in IN_COLS],
        out_shape=[jax.ShapeDtypeStruct((nt, n), F32) for n in IN_COLS],
        compiler_params=_cparams(("parallel",)),
        name="inproj",
    )(x, g, w)


def _rwkv_kernel(pa_ref, sh_ref, s0_ref, mu_ref, w0_ref, w2_ref, a0_ref, a2_ref, g2_ref, kk_ref, ka_ref,
                 rk_ref, lnw_ref, lnb_ref, ones_ref, o_ref, sT_ref, carry_scr, s_scr, o_scr, *rows_scr,
                 seq_len, block):
    blk = pl.program_id(1)

    @pl.when(blk == 0)
    def _():
        carry_scr[...] = sh_ref[0]
        s_scr[...] = s0_ref[0]

    pa = pa_ref[0]
    row = lax.broadcasted_iota(I32, (block, 1), 0)
    prev = jnp.where(row == 0, carry_scr[...], pltpu.roll(pa, 1, axis=0))
    carry_scr[...] = pa[block - 1:block, :]
    xm = pa + (prev - pa) * mu_ref[...]
    r = xm[:, 0:A_WIDTH]
    k = xm[:, A_WIDTH:2 * A_WIDTH]
    v = xm[:, 2 * A_WIDTH:3 * A_WIDTH]
    lw = xm[:, 3 * A_WIDTH:3 * A_WIDTH + LORA_W]
    la = xm[:, 3 * A_WIDTH + LORA_W:3 * A_WIDTH + LORA_W + LORA_A]
    lg = xm[:, 3 * A_WIDTH + LORA_W + LORA_A:]

    def dot_hi(a, b):
        return jnp.dot(a, b, precision=HIGHEST, preferred_element_type=F32)

    z = -(w0_ref[...] + dot_hi(jnp.tanh(lw), w2_ref[...]))
    softplus = jnp.maximum(z, 0.0) + jnp.log(1.0 + jnp.exp(-jnp.abs(z)))
    w_log = -softplus - 0.5
    decay = jnp.exp(-jnp.exp(w_log))
    a = _sigmoid(a0_ref[...] + dot_hi(la, a2_ref[...]))
    g = dot_hi(_sigmoid(lg), g2_ref[...])
    ones_blk = ones_ref[...]
    kk = k * kk_ref[...]
    kk = kk / jnp.maximum(jnp.sqrt(dot_hi(kk * kk, ones_blk)), 1e-12)
    k2 = k * (1.0 + (a - 1.0) * ka_ref[...])

    for ref, val in zip(rows_scr, (decay, kk, kk * a, k2, r, v)):
        ref[...] = val
    o_scr[...] = jnp.zeros_like(o_scr)

    n_valid = jnp.minimum(block, seq_len - blk * block)
    group = SUBLANES if seq_len % SUBLANES == 0 else seq_len
    assert seq_len % group == 0 and group <= SUBLANES
    lane =lax.broadcasted_iota(I32, (HEAD_DIM, LANES), 1)
    n_pair = A_WIDTH // LANES

    def col_bcast(rowv):
        tr = jnp.broadcast_to(rowv, (LANES, LANES)).T
        return jnp.where(lane < HEAD_DIM, tr[0:HEAD_DIM], tr[HEAD_DIM:LANES])

    def token_group(gi, state):
        t0 = pl.multiple_of(gi * SUBLANES, SUBLANES)
        new_state = []
        for p in range(n_pair):
            sl = slice(p * LANES, (p + 1) * LANES)
            sp = state[p]
            tiles = [ref[pl.ds(t0, SUBLANES), sl] for ref in rows_scr]
            o_rows = []
            for i in range(group):
                w_b, kk_b, kka_b, k_b, r_b = (col_bcast(tiles[n][i:i + 1, :]) for n in range(5))
                v_row = tiles[5][i:i + 1, :]
                sa = jnp.sum(sp * kk_b, axis=0, keepdims=True)
                sp = sp * w_b - kka_b * sa + k_b * v_row
                o_rows.append(jnp.sum(sp * r_b, axis=0, keepdims=True))
            o_scr[pl.ds(t0, group), sl] = jnp.concatenate(o_rows, axis=0)
            new_state.append(sp)
        return tuple(new_state)

    state0 = tuple(s_scr[:, p * LANES:(p + 1) * LANES] for p in range(n_pair))
    state = lax.fori_loop(0, n_valid // group, token_group, state0)
    for p in range(n_pair):
        s_scr[:, p * LANES:(p + 1) * LANES] = state[p]
    sT_ref[0] = s_scr[...]

    o = o_scr[...]
    inv = 1.0 / HEAD_DIM
    mean = dot_hi(o, ones_blk) * inv
    cen = o - mean
    var = dot_hi(cen * cen, ones_blk) * inv
    o = cen * lax.rsqrt(var + GN_EPS) * lnw_ref[...] + lnb_ref[...]
    bonus = dot_hi(r * k2 * rk_ref[...], ones_blk) * v
    o_ref[0] = (o + bonus) * g


def _rwkv(pa, shift_prev, s0_t, params, ones_blk, seq_len, block):
    n_seq, length, _ = pa.shape
    n_blk = -(-seq_len // block)
    row_spec = lambda n: pl.BlockSpec((1, n), lambda s, i: (0, 0))
    mat_spec = lambda a: pl.BlockSpec(a.shape, lambda s, i: (0, 0))
    mu, w0, w2, a0, a2, g2, k_k, k_a, r_k, ln_w, ln_b = params
    kern = functools.partial(_rwkv_kernel, seq_len=seq_len, block=block)
    return pl.pallas_call(
        kern,
        grid=(n_seq, n_blk),
        in_specs=[pl.BlockSpec((1, block, A_PROJ), lambda s, i: (s, i, 0)),
                  pl.BlockSpec((1, 1, A_PROJ), lambda s, i: (s, 0, 0)),
                  pl.BlockSpec((1, HEAD_DIM, A_WIDTH), lambda s, i: (s, 0, 0)),
                  row_spec(A_PROJ), row_spec(A_WIDTH), mat_spec(w2), row_spec(A_WIDTH), mat_spec(a2),
                  mat_spec(g2), row_spec(A_WIDTH), row_spec(A_WIDTH), row_spec(A_WIDTH), row_spec(A_WIDTH),
                  row_spec(A_WIDTH), mat_spec(ones_blk)],
        out_specs=[pl.BlockSpec((1, block, A_WIDTH), lambda s, i: (s, i, 0)),
                   pl.BlockSpec((1, HEAD_DIM, A_WIDTH), lambda s, i: (s, 0, 0))],
        out_shape=[jax.ShapeDtypeStruct((n_seq, length, A_WIDTH), F32),
                   jax.ShapeDtypeStruct((n_seq, HEAD_DIM, A_WIDTH), F32)],
        scratch_shapes=[pltpu.VMEM((1, A_PROJ), F32), pltpu.VMEM((HEAD_DIM, A_WIDTH), F32)]
        + [pltpu.VMEM((block, A_WIDTH), F32)] * 7,
        compiler_params=_cparams(("arbitrary", "arbitrary")),
        name="rwkv",
    )(pa, shift_prev, s0_t, mu[None], w0[None], w2, a0[None], a2, g2, k_k[None], k_a[None], r_k[None],
      ln_w[None], ln_b[None], ones_blk)


def _sortable_key(s):
    bits = pltpu.bitcast(s + 0.0, I32)
    return bits ^ ((bits >> 31) & jnp.int32(0x7FFFFFFF))


def _index_scores(qi_b, wi, ki_b):
    s = None
    for h in range(IDX_HEADS):
        d = lax.dot_general(qi_b[:, h * IDX_DIM:(h + 1) * IDX_DIM], ki_b, (((1,), (1,)), ((), ())),
                            preferred_element_type=F32)
        t = jnp.maximum(d, 0.0) * wi[:, h:h + 1]
        s = t if s is None else s + t
    return s * IDX_SCALE


def _topk_threshold(count_fn, count_tie_fn, rows, topk):
    kf = jnp.float32(topk)
    c0 = count_fn(jnp.zeros((rows, 1), I32), False)
    prefix0 = jnp.where(c0 >= kf, jnp.int32(0), jnp.int32(INT_MIN))

    def bit_step(i, prefix):
        cand = prefix | lax.shift_left(jnp.int32(1), jnp.int32(30) - i)
        return jnp.where(count_fn(cand, False) >= kf, cand, prefix)

    tau = lax.fori_loop(0, 31, bit_step, prefix0)
    c_gt = count_fn(tau, True)
    c_ge = count_fn(tau, False)
    need = kf - c_gt
    excess = jnp.max(c_ge - c_gt - need)

    def tie_search(_):
        def jbit(i, jl):
            cand = jl | lax.shift_left(jnp.int32(1), jnp.int32(14) - i)
            return jnp.where(count_tie_fn(tau, cand) < need, cand, jl)
        return lax.fori_loop(0, 15, jbit, jnp.zeros((rows, 1), I32))

    jlim = lax.cond(excess > 0.0, tie_search, lambda _: jnp.full((rows, 1), 2 ** 30, I32), 0)
    return tau, jlim


def _select(keys, aidx, tau, jlim):
    return ((keys > tau) | ((keys == tau) & (aidx <= jlim))) & (keys > jnp.int32(NEG_INF_KEY))


def _softmax_step(state, qg, kt, vt, sel_rows, delta):
    m, l, acc = state
    lg = lax.dot_general(qg, kt, (((1,), (1,)), ((), ())), preferred_element_type=F32)
    if delta is not None:
        lg = lg + delta
    lg = jnp.where(sel_rows, lg, NEG_BIG)
    m_new = jnp.maximum(m, jnp.max(lg, axis=1, keepdims=True))
    alpha = jnp.exp(m - m_new)
    p = jnp.where(sel_rows, jnp.exp(lg - m_new), 0.0)
    l = alpha * l + jnp.sum(p, axis=1, keepdims=True)
    acc = alpha * acc + jnp.dot(p.astype(BF16), vt, preferred_element_type=F32)
    return m_new, l, acc


def _dsa_prompt_kernel(q_ref, qi_ref, kw_ref, kp_ref, vp_ref, kip_ref, dl_ref, o_ref, key_scr, *, topk):
    j = pl.program_id(0)
    rows = Q_TILE
    n_chunks = (j + 5) // 4
    qi_b = qi_ref[...].astype(BF16)
    wi = kw_ref[:, IDX_DIM:IDX_DIM + IDX_HEADS]
    qpos = j * Q_TILE + lax.broadcasted_iota(I32, (rows, 1), 0)
    lane_c = lax.broadcasted_iota(I32, (1, KEY_CHUNK), 1)

    def score_chunk(c, carry):
        a0 = pl.multiple_of(c * KEY_CHUNK, KEY_CHUNK)
        s = _index_scores(qi_b, wi, kip_ref[pl.ds(a0, KEY_CHUNK), :])
        kpos = a0 - KEY_FRONT_PAD + lane_c
        s = jnp.where((kpos <= qpos) & (kpos >= 0), s, -jnp.inf)
        key_scr[:, pl.ds(a0, KEY_CHUNK)] = _sortable_key(s)
        return carry

    lax.fori_loop(0, n_chunks, score_chunk, 0)

    def fold(m):
        out = m[:, 0:LANES]
        for i in range(1, KEY_CHUNK // LANES):
            out = out + m[:, i * LANES:(i + 1) * LANES]
        return out

    def count_fn(cand, strict):
        def body(c, acc):
            a0 = pl.multiple_of(c * KEY_CHUNK, KEY_CHUNK)
            kk = key_scr[:, pl.ds(a0, KEY_CHUNK)]
            hit = (kk > cand) if strict else (kk >= cand)
            return acc + fold(jnp.where(hit, 1.0, 0.0))
        acc = lax.fori_loop(0, n_chunks, body, jnp.zeros((rows, LANES), F32))
        return jnp.sum(acc, axis=1, keepdims=True)

    def count_tie_fn(tau, jl):
        def body(c, acc):
            a0 = pl.multiple_of(c * KEY_CHUNK, KEY_CHUNK)
            kk = key_scr[:, pl.ds(a0, KEY_CHUNK)]
            hit = (kk == tau) & ((a0 + lane_c) < jl)
            return acc + fold(jnp.where(hit, 1.0, 0.0))
        acc = lax.fori_loop(0, n_chunks, body, jnp.zeros((rows, LANES), F32))
        return jnp.sum(acc, axis=1, keepdims=True)

    tau, jlim = _topk_threshold(count_fn, count_tie_fn, rows, topk)

    q = q_ref[...] * (HEAD_DIM ** -0.5)
    qg = [jnp.concatenate([q[:, (g * B_REP + r) * HEAD_DIM:(g * B_REP + r + 1) * HEAD_DIM]
                           for r in range(B_REP)], axis=0).astype(BF16) for g in range(B_KV_HEADS)]
    grows = B_REP * rows

    def init_state():
        return (jnp.full((grows, 1), NEG_BIG, F32), jnp.zeros((grows, 1), F32), jnp.zeros((grows, HEAD_DIM), F32))

    def attend(a0, width, states, limit, delta):
        kk = key_scr[:, pl.ds(a0, width)]
        aidx = a0 + lax.broadcasted_iota(I32, (1, width), 1)
        sel = _select(kk, aidx, tau, jlim)
        if limit is not None:
            sel = sel & (aidx < limit)
        sel3 = jnp.concatenate([sel] * B_REP, axis=0)
        out = []
        for g in range(B_KV_HEADS):
            kt = kp_ref[pl.ds(a0, width), g * HEAD_DIM:(g + 1) * HEAD_DIM]
            vt = vp_ref[pl.ds(a0, width), g * HEAD_DIM:(g + 1) * HEAD_DIM]
            out.append(_softmax_step(states[g], qg[g], kt, vt, sel3, None if delta is None else delta[g]))
        return tuple(out)

    near0 = j * Q_TILE

    def far_chunk(c, states):
        return attend(pl.multiple_of(c * KEY_CHUNK, KEY_CHUNK), KEY_CHUNK, states, near0, None)

    states = lax.fori_loop(0, (j + 3) // 4, far_chunk, (init_state(), init_state()))
    states = attend(pl.multiple_of(near0, Q_TILE), 2 * Q_TILE, states, None, (dl_ref[0], dl_ref[1]))

    pieces = [None] * B_HEADS
    for g in range(B_KV_HEADS):
        _, l, acc = states[g]
        og = acc / l
        for r in range(B_REP):
            pieces[g * B_REP + r] = og[r * rows:(r + 1) * rows, :]
    o_ref[...] = jnp.concatenate(pieces, axis=1)


def _dsa_prompt(q, qi, kw, kp, vp, kip, delta, n_qblk, topk):
    nkp = kp.shape[0]
    full = lambda a: pl.BlockSpec(a.shape, lambda j: (0,) * a.ndim)
    return pl.pallas_call(
        functools.partial(_dsa_prompt_kernel, topk=topk),
        grid=(n_qblk,),
        in_specs=[pl.BlockSpec((Q_TILE, B_WIDTH), lambda j: (j, 0)),
                  pl.BlockSpec((Q_TILE, IDX_HEADS * IDX_DIM), lambda j: (j, 0)),
                  pl.BlockSpec((Q_TILE, LANES), lambda j: (j, 0)),
                  full(kp), full(vp), full(kip), full(delta)],
        out_specs=pl.BlockSpec((Q_TILE, B_WIDTH), lambda j: (j, 0)),
        out_shape=jax.ShapeDtypeStruct((n_qblk * Q_TILE, B_WIDTH), F32),
        scratch_shapes=[pltpu.VMEM((Q_TILE, nkp), I32)],
        compiler_params=_cparams(("arbitrary",)),
        name="dsa_prompt",
    )(q, qi, kw, kp, vp, kip, delta)


SROWS = 8


def _dsa_sample_score_kernel(pt_ref, qi_ref, kw_ref, cki_ref, kin_ref, key_ref, *, n_pages, dec_seq):
    p = pl.program_id(1)
    qi_b = qi_ref[0].astype(BF16)
    wi = kw_ref[0][:, IDX_DIM:IDX_DIM + IDX_HEADS]
    t = lax.broadcasted_iota(I32, (SROWS, 1), 0)
    c = lax.broadcasted_iota(I32, (1, PAGE_SIZE), 1)

    @pl.when(p < n_pages)
    def _():
        s = _index_scores(qi_b, wi, cki_ref[0].astype(BF16))
        key_ref[0] = _sortable_key(jnp.where(t < dec_seq, s, -jnp.inf))

    @pl.when(p == n_pages)
    def _():
        s = _index_scores(qi_b, wi, kin_ref[0].astype(BF16))
        key_ref[0] = _sortable_key(jnp.where((t < dec_seq) & (c <= t), s, -jnp.inf))


def _dsa_sample_attend_kernel(pt_ref, q_ref, key_ref, ck_ref, cv_ref, kn_ref, vn_ref, dl_ref, o_ref,
                              tau_scr, jl_scr, m_scr, l_scr, acc_scr, *, n_pages, topk):
    p = pl.program_id(1)
    grows = B_REP * SROWS

    @pl.when(p == 0)
    def _():
        keys = key_ref[0]
        aidx = lax.broadcasted_iota(I32, (1, keys.shape[1]), 1)

        def count_fn(cand, strict):
            hit = (keys > cand) if strict else (keys >= cand)
            return jnp.sum(jnp.where(hit, 1.0, 0.0), axis=1, keepdims=True)

        def count_tie_fn(tau, jl):
            return jnp.sum(jnp.where((keys == tau) & (aidx < jl), 1.0, 0.0), axis=1, keepdims=True)

        tau, jlim = _topk_threshold(count_fn, count_tie_fn, SROWS, topk)
        tau_scr[...] = tau
        jl_scr[...] = jlim
        m_scr[...] = jnp.full(m_scr.shape, NEG_BIG, F32)
        l_scr[...] = jnp.zeros(l_scr.shape, F32)
        acc_scr[...] = jnp.zeros(acc_scr.shape, F32)

    a0 = pl.multiple_of(p * PAGE_SIZE, PAGE_SIZE)
    kk = key_ref[0, :, pl.ds(a0, PAGE_SIZE)]
    aidx = a0 + lax.broadcasted_iota(I32, (1, PAGE_SIZE), 1)
    sel = _select(kk, aidx, tau_scr[...], jl_scr[...])
    sel3 = jnp.concatenate([sel] * B_REP, axis=0)
    q = q_ref[0] * (HEAD_DIM ** -0.5)

    def run(k_tile, v_tile, delta):
        for g in range(B_KV_HEADS):
            qg = jnp.concatenate([q[:, (g * B_REP + r) * HEAD_DIM:(g * B_REP + r + 1) * HEAD_DIM]
                                  for r in range(B_REP)], axis=0).astype(BF16)
            kt = k_tile[:, g * HEAD_DIM:(g + 1) * HEAD_DIM].astype(BF16)
            vt = v_tile[:, g * HEAD_DIM:(g + 1) * HEAD_DIM].astype(BF16)
            m, l, acc = _softmax_step((m_scr[g], l_scr[g], acc_scr[g]), qg, kt, vt, sel3,
                                      None if delta is None else delta[g])
            m_scr[g] = m
            l_scr[g] = l
            acc_scr[g] = acc

    @pl.when(p < n_pages - 1)
    def _():
        run(ck_ref[0], cv_ref[0], None)

    @pl.when(p == n_pages - 1)
    def _():
        run(ck_ref[0], cv_ref[0], (dl_ref[0, 0], dl_ref[0, 1]))

    @pl.when(p == n_pages)
    def _():
        run(kn_ref[0], vn_ref[0], (dl_ref[1, 0], dl_ref[1, 1]))
        pieces = [None] * B_HEADS
        for g in range(B_KV_HEADS):
            og = acc_scr[g] / l_scr[g]
            for r in range(B_REP):
                pieces[g * B_REP + r] = og[r * SROWS:(r + 1) * SROWS, :]
        o_ref[0] = jnp.concatenate(pieces, axis=1)


def _dsa_sample(page_table, q_s, qi_s, kw_s, cki, ck, cv, ki_new, k_new, v_new, delta_s, dec_seq, topk):
    nb = q_s.shape[0]
    n_pages = page_table.shape[1]
    n_keys = (n_pages + 1) * PAGE_SIZE
    page = lambda b, p, pt: (pt[b, jnp.minimum(p, n_pages - 1)], 0, 0)
    per_b = lambda b, p, pt: (b, 0, 0)
    keys = pl.pallas_call(
        functools.partial(_dsa_sample_score_kernel, n_pages=n_pages, dec_seq=dec_seq),
        grid_spec=pltpu.PrefetchScalarGridSpec(
            num_scalar_prefetch=1, grid=(nb, n_pages + 1),
            in_specs=[pl.BlockSpec((1, SROWS, IDX_HEADS * IDX_DIM), per_b),
                      pl.BlockSpec((1, SROWS, LANES), per_b),
                      pl.BlockSpec((1, PAGE_SIZE, IDX_DIM), page),
                      pl.BlockSpec((1, PAGE_SIZE, IDX_DIM), per_b)],
            out_specs=pl.BlockSpec((1, SROWS, PAGE_SIZE), lambda b, p, pt: (b, 0, p))),
        out_shape=jax.ShapeDtypeStruct((nb, SROWS, n_keys), I32),
        compiler_params=_cparams(("arbitrary", "arbitrary")),
        name="dsa_sample_score",
    )(page_table, qi_s, kw_s, cki, ki_new)
    grows = B_REP * SROWS
    return pl.pallas_call(
        functools.partial(_dsa_sample_attend_kernel, n_pages=n_pages, topk=topk),
        grid_spec=pltpu.PrefetchScalarGridSpec(
            num_scalar_prefetch=1, grid=(nb, n_pages + 1),
            in_specs=[pl.BlockSpec((1, SROWS, B_WIDTH), per_b),
                      pl.BlockSpec((1, SROWS, n_keys), per_b),
                      pl.BlockSpec((1, PAGE_SIZE, KV_WIDTH), page),
                      pl.BlockSpec((1, PAGE_SIZE, KV_WIDTH), page),
                      pl.BlockSpec((1, PAGE_SIZE, KV_WIDTH), per_b),
                      pl.BlockSpec((1, PAGE_SIZE, KV_WIDTH), per_b),
                      pl.BlockSpec(delta_s.shape, lambda b, p, pt: (0, 0, 0, 0))],
            out_specs=pl.BlockSpec((1, SROWS, B_WIDTH), per_b),
            scratch_shapes=[pltpu.VMEM((SROWS, 1), I32), pltpu.VMEM((SROWS, 1), I32),
                            pltpu.VMEM((B_KV_HEADS, grows, 1), F32), pltpu.VMEM((B_KV_HEADS, grows, 1), F32),
                            pltpu.VMEM((B_KV_HEADS, grows, HEAD_DIM), F32)]),
        out_shape=jax.ShapeDtypeStruct((nb, SROWS, B_WIDTH), F32),
        compiler_params=_cparams(("arbitrary", "arbitrary")),
        name="dsa_sample_attend",
    )(page_table, q_s, keys, ck, cv, k_new, v_new, delta_s)


def _pool_kernel(c_ref, halo_ref, w_ref, scale_ref, o_ref, *, block, offset):
    i = pl.program_id(1)
    c = c_ref[0]
    halo = jnp.where(i == 0, 0.0, halo_ref[0])
    full = jnp.concatenate([halo, c], axis=0)
    sums = [full]
    for sh in (1, 2, 4, 8):
        sums.append(sums[-1] + pltpu.roll(sums[-1], sh, axis=0))
    lane = lax.broadcasted_iota(I32, (block, C_WIDTH), 1)
    pos = i * block + lax.broadcasted_iota(I32, (block, C_WIDTH), 0) - offset
    win = jnp.zeros((block, C_WIDTH), F32)
    cnt = jnp.ones((block, C_WIDTH), F32)
    for gi, w in enumerate(POOL_WINDOWS):
        in_g = (lane >= gi * C_GROUP_DIM) & (lane < (gi + 1) * C_GROUP_DIM)
        win = jnp.where(in_g, sums[gi + 1][POOL_HALO:, :], win)
        cnt = jnp.where(in_g, jnp.clip(pos + 1, 1, w).astype(F32), cnt)
    d = win / cnt - c
    o_ref[0] = jnp.dot(d.astype(BF16), w_ref[...], preferred_element_type=F32) * scale_ref[...]


def _pool(c, w_bd, scale, block, offset):
    n_seq, length, _ = c.shape
    ratio = block // POOL_HALO
    return pl.pallas_call(
        functools.partial(_pool_kernel, block=block, offset=offset),
        grid=(n_seq, length // block),
        in_specs=[pl.BlockSpec((1, block, C_WIDTH), lambda s, i: (s, i, 0)),
                  pl.BlockSpec((1, POOL_HALO, C_WIDTH), lambda s, i: (s, jnp.maximum(i * ratio - 1, 0), 0)),
                  pl.BlockSpec((C_WIDTH, C_WIDTH), lambda s, i: (0, 0)),
                  pl.BlockSpec((1, C_WIDTH), lambda s, i: (0, 0))],
        out_specs=pl.BlockSpec((1, block, C_WIDTH), lambda s, i: (s, i, 0)),
        out_shape=jax.ShapeDtypeStruct(c.shape, F32),
        compiler_params=_cparams(("parallel", "parallel")),
        name="pool",
    )(c, c, w_bd, scale)


def _outproj_kernel(x_ref, oa_ref, ob_ref, oc_ref, wo_ref, g_ref, wr_ref, br_ref, xo_ref, h_ref, gm_ref):
    mix = jnp.dot(oa_ref[...].astype(BF16), wo_ref[0:A_WIDTH, :], preferred_element_type=F32)
    mix += jnp.dot(ob_ref[...].astype(BF16), wo_ref[A_WIDTH:A_WIDTH + B_WIDTH, :], preferred_element_type=F32)
    mix += jnp.dot(oc_ref[...].astype(BF16), wo_ref[A_WIDTH + B_WIDTH:, :], preferred_element_type=F32)
    x = x_ref[...] + mix
    xo_ref[...] = x
    h = _rms(x, g_ref[...])
    h_ref[...] = h.astype(BF16)
    logits = jnp.dot(h, wr_ref[...], precision=HIGHEST, preferred_element_type=F32) + br_ref[...]
    lane = lax.broadcasted_iota(I32, logits.shape, 1)
    work = logits
    vals, hots = [], []
    for _ in range(TOP_K):
        m = jnp.max(work, axis=1, keepdims=True)
        idx = jnp.min(jnp.where(work == m, lane, N_EXPERTS), axis=1, keepdims=True)
        hot = lane == idx
        vals.append(m)
        hots.append(hot)
        work = jnp.where(hot, -jnp.inf, work)
    es = [jnp.exp(v - vals[0]) for v in vals]
    den = es[0] + es[1] + es[2] + es[3]
    gm = jnp.zeros(logits.shape, F32)
    for e, hot in zip(es, hots):
        gm = jnp.where(hot, e / den, gm)
    gm_ref[...] = gm


def _outproj(x, oa, ob, oc, wo, g, wr, br):
    nt = x.shape[0]
    tile = lambda n: pl.BlockSpec((TOK_TILE, n), lambda i: (i, 0))
    full = lambda a: pl.BlockSpec(a.shape, lambda i: (0, 0))
    return pl.pallas_call(
        _outproj_kernel,
        grid=(nt // TOK_TILE,),
        in_specs=[tile(D_MODEL), tile(A_WIDTH), tile(B_WIDTH), tile(C_WIDTH), full(wo), full(g), full(wr), full(br)],
        out_specs=[tile(D_MODEL), tile(D_MODEL), tile(N_EXPERTS)],
        out_shape=[jax.ShapeDtypeStruct((nt, D_MODEL), F32), jax.ShapeDtypeStruct((nt, D_MODEL), BF16),
                   jax.ShapeDtypeStruct((nt, N_EXPERTS), F32)],
        compiler_params=_cparams(("parallel",)),
        name="outproj",
    )(x, oa, ob, oc, wo, g, wr, br)


def _moe_kernel(h_ref, gm_ref, x_ref, w1_ref, b1_ref, w2_ref, b2_ref, gf_ref, o_ref, acc_scr, *, final):
    e = pl.program_id(1)

    @pl.when(e == 0)
    def _():
        acc_scr[...] = jnp.zeros_like(acc_scr)

    u = jnp.dot(h_ref[...], w1_ref[0].astype(BF16), preferred_element_type=F32) + b1_ref[0]
    glu = jnp.minimum(u[:, :D_FF], SWIGLU_LIMIT)
    lin = jnp.clip(u[:, D_FF:], -SWIGLU_LIMIT, SWIGLU_LIMIT)
    act = (glu * _sigmoid(SWIGLU_ALPHA * glu) * (lin + 1.0)).astype(BF16)
    out = jnp.dot(act, w2_ref[0].astype(BF16), preferred_element_type=F32) + b2_ref[0]
    gm = gm_ref[...]
    lane = lax.broadcasted_iota(I32, gm.shape, 1)
    gate = jnp.sum(jnp.where(lane == e, gm, 0.0), axis=1, keepdims=True)
    acc_scr[...] += gate * out

    @pl.when(e == N_EXPERTS - 1)
    def _():
        x = x_ref[...] + acc_scr[...]
        if final:
            x = _rms(x, gf_ref[...])
        o_ref[...] = x


def _moe(h, gm, x, w1, b1, w2, b2, gf, final):
    nt = x.shape[0]
    tile = lambda n: pl.BlockSpec((TOK_TILE, n), lambda i, e: (i, 0))
    return pl.pallas_call(
        functools.partial(_moe_kernel, final=final),
        grid=(nt // TOK_TILE, N_EXPERTS),
        in_specs=[tile(D_MODEL), tile(N_EXPERTS), tile(D_MODEL),
                  pl.BlockSpec((1, D_MODEL, 2 * D_FF), lambda i, e: (e, 0, 0)),
                  pl.BlockSpec((1, 1, 2 * D_FF), lambda i, e: (e, 0, 0)),
                  pl.BlockSpec((1, D_FF, D_MODEL), lambda i, e: (e, 0, 0)),
                  pl.BlockSpec((1, 1, D_MODEL), lambda i, e: (e, 0, 0)),
                  pl.BlockSpec((1, D_MODEL), lambda i, e: (0, 0))],
        out_specs=tile(D_MODEL),
        out_shape=jax.ShapeDtypeStruct((nt, D_MODEL), F32),
        scratch_shapes=[pltpu.VMEM((TOK_TILE, D_MODEL), F32)],
        compiler_params=_cparams(("parallel", "arbitrary")),
        name="moe",
    )(h, gm, x, w1, b1[:, None, :], w2, b2[:, None, :], gf)


def _rel_bucket(dist):
    n = jnp.maximum(dist, 0)
    max_exact = N_BUCKETS // 2
    log_ratio = jnp.log(jnp.maximum(n, 1).astype(jnp.float32) / max_exact) / math.log(MAX_DISTANCE / max_exact)
    large = jnp.minimum(max_exact + (log_ratio * (N_BUCKETS - max_exact)).astype(jnp.int32), N_BUCKETS - 1)
    return jnp.where(n < max_exact, n, large)


def _bias_delta(rel_bias, dist):
    rb = rel_bias.astype(F32)
    d = rb[_rel_bucket(dist)] - rb[N_BUCKETS - 1]
    d = jnp.where(((dist >= 0) & (dist < MAX_DISTANCE))[..., None], d, 0.0)
    return jnp.moveaxis(d, -1, 0)


def _group_rows(delta, rows):
    return delta.reshape(B_KV_HEADS, B_REP * rows, delta.shape[-1])


def kernel(x_prompt, x_sample, cache_k, cache_v, cache_kidx, state_wkv, state_shift, state_pool, page_table,
           meta_tokens, rel_bias, norm_mix, norm_ffn, norm_final, w_in, w_out,
           a_mu, a_w0, a_w2, a_a0, a_a2, a_g2, a_kk, a_ka, a_rk, a_ln_w, a_ln_b,
           c_w, c_scale, moe_w_router, moe_b_router, moe_w1, moe_b1, moe_w2, moe_b2):
    assert x_prompt.shape[0] == 1
    depth = w_in.shape[0]
    seq = x_prompt.shape[1]
    nb, dec_seq = x_sample.shape[:2]
    assert dec_seq <= SROWS
    t_p = seq + N_META
    n_s = nb * dec_seq
    n_qblk = -(-t_p // Q_TILE)
    nkp = _round_up(Q_TILE * (n_qblk + 1), KEY_CHUNK)
    nt = max(_round_up(t_p + n_s, TOK_TILE), nkp)
    n_pages = page_table.shape[1]
    past = n_pages * PAGE_SIZE
    topk_p = min(TOPK_MAX, seq // 4)
    topk_s = min(TOPK_MAX, (past + dec_seq) // 4)
    n_phys = cache_k.shape[1]

    x = jnp.concatenate([meta_tokens.astype(F32), x_prompt[0], x_sample.reshape(n_s, D_MODEL),
                         jnp.zeros((nt - t_p - n_s, D_MODEL), F32)], axis=0)

    ones_blk = jnp.kron(jnp.eye(A_HEADS, dtype=F32), jnp.ones((HEAD_DIM, HEAD_DIM), F32))
    qi_ = jnp.arange(Q_TILE)[:, None]
    cj_ = jnp.arange(2 * Q_TILE)[None, :]
    delta_p = _group_rows(_bias_delta(rel_bias, Q_TILE + qi_ - cj_), Q_TILE)
    ts_ = jnp.arange(SROWS)[:, None]
    cs_ = jnp.arange(PAGE_SIZE)[None, :]
    delta_s = jnp.stack([_group_rows(_bias_delta(rel_bias, ts_ + PAGE_SIZE - cs_), SROWS),
                         _group_rows(_bias_delta(rel_bias, ts_ - cs_), SROWS)])

    outs = {n: [] for n in ("kp", "vp", "kip", "wkvp", "shp", "plp", "ks", "vs", "kis", "wkvs", "shs", "pls")}
    for l in range(depth):
        wl = w_in[l]
        b0 = A_PROJ
        w_cat = jnp.concatenate([
            wl[:, :A_PROJ], wl[:, b0:b0 + B_WIDTH + 2 * KV_WIDTH + IDX_HEADS * IDX_DIM],
            wl[:, b0 + B_PROJ - IDX_DIM - IDX_HEADS:b0 + B_PROJ],
            jnp.zeros((D_MODEL, LANES - IDX_DIM - IDX_HEADS), F32), wl[:, b0 + B_PROJ:]], axis=1).astype(BF16)
        pa, q, k, v, qi, kw, pc = _inproj(x, norm_mix[l][None], w_cat)

        sl_p = slice(0, t_p)
        sl_s = slice(t_p, t_p + n_s)
        outs["kp"].append(k[sl_p].reshape(1, t_p, B_KV_HEADS, HEAD_DIM))
        outs["vp"].append(v[sl_p].reshape(1, t_p, B_KV_HEADS, HEAD_DIM))
        outs["kip"].append(kw[sl_p, :IDX_DIM][None])
        outs["ks"].append(k[sl_s].reshape(nb, dec_seq, B_KV_HEADS, HEAD_DIM))
        outs["vs"].append(v[sl_s].reshape(nb, dec_seq, B_KV_HEADS, HEAD_DIM))
        outs["kis"].append(kw[sl_s, :IDX_DIM].reshape(nb, dec_seq, IDX_DIM))
        outs["shp"].append(pa[t_p - 1:t_p])
        pa_s = pa[sl_s].reshape(nb, dec_seq, A_PROJ)
        outs["shs"].append(pa_s[:, -1])
        pc_s = pc[sl_s].reshape(nb, dec_seq, C_WIDTH)
        outs["plp"].append(pc[t_p - POOL_BUF:t_p][None])
        pool_full = jnp.concatenate([state_pool[l], pc_s], axis=1)
        outs["pls"].append(pool_full[:, -POOL_BUF:])

        rw = (a_mu[l], a_w0[l], a_w2[l], a_a0[l], a_a2[l], a_g2[l], a_kk[l], a_ka[l], a_rk[l], a_ln_w[l], a_ln_b[l])
        oa_p, st_p = _rwkv(pa[None], jnp.zeros((1, 1, A_PROJ), F32), jnp.zeros((1, HEAD_DIM, A_WIDTH), F32),
                           rw, ones_blk, t_p, RWKV_BLOCK)
        pa_s8 = jnp.pad(pa_s, ((0, 0), (0, SROWS - dec_seq), (0, 0)))
        s0_s = state_wkv[l].astype(F32).transpose(0, 3, 1, 2).reshape(nb, HEAD_DIM, A_WIDTH)
        oa_s, st_s = _rwkv(pa_s8, state_shift[l][:, None, :], s0_s, rw, ones_blk, dec_seq, SROWS)
        untr = lambda s: s.reshape(-1, HEAD_DIM, A_HEADS, HEAD_DIM).transpose(0, 2, 3, 1)
        outs["wkvp"].append(untr(st_p).astype(state_wkv.dtype))
        outs["wkvs"].append(untr(st_s).astype(state_wkv.dtype))

        front = lambda a: jnp.concatenate([jnp.zeros((KEY_FRONT_PAD, a.shape[1]), a.dtype),
                                           a[:nkp - KEY_FRONT_PAD]], axis=0).astype(BF16)
        ob_p = _dsa_prompt(q, qi, kw, front(k), front(v), front(kw[:, :IDX_DIM]), delta_p, n_qblk, topk_p)
        rows8 = lambda a: jnp.pad(a[sl_s].reshape(nb, dec_seq, -1), ((0, 0), (0, SROWS - dec_seq), (0, 0)))
        page8 = lambda a: jnp.pad(a[sl_s].reshape(nb, dec_seq, -1), ((0, 0), (0, PAGE_SIZE - dec_seq), (0, 0)))
        ob_s = _dsa_sample(page_table, rows8(q), rows8(qi), rows8(kw),
                           cache_kidx[l], cache_k[l].reshape(n_phys, PAGE_SIZE, KV_WIDTH),
                           cache_v[l].reshape(n_phys, PAGE_SIZE, KV_WIDTH),
                           page8(kw[:, :IDX_DIM]), page8(k), page8(v), delta_s, dec_seq, topk_s)

        w_bd = jax.scipy.linalg.block_diag(*[c_w[l][gi] for gi in range(len(POOL_WINDOWS))]).astype(BF16)
        oc_p = _pool(pc[None], w_bd, c_scale[l][None], TOK_TILE, 0)
        pool_in = jnp.concatenate([jnp.zeros((nb, 1, C_WIDTH), F32), pool_full,
                                   jnp.zeros((nb, 2 * POOL_HALO - 1 - POOL_BUF - dec_seq, C_WIDTH), F32)], axis=1)
        oc_s = _pool(pool_in, w_bd, c_scale[l][None], POOL_HALO, 1)[:, POOL_HALO:POOL_HALO + dec_seq]

        def merge(p_rows, s_rows):
            w = p_rows.shape[-1]
            return jnp.concatenate([p_rows[:t_p], s_rows.reshape(n_s, w), jnp.zeros((nt - t_p - n_s, w), F32)], axis=0)

        oa = merge(oa_p[0], oa_s[:, :dec_seq])
        ob = merge(ob_p, ob_s[:, :dec_seq])
        oc = merge(oc_p[0], oc_s)
        x, h2, gm = _outproj(x, oa, ob, oc, w_out[l].astype(BF16), norm_ffn[l][None], moe_w_router[l],
                             moe_b_router[l][None])
        x = _moe(h2, gm, x, moe_w1[l], moe_b1[l], moe_w2[l], moe_b2[l], norm_final[None], l == depth - 1)

    y_prompt = x[N_META:t_p][None]
    y_sample = x[t_p:t_p + n_s].reshape(nb, dec_seq, D_MODEL)
    st = lambda n: jnp.stack(outs[n])
    return (y_prompt, y_sample, st("kp"), st("vp"), st("kip"), st("wkvp"), st("shp"), st("plp"),
            st("ks"), st("vs"), st("kis"), st("wkvs"), st("shs"), st("pls"))
```

```python
import functools
import math

import jax
import jax.numpy as jnp
import numpy as np
from jax import lax
from jax.experimental import pallas as pl
from jax.experimental.pallas import tpu as pltpu

F32 = jnp.float32
BF16 = jnp.bfloat16
I32 = jnp.int32
I16 = jnp.int16
HIGHEST = lax.Precision.HIGHEST

D_MODEL = 1024
N_META = 16
HEAD_DIM = 64
A_WIDTH = 384
A_HEADS = 6
LORA_W, LORA_A, LORA_G = 64, 64, 128
A_PROJ = 3 * A_WIDTH + LORA_W + LORA_A + LORA_G
GN_EPS = 64e-5
B_WIDTH = 384
B_HEADS = 6
B_KV_HEADS = 2
B_REP = 3
KV_WIDTH = 128
IDX_HEADS = 4
IDX_DIM = 64
IDX_SCALE = (IDX_HEADS * IDX_DIM) ** -0.5
TOPK_MAX = 256
B_PROJ = B_WIDTH + 2 * KV_WIDTH + IDX_HEADS * IDX_DIM + IDX_DIM + IDX_HEADS
C_WIDTH = 256
C_GROUP_DIM = 64
POOL_WINDOWS = (2, 4, 8, 16)
POOL_BUF = 15
N_BUCKETS = 32
MAX_DISTANCE = 128
N_EXPERTS = 32
TOP_K = 4
D_FF = 1024
SWIGLU_LIMIT = 7.0
SWIGLU_ALPHA = 1.702
RMS_EPS = 1e-5
PAGE_SIZE = 128

LANES = 128
SUBLANES = 8
Q_TILE = 128
KEY_CHUNK = 512
KEY_FRONT_PAD = 128
TAIL_WIDTH = 640
SAMPLE_PAGES = 16
TOK_TILE = 512
RWKV_BLOCK = 128
POOL_HALO = 16
VMEM_LIMIT = 56 * 1024 * 1024

INT_MIN = -(2 ** 31)
NEG_INF_KEY = np.int32(np.uint32(0xFF800000) ^ np.uint32(0x7FFFFFFF))
NEG_BIG = -1e30


def _round_up(x, m):
    return (x + m - 1) // m * m


def _cparams(sem):
    return pltpu.CompilerParams(dimension_semantics=sem, vmem_limit_bytes=VMEM_LIMIT)


def _rms(x, g):
    return x * lax.rsqrt(jnp.mean(x * x, axis=-1, keepdims=True) + RMS_EPS) * g


def _sigmoid(x):
    return 1.0 / (1.0 + jnp.exp(-x))


IN_COLS = (A_PROJ, B_WIDTH, KV_WIDTH, KV_WIDTH, IDX_HEADS * IDX_DIM, LANES, C_WIDTH)


def _inproj_kernel(x_ref, g_ref, w_ref, *out_refs):
    hb = _rms(x_ref[...], g_ref[...]).astype(BF16)
    off = 0
    for ref in out_refs:
        n = ref.shape[1]
        ref[...] = jnp.dot(hb, w_ref[:, off:off + n], preferred_element_type=F32)
        off += n


def _inproj(x, g, w):
    nt = x.shape[0]
    return pl.pallas_call(
        _inproj_kernel,
        grid=(nt // TOK_TILE,),
        in_specs=[pl.BlockSpec((TOK_TILE, D_MODEL), lambda i: (i, 0)),
                  pl.BlockSpec((1, D_MODEL), lambda i: (0, 0)),
                  pl.BlockSpec(w.shape, lambda i: (0, 0))],
        out_specs=[pl.BlockSpec((TOK_TILE, n), lambda i: (i, 0)) for n in IN_COLS],
        out_shape=[jax.ShapeDtypeStruct((nt, n), F32) for n in IN_COLS],
        compiler_params=_cparams(("parallel",)),
        name="inproj",
    )(x, g, w)


def _rwkv_kernel(pa_ref, sh_ref, s0_ref, mu_ref, w0_ref, w2_ref, a0_ref, a2_ref, g2_ref, kk_ref, ka_ref,
                 rk_ref, lnw_ref, lnb_ref, ones_ref, o_ref, sT_ref, carry_scr, s_scr, o_scr, *rows_scr,
                 seq_len, block):
    blk = pl.program_id(1)

    @pl.when(blk == 0)
    def _():
        carry_scr[...] = sh_ref[0]
        s_scr[...] = s0_ref[0]

    pa = pa_ref[0]
    row = lax.broadcasted_iota(I32, (block, 1), 0)
    prev = jnp.where(row == 0, carry_scr[...], pltpu.roll(pa, 1, axis=0))
    carry_scr[...] = pa[block - 1:block, :]
    xm = pa + (prev - pa) * mu_ref[...]
    r = xm[:, 0:A_WIDTH]
    k = xm[:, A_WIDTH:2 * A_WIDTH]
    v = xm[:, 2 * A_WIDTH:3 * A_WIDTH]
    lw = xm[:, 3 * A_WIDTH:3 * A_WIDTH + LORA_W]
    la = xm[:, 3 * A_WIDTH + LORA_W:3 * A_WIDTH + LORA_W + LORA_A]
    lg = xm[:, 3 * A_WIDTH + LORA_W + LORA_A:]

    def dot_hi(a, b):
        return jnp.dot(a, b, precision=HIGHEST, preferred_element_type=F32)

    z = -(w0_ref[...] + dot_hi(jnp.tanh(lw), w2_ref[...]))
    softplus = jnp.maximum(z, 0.0) + jnp.log(1.0 + jnp.exp(-jnp.abs(z)))
    w_log = -softplus - 0.5
    decay = jnp.exp(-jnp.exp(w_log))
    a = _sigmoid(a0_ref[...] + dot_hi(la, a2_ref[...]))
    g = dot_hi(_sigmoid(lg), g2_ref[...])
    ones_blk = ones_ref[...]
    kk = k * kk_ref[...]
    kk = kk / jnp.maximum(jnp.sqrt(dot_hi(kk * kk, ones_blk)), 1e-12)
    k2 = k * (1.0 + (a - 1.0) * ka_ref[...])

    for ref, val in zip(rows_scr, (decay, kk, kk * a, k2, r, v)):
        ref[...] = val
    o_scr[...] = jnp.zeros_like(o_scr)

    n_valid = jnp.minimum(block, seq_len - blk * block)
    group = SUBLANES if seq_len % SUBLANES == 0 else seq_len
    assert seq_len % group == 0 and group <= SUBLANES
    lane =lax.broadcasted_iota(I32, (HEAD_DIM, LANES), 1)
    n_pair = A_WIDTH // LANES

    def col_bcast(rowv):
        tr = jnp.broadcast_to(rowv, (LANES, LANES)).T
        return jnp.where(lane < HEAD_DIM, tr[0:HEAD_DIM], tr[HEAD_DIM:LANES])

    def token_group(gi, state):
        t0 = pl.multiple_of(gi * SUBLANES, SUBLANES)
        new_state = []
        for p in range(n_pair):
            sl = slice(p * LANES, (p + 1) * LANES)
            sp = state[p]
            tiles = [ref[pl.ds(t0, SUBLANES), sl] for ref in rows_scr]
            o_rows = []
            for i in range(group):
                w_b, kk_b, kka_b, k_b, r_b = (col_bcast(tiles[n][i:i + 1, :]) for n in range(5))
                v_row = tiles[5][i:i + 1, :]
                sa = jnp.sum(sp * kk_b, axis=0, keepdims=True)
                sp = sp * w_b - kka_b * sa + k_b * v_row
                o_rows.append(jnp.sum(sp * r_b, axis=0, keepdims=True))
            o_scr[pl.ds(t0, group), sl] = jnp.concatenate(o_rows, axis=0)
            new_state.append(sp)
        return tuple(new_state)

    state0 = tuple(s_scr[:, p * LANES:(p + 1) * LANES] for p in range(n_pair))
    state = lax.fori_loop(0, n_valid // group, token_group, state0)
    for p in range(n_pair):
        s_scr[:, p * LANES:(p + 1) * LANES] = state[p]
    sT_ref[0] = s_scr[...]

    o = o_scr[...]
    inv = 1.0 / HEAD_DIM
    mean = dot_hi(o, ones_blk) * inv
    cen = o - mean
    var = dot_hi(cen * cen, ones_blk) * inv
    o = cen * lax.rsqrt(var + GN_EPS) * lnw_ref[...] + lnb_ref[...]
    bonus = dot_hi(r * k2 * rk_ref[...], ones_blk) * v
    o_ref[0] = (o + bonus) * g


def _rwkv(pa, shift_prev, s0_t, params, ones_blk, seq_len, block):
    n_seq, length, _ = pa.shape
    n_blk = -(-seq_len // block)
    row_spec = lambda n: pl.BlockSpec((1, n), lambda s, i: (0, 0))
    mat_spec = lambda a: pl.BlockSpec(a.shape, lambda s, i: (0, 0))
    mu, w0, w2, a0, a2, g2, k_k, k_a, r_k, ln_w, ln_b = params
    kern = functools.partial(_rwkv_kernel, seq_len=seq_len, block=block)
    return pl.pallas_call(
        kern,
        grid=(n_seq, n_blk),
        in_specs=[pl.BlockSpec((1, block, A_PROJ), lambda s, i: (s, i, 0)),
                  pl.BlockSpec((1, 1, A_PROJ), lambda s, i: (s, 0, 0)),
                  pl.BlockSpec((1, HEAD_DIM, A_WIDTH), lambda s, i: (s, 0, 0)),
                  row_spec(A_PROJ), row_spec(A_WIDTH), mat_spec(w2), row_spec(A_WIDTH), mat_spec(a2),
                  mat_spec(g2), row_spec(A_WIDTH), row_spec(A_WIDTH), row_spec(A_WIDTH), row_spec(A_WIDTH),
                  row_spec(A_WIDTH), mat_spec(ones_blk)],
        out_specs=[pl.BlockSpec((1, block, A_WIDTH), lambda s, i: (s, i, 0)),
                   pl.BlockSpec((1, HEAD_DIM, A_WIDTH), lambda s, i: (s, 0, 0))],
        out_shape=[jax.ShapeDtypeStruct((n_seq, length, A_WIDTH), F32),
                   jax.ShapeDtypeStruct((n_seq, HEAD_DIM, A_WIDTH), F32)],
        scratch_shapes=[pltpu.VMEM((1, A_PROJ), F32), pltpu.VMEM((HEAD_DIM, A_WIDTH), F32)]
        + [pltpu.VMEM((block, A_WIDTH), F32)] * 7,
        compiler_params=_cparams(("arbitrary", "arbitrary")),
        name="rwkv",
    )(pa, shift_prev, s0_t, mu[None], w0[None], w2, a0[None], a2, g2, k_k[None], k_a[None], r_k[None],
      ln_w[None], ln_b[None], ones_blk)


def _sortable_key(s):
    bits = pltpu.bitcast(s + 0.0, I32)
    return bits ^ ((bits >> 31) & jnp.int32(0x7FFFFFFF))


def _index_scores(qi_b, wi, ki_b):
    s = None
    for h in range(IDX_HEADS):
        d = lax.dot_general(qi_b[:, h * IDX_DIM:(h + 1) * IDX_DIM], ki_b, (((1,), (1,)), ((), ())),
                            preferred_element_type=F32)
        t = jnp.maximum(d, 0.0) * wi[:, h:h + 1]
        s = t if s is None else s + t
    return s * IDX_SCALE


def _topk_threshold(count_fn, count_tie_fn, rows, topk):
    kf = jnp.float32(topk)
    c0 = count_fn(jnp.zeros((rows, 1), I32), False)
    prefix0 = jnp.where(c0 >= kf, jnp.int32(0), jnp.int32(INT_MIN))

    def bit_step(i, prefix):
        cand = prefix | lax.shift_left(jnp.int32(1), jnp.int32(30) - i)
        return jnp.where(count_fn(cand, False) >= kf, cand, prefix)

    tau = lax.fori_loop(0, 31, bit_step, prefix0)
    c_gt = count_fn(tau, True)
    c_ge = count_fn(tau, False)
    need = kf - c_gt
    excess = jnp.max(c_ge - c_gt - need)

    def tie_search(_):
        def jbit(i, jl):
            cand = jl | lax.shift_left(jnp.int32(1), jnp.int32(14) - i)
            return jnp.where(count_tie_fn(tau, cand) < need, cand, jl)
        return lax.fori_loop(0, 15, jbit, jnp.zeros((rows, 1), I32))

    jlim = lax.cond(excess > 0.0, tie_search, lambda _: jnp.full((rows, 1), 2 ** 30, I32), 0)
    return tau, jlim


def _select(keys, aidx, tau, jlim):
    return ((keys > tau) | ((keys == tau) & (aidx <= jlim))) & (keys > jnp.int32(NEG_INF_KEY))


def _softmax_step(state, qg, kt, vt, sel_rows, delta):
    m, l, acc = state
    lg = lax.dot_general(qg, kt, (((1,), (1,)), ((), ())), preferred_element_type=F32)
    if delta is not None:
        lg = lg + delta
    lg = jnp.where(sel_rows, lg, NEG_BIG)
    m_new = jnp.maximum(m, jnp.max(lg, axis=1, keepdims=True))
    alpha = jnp.exp(m - m_new)
    p = jnp.where(sel_rows, jnp.exp(lg - m_new), 0.0)
    l = alpha * l + jnp.sum(p, axis=1, keepdims=True)
    acc = alpha * acc + jnp.dot(p.astype(BF16), vt, preferred_element_type=F32)
    return m_new, l, acc


def _dsa_prompt_kernel(q_ref, qi_ref, kw_ref, kp_ref, vp_ref, kip_ref, dl_ref, o_ref, key_scr, half_scr, *, topk):
    j = pl.program_id(0)
    rows = Q_TILE
    n_chunks = (j + 5) // 4
    qi_b = qi_ref[...].astype(BF16)
    wi = kw_ref[:, IDX_DIM:IDX_DIM + IDX_HEADS]
    qpos = j * Q_TILE + lax.broadcasted_iota(I32, (rows, 1), 0)
    lane_c = lax.broadcasted_iota(I32, (1, KEY_CHUNK), 1)

    def score_chunk(c, carry):
        a0 = pl.multiple_of(c * KEY_CHUNK, KEY_CHUNK)
        s = _index_scores(qi_b, wi, kip_ref[pl.ds(a0, KEY_CHUNK), :])
        kpos = a0 - KEY_FRONT_PAD + lane_c
        s = jnp.where((kpos <= qpos) & (kpos >= 0), s, -jnp.inf)
        key = _sortable_key(s)
        key_scr[:, pl.ds(a0, KEY_CHUNK)] = key
        half_scr[:, pl.ds(a0, KEY_CHUNK)] = (key >> 16).astype(I16)
        return carry

    lax.fori_loop(0, n_chunks, score_chunk, 0)

    key_scr[:, pl.ds(pl.multiple_of(n_chunks * KEY_CHUNK, KEY_CHUNK), KEY_CHUNK)] = jnp.full(
        (rows, KEY_CHUNK), NEG_INF_KEY, I32)

    n_fold = KEY_CHUNK // LANES

    def chunk_at(ref, c):
        return ref[:, pl.ds(pl.multiple_of(c * KEY_CHUNK, KEY_CHUNK), KEY_CHUNK)]

    def lane_total(acc):
        return jnp.sum(acc.astype(F32), axis=1, keepdims=True)

    def count32(cand, strict):
        def body(c, acc):
            kk = chunk_at(key_scr, c)
            hit = (kk > cand) if strict else (kk >= cand)
            m = jnp.where(hit, 1.0, 0.0)
            for i in range(n_fold):
                acc = acc + m[:, i * LANES:(i + 1) * LANES]
            return acc
        return lane_total(lax.fori_loop(0, n_chunks, body, jnp.zeros((rows, LANES), F32)))

    def count16(cand, below):
        cand_t = jnp.broadcast_to(cand, (rows, LANES)).astype(I16)

        def body(c, acc):
            kk = chunk_at(half_scr, c)
            for i in range(n_fold):
                t = kk[:, i * LANES:(i + 1) * LANES]
                hit = (t < cand_t) if below else (t >= cand_t)
                acc = acc + jnp.where(hit, jnp.int16(1), jnp.int16(0))
            return acc
        return lane_total(lax.fori_loop(0, n_chunks, body, jnp.zeros((rows, LANES), I16)))

    def greedy16(count_ok, start):
        def step(i, prefix):
            cand = prefix | lax.shift_left(jnp.int32(1), jnp.int32(14) - i)
            return jnp.where(count_ok(cand), cand, prefix)
        return lax.fori_loop(0, 15, step, start)

    kf = jnp.float32(topk)
    i16_min = jnp.int32(-(2 ** 15))

    ok_hi = lambda cand: count16(cand, False) >= kf
    tau_hi = greedy16(ok_hi, jnp.where(ok_hi(jnp.zeros((rows, 1), I32)), jnp.int32(0), i16_min))

    def to_low(c, carry):
        kk = chunk_at(key_scr, c)
        hi = kk >> 16
        low = (kk & jnp.int32(0xFFFF)) - 2 ** 15
        red = jnp.where(hi == tau_hi, low, jnp.where(hi > tau_hi, jnp.int32(2 ** 15 - 1), i16_min))
        half_scr[:, pl.ds(pl.multiple_of(c * KEY_CHUNK, KEY_CHUNK), KEY_CHUNK)] = red.astype(I16)
        return carry

    lax.fori_loop(0, n_chunks, to_low, 0)
    ok_lo = lambda cand: count16(cand, False) >= kf
    tau_lo = greedy16(ok_lo, jnp.where(ok_lo(jnp.zeros((rows, 1), I32)), jnp.int32(0), i16_min))
    tau = lax.shift_left(tau_hi, 16) | (tau_lo + 2 ** 15)

    c_gt = count32(tau, True)
    c_ge = count32(tau, False)
    need = kf - c_gt
    excess = jnp.max(c_ge - c_gt - need)

    def tie_search(_):
        def to_tie(c, carry):
            a0 = pl.multiple_of(c * KEY_CHUNK, KEY_CHUNK)
            kk = key_scr[:, pl.ds(a0, KEY_CHUNK)]
            half_scr[:, pl.ds(a0, KEY_CHUNK)] = jnp.where(kk == tau, a0 + lane_c, jnp.int32(2 ** 15 - 1)).astype(I16)
            return carry
        lax.fori_loop(0, n_chunks, to_tie, 0)
        return greedy16(lambda cand: count16(cand, True) < need, jnp.zeros((rows, 1), I32))

    jlim = lax.cond(excess > 0.0, tie_search, lambda _: jnp.full((rows, 1), 2 ** 15 - 1, I32), 0)
    jlim = jnp.where(tau == jnp.int32(NEG_INF_KEY), jnp.int32(-1), jlim)
    tau_b = jnp.broadcast_to(tau, (rows, LANES))
    tau_m1_b = tau_b - 1

    q = q_ref[...] * (HEAD_DIM ** -0.5)
    qh = [q[:, h * HEAD_DIM:(h + 1) * HEAD_DIM].astype(BF16) for h in range(B_HEADS)]

    def init_state():
        return (jnp.full((rows, 1), NEG_BIG, F32), jnp.zeros((rows, 1), F32), jnp.zeros((rows, HEAD_DIM), F32))

    def attend(a0, width, states, delta):
        kk = key_scr[:, pl.ds(a0, width)]
        aidx = a0 + lax.broadcasted_iota(I32, (1, width), 1)
        thr = jnp.where(aidx <= jlim, jnp.concatenate([tau_m1_b] * (width // LANES), axis=1),
                        jnp.concatenate([tau_b] * (width // LANES), axis=1))
        madd = jnp.where(kk > thr, 0.0, NEG_BIG)
        kts = [kp_ref[pl.ds(a0, width), g * HEAD_DIM:(g + 1) * HEAD_DIM] for g in range(B_KV_HEADS)]
        vts = [vp_ref[pl.ds(a0, width), g * HEAD_DIM:(g + 1) * HEAD_DIM] for g in range(B_KV_HEADS)]
        lgs = [lax.dot_general(qh[h], kts[h // B_REP], (((1,), (1,)), ((), ())), preferred_element_type=F32)
               for h in range(B_HEADS)]
        mid = []
        for h in range(B_HEADS):
            m, l, _ = states[h]
            lg = lgs[h] + madd
            if delta is not None:
                lg = lg + delta[h]
            m_new = jnp.maximum(m, jnp.max(lg, axis=1, keepdims=True))
            alpha = jnp.exp(m - m_new)
            p = jnp.exp(lg - m_new)
            mid.append((m_new, alpha, alpha * l + jnp.sum(p, axis=1, keepdims=True), p.astype(BF16)))
        out = []
        for h in range(B_HEADS):
            m_new, alpha, l, pb = mid[h]
            acc = alpha * states[h][2] + jnp.dot(pb, vts[h // B_REP], preferred_element_type=F32)
            out.append((m_new, l, acc))
        return tuple(out)

    def far_chunk(c, states):
        return attend(pl.multiple_of(c * KEY_CHUNK, KEY_CHUNK), KEY_CHUNK, states, None)

    n_far = j // 4
    states = lax.fori_loop(0, n_far, far_chunk, tuple(init_state() for _ in range(B_HEADS)))
    d0 = pl.multiple_of((3 - j % 4) * Q_TILE, Q_TILE)
    delta = [dl_ref[h, :, pl.ds(d0, TAIL_WIDTH)] for h in range(B_HEADS)]
    states = attend(pl.multiple_of(n_far * KEY_CHUNK, KEY_CHUNK), TAIL_WIDTH, states, delta)

    o_ref[...] = jnp.concatenate([acc / l for _, l, acc in states], axis=1)


def _dsa_prompt(q, qi, kw, kp, vp, kip, delta, n_qblk, topk):
    nkp = kp.shape[0]
    full = lambda a: pl.BlockSpec(a.shape, lambda j: (0,) * a.ndim)
    return pl.pallas_call(
        functools.partial(_dsa_prompt_kernel, topk=topk),
        grid=(n_qblk,),
        in_specs=[pl.BlockSpec((Q_TILE, B_WIDTH), lambda j: (j, 0)),
                  pl.BlockSpec((Q_TILE, IDX_HEADS * IDX_DIM), lambda j: (j, 0)),
                  pl.BlockSpec((Q_TILE, LANES), lambda j: (j, 0)),
                  full(kp), full(vp), full(kip), full(delta)],
        out_specs=pl.BlockSpec((Q_TILE, B_WIDTH), lambda j: (j, 0)),
        out_shape=jax.ShapeDtypeStruct((n_qblk * Q_TILE, B_WIDTH), F32),
        scratch_shapes=[pltpu.VMEM((Q_TILE, nkp), I32), pltpu.VMEM((Q_TILE, nkp), I16)],
        compiler_params=_cparams(("arbitrary",)),
        name="dsa_prompt",
    )(q, qi, kw, kp, vp, kip, delta)


SROWS = 8


def _dsa_sample_score_kernel(pt_ref, qi_ref, kw_ref, *refs, n_steps, pages, dec_seq):
    cki_refs, kin_ref, key_ref = refs[:pages], refs[pages], refs[pages + 1]
    p = pl.program_id(1)
    qi_b = qi_ref[0].astype(BF16)
    wi = kw_ref[0][:, IDX_DIM:IDX_DIM + IDX_HEADS]
    t = lax.broadcasted_iota(I32, (SROWS, 1), 0)
    c = lax.broadcasted_iota(I32, (1, PAGE_SIZE), 1)

    @pl.when(p < n_steps)
    def _():
        for i in range(pages):
            s = _index_scores(qi_b, wi, cki_refs[i][0].astype(BF16))
            key_ref[0, :, i * PAGE_SIZE:(i + 1) * PAGE_SIZE] = _sortable_key(jnp.where(t < dec_seq, s, -jnp.inf))

    @pl.when(p == n_steps)
    def _():
        s = _index_scores(qi_b, wi, kin_ref[0].astype(BF16))
        key_ref[0, :, 0:PAGE_SIZE] = _sortable_key(jnp.where((t < dec_seq) & (c <= t), s, -jnp.inf))
        if pages > 1:
            key_ref[0, :, PAGE_SIZE:] = jnp.full((SROWS, (pages - 1) * PAGE_SIZE), NEG_INF_KEY, I32)


def _dsa_sample_attend_kernel(pt_ref, q_ref, key_ref, *refs, n_steps, pages, topk):
    ck_refs, cv_refs = refs[:pages], refs[pages:2 * pages]
    kn_ref, vn_ref, dl_ref, o_ref, tau_scr, jl_scr, m_scr, l_scr, acc_scr = refs[2 * pages:]
    p = pl.program_id(1)
    width = pages * PAGE_SIZE

    @pl.when(p == 0)
    def _():
        keys = key_ref[0]
        aidx = lax.broadcasted_iota(I32, (1, keys.shape[1]), 1)

        def count_fn(cand, strict):
            hit = (keys > cand) if strict else (keys >= cand)
            return jnp.sum(jnp.where(hit, 1.0, 0.0), axis=1, keepdims=True)

        def count_tie_fn(tau, jl):
            return jnp.sum(jnp.where((keys == tau) & (aidx < jl), 1.0, 0.0), axis=1, keepdims=True)

        tau, jlim = _topk_threshold(count_fn, count_tie_fn, SROWS, topk)
        tau_scr[...] = tau
        jl_scr[...] = jlim
        m_scr[...] = jnp.full(m_scr.shape, NEG_BIG, F32)
        l_scr[...] = jnp.zeros(l_scr.shape, F32)
        acc_scr[...] = jnp.zeros(acc_scr.shape, F32)

    a0 = pl.multiple_of(p * width, width)
    kk = key_ref[0, :, pl.ds(a0, width)]
    aidx = a0 + lax.broadcasted_iota(I32, (1, width), 1)
    sel = _select(kk, aidx, tau_scr[...], jl_scr[...])
    sel3 = jnp.concatenate([sel] * B_REP, axis=0)
    q = q_ref[0] * (HEAD_DIM ** -0.5)

    def run(k_refs, v_refs, delta):
        n = len(k_refs)
        sel_n = sel3[:, :n * PAGE_SIZE]
        for g in range(B_KV_HEADS):
            hs = slice(g * HEAD_DIM, (g + 1) * HEAD_DIM)
            qg = jnp.concatenate([q[:, (g * B_REP + r) * HEAD_DIM:(g * B_REP + r + 1) * HEAD_DIM]
                                  for r in range(B_REP)], axis=0).astype(BF16)
            parts = [lax.dot_general(qg, kr[0][:, hs].astype(BF16), (((1,), (1,)), ((), ())),
                                     preferred_element_type=F32) for kr in k_refs]
            if delta is not None:
                parts[-1] = parts[-1] + delta[g]
            lg = jnp.where(sel_n, jnp.concatenate(parts, axis=1), NEG_BIG)
            m = m_scr[g]
            m_new = jnp.maximum(m, jnp.max(lg, axis=1, keepdims=True))
            alpha = jnp.exp(m - m_new)
            pr = jnp.where(sel_n, jnp.exp(lg - m_new), 0.0).astype(BF16)
            acc = alpha * acc_scr[g]
            for i, vr in enumerate(v_refs):
                acc = acc + jnp.dot(pr[:, i * PAGE_SIZE:(i + 1) * PAGE_SIZE], vr[0][:, hs].astype(BF16),
                                    preferred_element_type=F32)
            m_scr[g] = m_new
            l_scr[g] = alpha * l_scr[g] + jnp.sum(pr.astype(F32), axis=1, keepdims=True)
            acc_scr[g] = acc

    @pl.when(p < n_steps - 1)
    def _():
        run(ck_refs, cv_refs, None)

    @pl.when(p == n_steps - 1)
    def _():
        run(ck_refs, cv_refs, (dl_ref[0, 0], dl_ref[0, 1]))

    @pl.when(p == n_steps)
    def _():
        run([kn_ref], [vn_ref], (dl_ref[1, 0], dl_ref[1, 1]))
        pieces = [None] * B_HEADS
        for g in range(B_KV_HEADS):
            og = acc_scr[g] / l_scr[g]
            for r in range(B_REP):
                pieces[g * B_REP + r] = og[r * SROWS:(r + 1) * SROWS, :]
        o_ref[0] = jnp.concatenate(pieces, axis=1)


def _dsa_sample(page_table, q_s, qi_s, kw_s, cki, ck, cv, ki_new, k_new, v_new, delta_s, dec_seq, topk):
    nb = q_s.shape[0]
    n_pages = page_table.shape[1]
    pages = math.gcd(n_pages, SAMPLE_PAGES)
    n_steps = n_pages // pages
    width = pages * PAGE_SIZE
    n_keys = (n_steps + 1) * width

    def page(i):
        return lambda b, p, pt: (pt[b, jnp.minimum(p, n_steps - 1) * pages + i], 0, 0)

    per_b = lambda b, p, pt: (b, 0, 0)
    keys = pl.pallas_call(
        functools.partial(_dsa_sample_score_kernel, n_steps=n_steps, pages=pages, dec_seq=dec_seq),
        grid_spec=pltpu.PrefetchScalarGridSpec(
            num_scalar_prefetch=1, grid=(nb, n_steps + 1),
            in_specs=[pl.BlockSpec((1, SROWS, IDX_HEADS * IDX_DIM), per_b),
                      pl.BlockSpec((1, SROWS, LANES), per_b)]
            + [pl.BlockSpec((1, PAGE_SIZE, IDX_DIM), page(i)) for i in range(pages)]
            + [pl.BlockSpec((1, PAGE_SIZE, IDX_DIM), per_b)],
            out_specs=pl.BlockSpec((1, SROWS, width), lambda b, p, pt: (b, 0, p))),
        out_shape=jax.ShapeDtypeStruct((nb, SROWS, n_keys), I32),
        compiler_params=_cparams(("arbitrary", "arbitrary")),
        name="dsa_sample_score",
    )(page_table, qi_s, kw_s, *([cki] * pages), ki_new)
    grows = B_REP * SROWS
    return pl.pallas_call(
        functools.partial(_dsa_sample_attend_kernel, n_steps=n_steps, pages=pages, topk=topk),
        grid_spec=pltpu.PrefetchScalarGridSpec(
            num_scalar_prefetch=1, grid=(nb, n_steps + 1),
            in_specs=[pl.BlockSpec((1, SROWS, B_WIDTH), per_b),
                      pl.BlockSpec((1, SROWS, n_keys), per_b)]
            + [pl.BlockSpec((1, PAGE_SIZE, KV_WIDTH), page(i)) for i in range(pages)] * 2
            + [pl.BlockSpec((1, PAGE_SIZE, KV_WIDTH), per_b),
               pl.BlockSpec((1, PAGE_SIZE, KV_WIDTH), per_b),
               pl.BlockSpec(delta_s.shape, lambda b, p, pt: (0, 0, 0, 0))],
            out_specs=pl.BlockSpec((1, SROWS, B_WIDTH), per_b),
            scratch_shapes=[pltpu.VMEM((SROWS, 1), I32), pltpu.VMEM((SROWS, 1), I32),
                            pltpu.VMEM((B_KV_HEADS, grows, 1), F32), pltpu.VMEM((B_KV_HEADS, grows, 1), F32),
                            pltpu.VMEM((B_KV_HEADS, grows, HEAD_DIM), F32)]),
        out_shape=jax.ShapeDtypeStruct((nb, SROWS, B_WIDTH), F32),
        compiler_params=_cparams(("arbitrary", "arbitrary")),
        name="dsa_sample_attend",
    )(page_table, q_s, keys, *([ck] * pages), *([cv] * pages), k_new, v_new, delta_s)


def _pool_kernel(c_ref, halo_ref, w_ref, scale_ref, o_ref, *, block, offset):
    i = pl.program_id(1)
    c = c_ref[0]
    halo = jnp.where(i == 0, 0.0, halo_ref[0])
    full = jnp.concatenate([halo, c], axis=0)
    sums = [full]
    for sh in (1, 2, 4, 8):
        sums.append(sums[-1] + pltpu.roll(sums[-1], sh, axis=0))
    lane = lax.broadcasted_iota(I32, (block, C_WIDTH), 1)
    pos = i * block + lax.broadcasted_iota(I32, (block, C_WIDTH), 0) - offset
    win = jnp.zeros((block, C_WIDTH), F32)
    cnt = jnp.ones((block, C_WIDTH), F32)
    for gi, w in enumerate(POOL_WINDOWS):
        in_g = (lane >= gi * C_GROUP_DIM) & (lane < (gi + 1) * C_GROUP_DIM)
        win = jnp.where(in_g, sums[gi + 1][POOL_HALO:, :], win)
        cnt = jnp.where(in_g, jnp.clip(pos + 1, 1, w).astype(F32), cnt)
    d = win / cnt - c
    o_ref[0] = jnp.dot(d.astype(BF16), w_ref[...], preferred_element_type=F32) * scale_ref[...]


def _pool(c, w_bd, scale, block, offset):
    n_seq, length, _ = c.shape
    ratio = block // POOL_HALO
    return pl.pallas_call(
        functools.partial(_pool_kernel, block=block, offset=offset),
        grid=(n_seq, length // block),
        in_specs=[pl.BlockSpec((1, block, C_WIDTH), lambda s, i: (s, i, 0)),
                  pl.BlockSpec((1, POOL_HALO, C_WIDTH), lambda s, i: (s, jnp.maximum(i * ratio - 1, 0), 0)),
                  pl.BlockSpec((C_WIDTH, C_WIDTH), lambda s, i: (0, 0)),
                  pl.BlockSpec((1, C_WIDTH), lambda s, i: (0, 0))],
        out_specs=pl.BlockSpec((1, block, C_WIDTH), lambda s, i: (s, i, 0)),
        out_shape=jax.ShapeDtypeStruct(c.shape, F32),
        compiler_params=_cparams(("parallel", "parallel")),
        name="pool",
    )(c, c, w_bd, scale)


def _outproj_kernel(x_ref, oa_ref, ob_ref, oc_ref, wo_ref, g_ref, wr_ref, br_ref, xo_ref, h_ref, gm_ref):
    mix = jnp.dot(oa_ref[...].astype(BF16), wo_ref[0:A_WIDTH, :], preferred_element_type=F32)
    mix += jnp.dot(ob_ref[...].astype(BF16), wo_ref[A_WIDTH:A_WIDTH + B_WIDTH, :], preferred_element_type=F32)
    mix += jnp.dot(oc_ref[...].astype(BF16), wo_ref[A_WIDTH + B_WIDTH:, :], preferred_element_type=F32)
    x = x_ref[...] + mix
    xo_ref[...] = x
    h = _rms(x, g_ref[...])
    h_ref[...] = h.astype(BF16)
    logits = jnp.dot(h, wr_ref[...], precision=HIGHEST, preferred_element_type=F32) + br_ref[...]
    lane = lax.broadcasted_iota(I32, logits.shape, 1)
    work = logits
    vals, hots = [], []
    for _ in range(TOP_K):
        m = jnp.max(work, axis=1, keepdims=True)
        idx = jnp.min(jnp.where(work == m, lane, N_EXPERTS), axis=1, keepdims=True)
        hot = lane == idx
        vals.append(m)
        hots.append(hot)
        work = jnp.where(hot, -jnp.inf, work)
    es = [jnp.exp(v - vals[0]) for v in vals]
    den = es[0] + es[1] + es[2] + es[3]
    gm = jnp.zeros(logits.shape, F32)
    for e, hot in zip(es, hots):
        gm = jnp.where(hot, e / den, gm)
    gm_ref[...] = gm


def _outproj(x, oa, ob, oc, wo, g, wr, br):
    nt = x.shape[0]
    tile = lambda n: pl.BlockSpec((TOK_TILE, n), lambda i: (i, 0))
    full = lambda a: pl.BlockSpec(a.shape, lambda i: (0, 0))
    return pl.pallas_call(
        _outproj_kernel,
        grid=(nt // TOK_TILE,),
        in_specs=[tile(D_MODEL), tile(A_WIDTH), tile(B_WIDTH), tile(C_WIDTH), full(wo), full(g), full(wr), full(br)],
        out_specs=[tile(D_MODEL), tile(D_MODEL), tile(N_EXPERTS)],
        out_shape=[jax.ShapeDtypeStruct((nt, D_MODEL), F32), jax.ShapeDtypeStruct((nt, D_MODEL), BF16),
                   jax.ShapeDtypeStruct((nt, N_EXPERTS), F32)],
        compiler_params=_cparams(("parallel",)),
        name="outproj",
    )(x, oa, ob, oc, wo, g, wr, br)


def _moe_kernel(h_ref, gm_ref, x_ref, w1_ref, b1_ref, w2_ref, b2_ref, gf_ref, o_ref, acc_scr, *, final):
    e = pl.program_id(1)

    @pl.when(e == 0)
    def _():
        acc_scr[...] = jnp.zeros_like(acc_scr)

    u = jnp.dot(h_ref[...], w1_ref[0].astype(BF16), preferred_element_type=F32) + b1_ref[0]
    glu = jnp.minimum(u[:, :D_FF], SWIGLU_LIMIT)
    lin = jnp.clip(u[:, D_FF:], -SWIGLU_LIMIT, SWIGLU_LIMIT)
    act = (glu * _sigmoid(SWIGLU_ALPHA * glu) * (lin + 1.0)).astype(BF16)
    out = jnp.dot(act, w2_ref[0].astype(BF16), preferred_element_type=F32) + b2_ref[0]
    gm = gm_ref[...]
    lane = lax.broadcasted_iota(I32, gm.shape, 1)
    gate = jnp.sum(jnp.where(lane == e, gm, 0.0), axis=1, keepdims=True)
    acc_scr[...] += gate * out

    @pl.when(e == N_EXPERTS - 1)
    def _():
        x = x_ref[...] + acc_scr[...]
        if final:
            x = _rms(x, gf_ref[...])
        o_ref[...] = x


def _moe(h, gm, x, w1, b1, w2, b2, gf, final):
    nt = x.shape[0]
    tile = lambda n: pl.BlockSpec((TOK_TILE, n), lambda i, e: (i, 0))
    return pl.pallas_call(
        functools.partial(_moe_kernel, final=final),
        grid=(nt // TOK_TILE, N_EXPERTS),
        in_specs=[tile(D_MODEL), tile(N_EXPERTS), tile(D_MODEL),
                  pl.BlockSpec((1, D_MODEL, 2 * D_FF), lambda i, e: (e, 0, 0)),
                  pl.BlockSpec((1, 1, 2 * D_FF), lambda i, e: (e, 0, 0)),
                  pl.BlockSpec((1, D_FF, D_MODEL), lambda i, e: (e, 0, 0)),
                  pl.BlockSpec((1, 1, D_MODEL), lambda i, e: (e, 0, 0)),
                  pl.BlockSpec((1, D_MODEL), lambda i, e: (0, 0))],
        out_specs=tile(D_MODEL),
        out_shape=jax.ShapeDtypeStruct((nt, D_MODEL), F32),
        scratch_shapes=[pltpu.VMEM((TOK_TILE, D_MODEL), F32)],
        compiler_params=_cparams(("parallel", "arbitrary")),
        name="moe",
    )(h, gm, x, w1, b1[:, None, :], w2, b2[:, None, :], gf)


def _rel_bucket(dist):
    n = jnp.maximum(dist, 0)
    max_exact = N_BUCKETS // 2
    log_ratio = jnp.log(jnp.maximum(n, 1).astype(jnp.float32) / max_exact) / math.log(MAX_DISTANCE / max_exact)
    large = jnp.minimum(max_exact + (log_ratio * (N_BUCKETS - max_exact)).astype(jnp.int32), N_BUCKETS - 1)
    return jnp.where(n < max_exact, n, large)


def _bias_delta(rel_bias, dist):
    rb = rel_bias.astype(F32)
    d = rb[_rel_bucket(dist)] - rb[N_BUCKETS - 1]
    d = jnp.where(((dist >= 0) & (dist < MAX_DISTANCE))[..., None], d, 0.0)
    return jnp.moveaxis(d, -1, 0)


def _group_rows(delta, rows):
    return delta.reshape(B_KV_HEADS, B_REP * rows, delta.shape[-1])


def kernel(x_prompt, x_sample, cache_k, cache_v, cache_kidx, state_wkv, state_shift, state_pool, page_table,
           meta_tokens, rel_bias, norm_mix, norm_ffn, norm_final, w_in, w_out,
           a_mu, a_w0, a_w2, a_a0, a_a2, a_g2, a_kk, a_ka, a_rk, a_ln_w, a_ln_b,
           c_w, c_scale, moe_w_router, moe_b_router, moe_w1, moe_b1, moe_w2, moe_b2):
    assert x_prompt.shape[0] == 1
    depth = w_in.shape[0]
    seq = x_prompt.shape[1]
    nb, dec_seq = x_sample.shape[:2]
    assert dec_seq <= SROWS
    t_p = seq + N_META
    n_s = nb * dec_seq
    n_qblk = -(-t_p // Q_TILE)
    nkp = KEY_CHUNK * ((n_qblk + 4) // 4 + 1)
    assert nkp < 2 ** 15
    nt = _round_up(max(t_p + n_s, Q_TILE * n_qblk), TOK_TILE)
    n_pages = page_table.shape[1]
    past = n_pages * PAGE_SIZE
    topk_p = min(TOPK_MAX, seq // 4)
    topk_s = min(TOPK_MAX, (past + dec_seq) // 4)
    n_phys = cache_k.shape[1]

    x = jnp.concatenate([meta_tokens.astype(F32), x_prompt[0], x_sample.reshape(n_s, D_MODEL),
                         jnp.zeros((nt - t_p - n_s, D_MODEL), F32)], axis=0)

    ones_blk = jnp.kron(jnp.eye(A_HEADS, dtype=F32), jnp.ones((HEAD_DIM, HEAD_DIM), F32))
    qi_ = jnp.arange(Q_TILE)[:, None]
    cj_ = jnp.arange(2 * Q_TILE)[None, :]
    side = TAIL_WIDTH - 2 * Q_TILE
    delta_p = jnp.pad(_bias_delta(rel_bias, Q_TILE + qi_ - cj_), ((0, 0), (0, 0), (side, side)))
    ts_ = jnp.arange(SROWS)[:, None]
    cs_ = jnp.arange(PAGE_SIZE)[None, :]
    delta_s = jnp.stack([_group_rows(_bias_delta(rel_bias, ts_ + PAGE_SIZE - cs_), SROWS),
                         _group_rows(_bias_delta(rel_bias, ts_ - cs_), SROWS)])

    outs = {n: [] for n in ("kp", "vp", "kip", "wkvp", "shp", "plp", "ks", "vs", "kis", "wkvs", "shs", "pls")}
    for l in range(depth):
        wl = w_in[l]
        b0 = A_PROJ
        w_cat = jnp.concatenate([
            wl[:, :A_PROJ], wl[:, b0:b0 + B_WIDTH + 2 * KV_WIDTH + IDX_HEADS * IDX_DIM],
            wl[:, b0 + B_PROJ - IDX_DIM - IDX_HEADS:b0 + B_PROJ],
            jnp.zeros((D_MODEL, LANES - IDX_DIM - IDX_HEADS), F32), wl[:, b0 + B_PROJ:]], axis=1).astype(BF16)
        pa, q, k, v, qi, kw, pc = _inproj(x, norm_mix[l][None], w_cat)

        sl_p = slice(0, t_p)
        sl_s = slice(t_p, t_p + n_s)
        outs["kp"].append(k[sl_p].reshape(1, t_p, B_KV_HEADS, HEAD_DIM))
        outs["vp"].append(v[sl_p].reshape(1, t_p, B_KV_HEADS, HEAD_DIM))
        outs["kip"].append(kw[sl_p, :IDX_DIM][None])
        outs["ks"].append(k[sl_s].reshape(nb, dec_seq, B_KV_HEADS, HEAD_DIM))
        outs["vs"].append(v[sl_s].reshape(nb, dec_seq, B_KV_HEADS, HEAD_DIM))
        outs["kis"].append(kw[sl_s, :IDX_DIM].reshape(nb, dec_seq, IDX_DIM))
        outs["shp"].append(pa[t_p - 1:t_p])
        pa_s = pa[sl_s].reshape(nb, dec_seq, A_PROJ)
        outs["shs"].append(pa_s[:, -1])
        pc_s = pc[sl_s].reshape(nb, dec_seq, C_WIDTH)
        outs["plp"].append(pc[t_p - POOL_BUF:t_p][None])
        pool_full = jnp.concatenate([state_pool[l], pc_s], axis=1)
        outs["pls"].append(pool_full[:, -POOL_BUF:])

        rw = (a_mu[l], a_w0[l], a_w2[l], a_a0[l], a_a2[l], a_g2[l], a_kk[l], a_ka[l], a_rk[l], a_ln_w[l], a_ln_b[l])
        oa_p, st_p = _rwkv(pa[None], jnp.zeros((1, 1, A_PROJ), F32), jnp.zeros((1, HEAD_DIM, A_WIDTH), F32),
                           rw, ones_blk, t_p, RWKV_BLOCK)
        pa_s8 = jnp.pad(pa_s, ((0, 0), (0, SROWS - dec_seq), (0, 0)))
        s0_s = state_wkv[l].astype(F32).transpose(0, 3, 1, 2).reshape(nb, HEAD_DIM, A_WIDTH)
        oa_s, st_s = _rwkv(pa_s8, state_shift[l][:, None, :], s0_s, rw, ones_blk, dec_seq, SROWS)
        untr = lambda s: s.reshape(-1, HEAD_DIM, A_HEADS, HEAD_DIM).transpose(0, 2, 3, 1)
        outs["wkvp"].append(untr(st_p).astype(state_wkv.dtype))
        outs["wkvs"].append(untr(st_s).astype(state_wkv.dtype))

        n_real = min(nt, nkp - KEY_FRONT_PAD)
        front = lambda a: jnp.pad(a[:n_real].astype(BF16), ((KEY_FRONT_PAD, nkp - KEY_FRONT_PAD - n_real), (0, 0)))
        ob_p = _dsa_prompt(q, qi, kw, front(k), front(v), front(kw[:, :IDX_DIM]), delta_p, n_qblk, topk_p)
        rows8 = lambda a: jnp.pad(a[sl_s].reshape(nb, dec_seq, -1), ((0, 0), (0, SROWS - dec_seq), (0, 0)))
        page8 = lambda a: jnp.pad(a[sl_s].reshape(nb, dec_seq, -1), ((0, 0), (0, PAGE_SIZE - dec_seq), (0, 0)))
        ob_s = _dsa_sample(page_table, rows8(q), rows8(qi), rows8(kw),
                           cache_kidx[l], cache_k[l].reshape(n_phys, PAGE_SIZE, KV_WIDTH),
                           cache_v[l].reshape(n_phys, PAGE_SIZE, KV_WIDTH),
                           page8(kw[:, :IDX_DIM]), page8(k), page8(v), delta_s, dec_seq, topk_s)

        w_bd = jax.scipy.linalg.block_diag(*[c_w[l][gi] for gi in range(len(POOL_WINDOWS))]).astype(BF16)
        oc_p = _pool(pc[None], w_bd, c_scale[l][None], TOK_TILE, 0)
        pool_in = jnp.concatenate([jnp.zeros((nb, 1, C_WIDTH), F32), pool_full,
                                   jnp.zeros((nb, 2 * POOL_HALO - 1 - POOL_BUF - dec_seq, C_WIDTH), F32)], axis=1)
        oc_s = _pool(pool_in, w_bd, c_scale[l][None], POOL_HALO, 1)[:, POOL_HALO:POOL_HALO + dec_seq]

        def merge(p_rows, s_rows):
            w = p_rows.shape[-1]
            return jnp.concatenate([p_rows[:t_p], s_rows.reshape(n_s, w), jnp.zeros((nt - t_p - n_s, w), F32)], axis=0)

        oa = merge(oa_p[0], oa_s[:, :dec_seq])
        ob = merge(ob_p, ob_s[:, :dec_seq])
        oc = merge(oc_p[0], oc_s)
        x, h2, gm = _outproj(x, oa, ob, oc, w_out[l].astype(BF16), norm_ffn[l][None], moe_w_router[l],
                             moe_b_router[l][None])
        x = _moe(h2, gm, x, moe_w1[l], moe_b1[l], moe_w2[l], moe_b2[l], norm_final[None], l == depth - 1)

    y_prompt = x[N_META:t_p][None]
    y_sample = x[t_p:t_p + n_s].reshape(nb, dec_seq, D_MODEL)
    st = lambda n: jnp.stack(outs[n])
    return (y_prompt, y_sample, st("kp"), st("vp"), st("kip"), st("wkvp"), st("shp"), st("plp"),
            st("ks"), st("vs"), st("kis"), st("wkvs"), st("shs"), st("pls"))
```

```python
import functools
import math

import jax
import jax.numpy as jnp
import numpy as np
from jax import lax
from jax.experimental import pallas as pl
from jax.experimental.pallas import tpu as pltpu

F32 = jnp.float32
BF16 = jnp.bfloat16
I32 = jnp.int32
HIGHEST = lax.Precision.HIGHEST

D_MODEL = 1024
N_META = 16
HEAD_DIM = 64
A_WIDTH = 384
A_HEADS = 6
LORA_W, LORA_A, LORA_G = 64, 64, 128
A_PROJ = 3 * A_WIDTH + LORA_W + LORA_A + LORA_G
GN_EPS = 64e-5
B_WIDTH = 384
B_HEADS = 6
B_KV_HEADS = 2
B_REP = 3
KV_WIDTH = 128
IDX_HEADS = 4
IDX_DIM = 64
IDX_SCALE = (IDX_HEADS * IDX_DIM) ** -0.5
TOPK_MAX = 256
B_PROJ = B_WIDTH + 2 * KV_WIDTH + IDX_HEADS * IDX_DIM + IDX_DIM + IDX_HEADS
C_WIDTH = 256
C_GROUP_DIM = 64
POOL_WINDOWS = (2, 4, 8, 16)
POOL_BUF = 15
N_BUCKETS = 32
MAX_DISTANCE = 128
N_EXPERTS = 32
TOP_K = 4
D_FF = 1024
SWIGLU_LIMIT = 7.0
SWIGLU_ALPHA = 1.702
RMS_EPS = 1e-5
PAGE_SIZE = 128

LANES = 128
SUBLANES = 8
Q_TILE = 128
KEY_CHUNK = 512
KEY_FRONT_PAD = 128
TAIL_WIDTH = 640
TOP_M = 12
SAMPLE_PAGES = 16
TOK_TILE = 512
RWKV_BLOCK = 128
POOL_HALO = 16
VMEM_LIMIT = 56 * 1024 * 1024

INT_MIN = -(2 ** 31)
NEG_INF_KEY = np.int32(np.uint32(0xFF800000) ^ np.uint32(0x7FFFFFFF))
NEG_BIG = -1e30


def _round_up(x, m):
    return (x + m - 1) // m * m


def _cparams(sem):
    return pltpu.CompilerParams(dimension_semantics=sem, vmem_limit_bytes=VMEM_LIMIT)


def _rms(x, g):
    return x * lax.rsqrt(jnp.mean(x * x, axis=-1, keepdims=True) + RMS_EPS) * g


def _sigmoid(x):
    return 1.0 / (1.0 + jnp.exp(-x))


IN_COLS = (A_PROJ, B_WIDTH, KV_WIDTH, KV_WIDTH, IDX_HEADS * IDX_DIM, LANES, C_WIDTH)


def _inproj_kernel(x_ref, g_ref, w_ref, *out_refs):
    hb = _rms(x_ref[...], g_ref[...]).astype(BF16)
    off = 0
    for ref in out_refs:
        n = ref.shape[1]
        ref[...] = jnp.dot(hb, w_ref[:, off:off + n], preferred_element_type=F32)
        off += n


def _inproj(x, g, w):
    nt = x.shape[0]
    return pl.pallas_call(
        _inproj_kernel,
        grid=(nt // TOK_TILE,),
        in_specs=[pl.BlockSpec((TOK_TILE, D_MODEL), lambda i: (i, 0)),
                  pl.BlockSpec((1, D_MODEL), lambda i: (0, 0)),
                  pl.BlockSpec(w.shape, lambda i: (0, 0))],
        out_specs=[pl.BlockSpec((TOK_TILE, n), lambda i: (i, 0)) for n in IN_COLS],
        out_shape=[jax.ShapeDtypeStruct((nt, n), F32) for n in IN_COLS],
        compiler_params=_cparams(("parallel",)),
        name="inproj",
    )(x, g, w)


def _rwkv_kernel(pa_ref, sh_ref, s0_ref, mu_ref, w0_ref, w2_ref, a0_ref, a2_ref, g2_ref, kk_ref, ka_ref,
                 rk_ref, lnw_ref, lnb_ref, ones_ref, o_ref, sT_ref, carry_scr, s_scr, o_scr, *rows_scr,
                 seq_len, block):
    blk = pl.program_id(1)

    @pl.when(blk == 0)
    def _():
        carry_scr[...] = sh_ref[0]
        s_scr[...] = s0_ref[0]

    pa = pa_ref[0]
    row = lax.broadcasted_iota(I32, (block, 1), 0)
    prev = jnp.where(row == 0, carry_scr[...], pltpu.roll(pa, 1, axis=0))
    carry_scr[...] = pa[block - 1:block, :]
    xm = pa + (prev - pa) * mu_ref[...]
    r = xm[:, 0:A_WIDTH]
    k = xm[:, A_WIDTH:2 * A_WIDTH]
    v = xm[:, 2 * A_WIDTH:3 * A_WIDTH]
    lw = xm[:, 3 * A_WIDTH:3 * A_WIDTH + LORA_W]
    la = xm[:, 3 * A_WIDTH + LORA_W:3 * A_WIDTH + LORA_W + LORA_A]
    lg = xm[:, 3 * A_WIDTH + LORA_W + LORA_A:]

    def dot_hi(a, b):
        return jnp.dot(a, b, precision=HIGHEST, preferred_element_type=F32)

    z = -(w0_ref[...] + dot_hi(jnp.tanh(lw), w2_ref[...]))
    softplus = jnp.maximum(z, 0.0) + jnp.log(1.0 + jnp.exp(-jnp.abs(z)))
    w_log = -softplus - 0.5
    decay = jnp.exp(-jnp.exp(w_log))
    a = _sigmoid(a0_ref[...] + dot_hi(la, a2_ref[...]))
    g = dot_hi(_sigmoid(lg), g2_ref[...])
    ones_blk = ones_ref[...]
    kk = k * kk_ref[...]
    kk = kk / jnp.maximum(jnp.sqrt(dot_hi(kk * kk, ones_blk)), 1e-12)
    k2 = k * (1.0 + (a - 1.0) * ka_ref[...])

    for ref, val in zip(rows_scr, (decay, kk, kk * a, k2, r, v)):
        ref[...] = val
    o_scr[...] = jnp.zeros_like(o_scr)

    n_valid = jnp.minimum(block, seq_len - blk * block)
    group = SUBLANES if seq_len % SUBLANES == 0 else seq_len
    assert seq_len % group == 0 and group <= SUBLANES
    lane =lax.broadcasted_iota(I32, (HEAD_DIM, LANES), 1)
    n_pair = A_WIDTH // LANES

    def col_bcast(rowv):
        tr = jnp.broadcast_to(rowv, (LANES, LANES)).T
        return jnp.where(lane < HEAD_DIM, tr[0:HEAD_DIM], tr[HEAD_DIM:LANES])

    def token_group(gi, state):
        t0 = pl.multiple_of(gi * SUBLANES, SUBLANES)
        new_state = []
        for p in range(n_pair):
            sl = slice(p * LANES, (p + 1) * LANES)
            sp = state[p]
            tiles = [ref[pl.ds(t0, SUBLANES), sl] for ref in rows_scr]
            o_rows = []
            for i in range(group):
                w_b, kk_b, kka_b, k_b, r_b = (col_bcast(tiles[n][i:i + 1, :]) for n in range(5))
                v_row = tiles[5][i:i + 1, :]
                sa = jnp.sum(sp * kk_b, axis=0, keepdims=True)
                sp = sp * w_b - kka_b * sa + k_b * v_row
                o_rows.append(jnp.sum(sp * r_b, axis=0, keepdims=True))
            o_scr[pl.ds(t0, group), sl] = jnp.concatenate(o_rows, axis=0)
            new_state.append(sp)
        return tuple(new_state)

    state0 = tuple(s_scr[:, p * LANES:(p + 1) * LANES] for p in range(n_pair))
    state = lax.fori_loop(0, n_valid // group, token_group, state0)
    for p in range(n_pair):
        s_scr[:, p * LANES:(p + 1) * LANES] = state[p]
    sT_ref[0] = s_scr[...]

    o = o_scr[...]
    inv = 1.0 / HEAD_DIM
    mean = dot_hi(o, ones_blk) * inv
    cen = o - mean
    var = dot_hi(cen * cen, ones_blk) * inv
    o = cen * lax.rsqrt(var + GN_EPS) * lnw_ref[...] + lnb_ref[...]
    bonus = dot_hi(r * k2 * rk_ref[...], ones_blk) * v
    o_ref[0] = (o + bonus) * g


def _rwkv(pa, shift_prev, s0_t, params, ones_blk, seq_len, block):
    n_seq, length, _ = pa.shape
    n_blk = -(-seq_len // block)
    row_spec = lambda n: pl.BlockSpec((1, n), lambda s, i: (0, 0))
    mat_spec = lambda a: pl.BlockSpec(a.shape, lambda s, i: (0, 0))
    mu, w0, w2, a0, a2, g2, k_k, k_a, r_k, ln_w, ln_b = params
    kern = functools.partial(_rwkv_kernel, seq_len=seq_len, block=block)
    return pl.pallas_call(
        kern,
        grid=(n_seq, n_blk),
        in_specs=[pl.BlockSpec((1, block, A_PROJ), lambda s, i: (s, i, 0)),
                  pl.BlockSpec((1, 1, A_PROJ), lambda s, i: (s, 0, 0)),
                  pl.BlockSpec((1, HEAD_DIM, A_WIDTH), lambda s, i: (s, 0, 0)),
                  row_spec(A_PROJ), row_spec(A_WIDTH), mat_spec(w2), row_spec(A_WIDTH), mat_spec(a2),
                  mat_spec(g2), row_spec(A_WIDTH), row_spec(A_WIDTH), row_spec(A_WIDTH), row_spec(A_WIDTH),
                  row_spec(A_WIDTH), mat_spec(ones_blk)],
        out_specs=[pl.BlockSpec((1, block, A_WIDTH), lambda s, i: (s, i, 0)),
                   pl.BlockSpec((1, HEAD_DIM, A_WIDTH), lambda s, i: (s, 0, 0))],
        out_shape=[jax.ShapeDtypeStruct((n_seq, length, A_WIDTH), F32),
                   jax.ShapeDtypeStruct((n_seq, HEAD_DIM, A_WIDTH), F32)],
        scratch_shapes=[pltpu.VMEM((1, A_PROJ), F32), pltpu.VMEM((HEAD_DIM, A_WIDTH), F32)]
        + [pltpu.VMEM((block, A_WIDTH), F32)] * 7,
        compiler_params=_cparams(("arbitrary", "arbitrary")),
        name="rwkv",
    )(pa, shift_prev, s0_t, mu[None], w0[None], w2, a0[None], a2, g2, k_k[None], k_a[None], r_k[None],
      ln_w[None], ln_b[None], ones_blk)


def _sortable_key(s):
    bits = pltpu.bitcast(s + 0.0, I32)
    return bits ^ ((bits >> 31) & jnp.int32(0x7FFFFFFF))


def _index_scores(qi_b, wi, ki_b, transposed=False):
    s = None
    for h in range(IDX_HEADS):
        d = lax.dot_general(qi_b[:, h * IDX_DIM:(h + 1) * IDX_DIM], ki_b,
                            (((1,), (0 if transposed else 1,)), ((), ())), preferred_element_type=F32)
        t = jnp.maximum(d, 0.0) * wi[:, h:h + 1]
        s = t if s is None else s + t
    return s * IDX_SCALE


def _topk_threshold(count_fn, count_tie_fn, rows, topk):
    kf = jnp.float32(topk)
    c0 = count_fn(jnp.zeros((rows, 1), I32), False)
    prefix0 = jnp.where(c0 >= kf, jnp.int32(0), jnp.int32(INT_MIN))

    def bit_step(i, prefix):
        cand = prefix | lax.shift_left(jnp.int32(1), jnp.int32(30) - i)
        return jnp.where(count_fn(cand, False) >= kf, cand, prefix)

    tau = lax.fori_loop(0, 31, bit_step, prefix0)
    c_gt = count_fn(tau, True)
    c_ge = count_fn(tau, False)
    need = kf - c_gt
    excess = jnp.max(c_ge - c_gt - need)

    def tie_search(_):
        def jbit(i, jl):
            cand = jl | lax.shift_left(jnp.int32(1), jnp.int32(14) - i)
            return jnp.where(count_tie_fn(tau, cand) < need, cand, jl)
        return lax.fori_loop(0, 15, jbit, jnp.zeros((rows, 1), I32))

    jlim = lax.cond(excess > 0.0, tie_search, lambda _: jnp.full((rows, 1), 2 ** 30, I32), 0)
    return tau, jlim


def _select(keys, aidx, tau, jlim):
    return ((keys > tau) | ((keys == tau) & (aidx <= jlim))) & (keys > jnp.int32(NEG_INF_KEY))


def _softmax_step(state, qg, kt, vt, sel_rows, delta):
    m, l, acc = state
    lg = lax.dot_general(qg, kt, (((1,), (1,)), ((), ())), preferred_element_type=F32)
    if delta is not None:
        lg = lg + delta
    lg = jnp.where(sel_rows, lg, NEG_BIG)
    m_new = jnp.maximum(m, jnp.max(lg, axis=1, keepdims=True))
    alpha = jnp.exp(m - m_new)
    p = jnp.where(sel_rows, jnp.exp(lg - m_new), 0.0)
    l = alpha * l + jnp.sum(p, axis=1, keepdims=True)
    acc = alpha * acc + jnp.dot(p.astype(BF16), vt, preferred_element_type=F32)
    return m_new, l, acc


def _dsa_prompt_kernel(q_ref, qi_ref, kw_ref, kp_ref, vp_ref, kip_ref, dl_ref, o_ref, key_scr, cand_scr, *, topk):
    j = pl.program_id(0)
    rows = Q_TILE
    n_chunks = (j + 5) // 4
    qi_b = qi_ref[...].astype(BF16)
    wi = kw_ref[:, IDX_DIM:IDX_DIM + IDX_HEADS]
    qpos = j * Q_TILE + lax.broadcasted_iota(I32, (rows, 1), 0)
    lane_c = lax.broadcasted_iota(I32, (1, KEY_CHUNK), 1)

    def score_chunk(c, carry):
        a0 = pl.multiple_of(c * KEY_CHUNK, KEY_CHUNK)
        s = _index_scores(qi_b, wi, kip_ref[pl.ds(a0, KEY_CHUNK), :])
        kpos = a0 - KEY_FRONT_PAD + lane_c
        s = jnp.where((kpos <= qpos) & (kpos >= 0), s, -jnp.inf)
        key = _sortable_key(s)
        key_scr[:, pl.ds(a0, KEY_CHUNK)] = key
        return carry

    lax.fori_loop(0, n_chunks, score_chunk, 0)

    key_scr[:, pl.ds(pl.multiple_of(n_chunks * KEY_CHUNK, KEY_CHUNK), KEY_CHUNK)] = jnp.full(
        (rows, KEY_CHUNK), NEG_INF_KEY, I32)

    n_fold = KEY_CHUNK // LANES

    def chunk_at(ref, c):
        return ref[:, pl.ds(pl.multiple_of(c * KEY_CHUNK, KEY_CHUNK), KEY_CHUNK)]

    def count32(cand, strict):
        def body(c, acc):
            kk = chunk_at(key_scr, c)
            hit = (kk > cand) if strict else (kk >= cand)
            m = jnp.where(hit, 1.0, 0.0)
            for i in range(n_fold):
                acc = acc + m[:, i * LANES:(i + 1) * LANES]
            return acc
        acc = lax.fori_loop(0, n_chunks, body, jnp.zeros((rows, LANES), F32))
        return jnp.sum(acc, axis=1, keepdims=True)

    kf = jnp.float32(topk)

    def kth_largest(count_ge):
        def bit_step(i, prefix):
            cand = prefix | lax.shift_left(jnp.int32(1), jnp.int32(30) - i)
            return jnp.where(count_ge(cand) >= kf, cand, prefix)
        nonneg = count_ge(jnp.zeros((rows, 1), I32)) >= kf
        return lax.fori_loop(0, 31, bit_step, jnp.where(nonneg, jnp.int32(0), jnp.int32(INT_MIN)))

    pair = 2 * SUBLANES
    for rp in range(rows // pair):
        r0 = rp * pair

        def insert_chunk(c, lists, r0=r0):
            a0 = pl.multiple_of(c * KEY_CHUNK, KEY_CHUNK)
            blk = key_scr[r0:r0 + pair, pl.ds(a0, KEY_CHUNK)]
            lists = list(lists)
            for half in range(2):
                for t in range(n_fold):
                    x = blk[half * SUBLANES:(half + 1) * SUBLANES, t * LANES:(t + 1) * LANES]
                    for i in range(TOP_M):
                        a = lists[half * TOP_M + i]
                        lists[half * TOP_M + i] = jnp.maximum(a, x)
                        x = jnp.minimum(a, x)
            return tuple(lists)

        lists = lax.fori_loop(0, n_chunks, insert_chunk,
                              tuple(jnp.full((SUBLANES, LANES), INT_MIN, I32) for _ in range(2 * TOP_M)))
        for half in range(2):
            for i in range(TOP_M):
                cand_scr[r0 + half * SUBLANES:r0 + (half + 1) * SUBLANES, i * LANES:(i + 1) * LANES] = lists[half * TOP_M + i]

    def count_cand(cand):
        acc = jnp.zeros((rows, LANES), F32)
        for i in range(TOP_M):
            acc = acc + jnp.where(cand_scr[:, i * LANES:(i + 1) * LANES] >= cand, 1.0, 0.0)
        return jnp.sum(acc, axis=1, keepdims=True)

    tau_c = kth_largest(count_cand)
    gt_c = count32(tau_c, True)
    ge_c = count32(tau_c, False)
    wrong = jnp.max(jnp.where((gt_c < kf) & (ge_c >= kf), 0.0, 1.0))

    def full_search(_):
        t = kth_largest(lambda cand: count32(cand, False))
        return t, count32(t, True), count32(t, False)

    tau, c_gt, c_ge = lax.cond(wrong > 0.0, full_search, lambda _: (tau_c, gt_c, ge_c), 0)
    need = kf - c_gt
    excess = jnp.max(c_ge - c_gt - need)

    def tie_pass(_):
        ri = lax.broadcasted_iota(I32, (LANES, 2 * LANES), 0)
        ci = lax.broadcasted_iota(I32, (LANES, 2 * LANES), 1)
        tri = jnp.where((ri <= ci) | (ci >= LANES), 1.0, 0.0).astype(BF16)

        def body(c, carry):
            seen, jacc = carry
            a0 = pl.multiple_of(c * KEY_CHUNK, KEY_CHUNK)
            kk = key_scr[:, pl.ds(a0, KEY_CHUNK)]
            for i in range(n_fold):
                tie = kk[:, i * LANES:(i + 1) * LANES] == tau
                pre = jnp.dot(jnp.where(tie, 1.0, 0.0).astype(BF16), tri, preferred_element_type=F32)
                rank = jnp.where(tie, seen + pre[:, :LANES], 1e9)
                aidx = a0 + i * LANES + lax.broadcasted_iota(I32, (1, LANES), 1)
                jacc = jnp.maximum(jacc, jnp.where(rank <= need, aidx, -1))
                seen = seen + pre[:, LANES:]
            return seen, jacc
        _, jacc = lax.fori_loop(0, n_chunks, body, (jnp.zeros((rows, LANES), F32), jnp.full((rows, LANES), -1, I32)))
        return jnp.max(jacc, axis=1, keepdims=True)

    jlim = lax.cond(excess > 0.0, tie_pass, lambda _: jnp.full((rows, 1), 2 ** 30, I32), 0)
    jlim = jnp.where(tau == jnp.int32(NEG_INF_KEY), jnp.int32(-1), jlim)
    tau_b = jnp.broadcast_to(tau, (rows, LANES))
    tau_m1_b = tau_b - 1

    q = q_ref[...] * (HEAD_DIM ** -0.5)
    qh = [q[:, h * HEAD_DIM:(h + 1) * HEAD_DIM].astype(BF16) for h in range(B_HEADS)]

    def init_state():
        return (jnp.full((rows, 1), NEG_BIG, F32), jnp.zeros((rows, 1), F32), jnp.zeros((rows, HEAD_DIM), F32))

    def attend(a0, width, states, delta):
        kk = key_scr[:, pl.ds(a0, width)]
        aidx = a0 + lax.broadcasted_iota(I32, (1, width), 1)
        thr = jnp.where(aidx <= jlim, jnp.concatenate([tau_m1_b] * (width // LANES), axis=1),
                        jnp.concatenate([tau_b] * (width // LANES), axis=1))
        madd = jnp.where(kk > thr, 0.0, NEG_BIG)
        kts = [kp_ref[pl.ds(a0, width), g * HEAD_DIM:(g + 1) * HEAD_DIM] for g in range(B_KV_HEADS)]
        vts = [vp_ref[pl.ds(a0, width), g * HEAD_DIM:(g + 1) * HEAD_DIM] for g in range(B_KV_HEADS)]
        lgs = [lax.dot_general(qh[h], kts[h // B_REP], (((1,), (1,)), ((), ())), preferred_element_type=F32)
               for h in range(B_HEADS)]
        mid = []
        for h in range(B_HEADS):
            m, l, _ = states[h]
            lg = lgs[h] + madd
            if delta is not None:
                lg = lg + delta[h]
            m_new = jnp.maximum(m, jnp.max(lg, axis=1, keepdims=True))
            alpha = jnp.exp(m - m_new)
            p = jnp.exp(lg - m_new)
            mid.append((m_new, alpha, alpha * l + jnp.sum(p, axis=1, keepdims=True), p.astype(BF16)))
        out = []
        for h in range(B_HEADS):
            m_new, alpha, l, pb = mid[h]
            acc = alpha * states[h][2] + jnp.dot(pb, vts[h // B_REP], preferred_element_type=F32)
            out.append((m_new, l, acc))
        return tuple(out)

    def far_chunk(c, states):
        return attend(pl.multiple_of(c * KEY_CHUNK, KEY_CHUNK), KEY_CHUNK, states, None)

    n_far = j // 4
    states = lax.fori_loop(0, n_far, far_chunk, tuple(init_state() for _ in range(B_HEADS)))
    d0 = pl.multiple_of((3 - j % 4) * Q_TILE, Q_TILE)
    delta = [dl_ref[h, :, pl.ds(d0, TAIL_WIDTH)] for h in range(B_HEADS)]
    states = attend(pl.multiple_of(n_far * KEY_CHUNK, KEY_CHUNK), TAIL_WIDTH, states, delta)

    o_ref[...] = jnp.concatenate([acc / l for _, l, acc in states], axis=1)


def _dsa_prompt(q, qi, kw, kp, vp, kip, delta, n_qblk, topk):
    nkp = kp.shape[0]
    full = lambda a: pl.BlockSpec(a.shape, lambda j: (0,) * a.ndim)
    return pl.pallas_call(
        functools.partial(_dsa_prompt_kernel, topk=topk),
        grid=(n_qblk,),
        in_specs=[pl.BlockSpec((Q_TILE, B_WIDTH), lambda j: (j, 0)),
                  pl.BlockSpec((Q_TILE, IDX_HEADS * IDX_DIM), lambda j: (j, 0)),
                  pl.BlockSpec((Q_TILE, LANES), lambda j: (j, 0)),
                  full(kp), full(vp), full(kip), full(delta)],
        out_specs=pl.BlockSpec((Q_TILE, B_WIDTH), lambda j: (j, 0)),
        out_shape=jax.ShapeDtypeStruct((n_qblk * Q_TILE, B_WIDTH), F32),
        scratch_shapes=[pltpu.VMEM((Q_TILE, nkp), I32), pltpu.VMEM((Q_TILE, TOP_M * LANES), I32)],
        compiler_params=_cparams(("arbitrary",)),
        name="dsa_prompt",
    )(q, qi, kw, kp, vp, kip, delta)


SROWS = 8


def _dsa_sample_score_kernel(pt_ref, qi_ref, kw_ref, *refs, n_steps, pages, dec_seq):
    cki_refs, kin_ref, key_ref = refs[:pages], refs[pages], refs[pages + 1]
    p = pl.program_id(1)
    qi_b = qi_ref[0].astype(BF16)
    wi = kw_ref[0][:, IDX_DIM:IDX_DIM + IDX_HEADS]
    t = lax.broadcasted_iota(I32, (SROWS, 1), 0)
    c = lax.broadcasted_iota(I32, (1, PAGE_SIZE), 1)

    @pl.when(p < n_steps)
    def _():
        for i in range(pages):
            s = _index_scores(qi_b, wi, cki_refs[i][0, 0].astype(BF16), transposed=True)
            key_ref[0, :, i * PAGE_SIZE:(i + 1) * PAGE_SIZE] = _sortable_key(jnp.where(t < dec_seq, s, -jnp.inf))

    @pl.when(p == n_steps)
    def _():
        s = _index_scores(qi_b, wi, kin_ref[0].astype(BF16), transposed=True)
        key_ref[0, :, 0:PAGE_SIZE] = _sortable_key(jnp.where((t < dec_seq) & (c <= t), s, -jnp.inf))
        if pages > 1:
            key_ref[0, :, PAGE_SIZE:] = jnp.full((SROWS, (pages - 1) * PAGE_SIZE), NEG_INF_KEY, I32)


def _dsa_sample_attend_kernel(pt_ref, q_ref, key_ref, *refs, n_steps, pages, topk):
    ck_refs, cv_refs = refs[:pages], refs[pages:2 * pages]
    kn_ref, vn_ref, dl_ref, o_ref, tau_scr, jl_scr, m_scr, l_scr, acc_scr = refs[2 * pages:]
    p = pl.program_id(1)
    width = pages * PAGE_SIZE

    @pl.when(p == 0)
    def _():
        keys = key_ref[0]
        aidx = lax.broadcasted_iota(I32, (1, keys.shape[1]), 1)

        def count_fn(cand, strict):
            hit = (keys > cand) if strict else (keys >= cand)
            return jnp.sum(jnp.where(hit, 1.0, 0.0), axis=1, keepdims=True)

        def count_tie_fn(tau, jl):
            return jnp.sum(jnp.where((keys == tau) & (aidx < jl), 1.0, 0.0), axis=1, keepdims=True)

        tau, jlim = _topk_threshold(count_fn, count_tie_fn, SROWS, topk)
        tau_scr[...] = tau
        jl_scr[...] = jlim
        m_scr[...] = jnp.full(m_scr.shape, NEG_BIG, F32)
        l_scr[...] = jnp.zeros(l_scr.shape, F32)
        acc_scr[...] = jnp.zeros(acc_scr.shape, F32)

    a0 = pl.multiple_of(p * width, width)
    kk = key_ref[0, :, pl.ds(a0, width)]
    aidx = a0 + lax.broadcasted_iota(I32, (1, width), 1)
    sel = _select(kk, aidx, tau_scr[...], jl_scr[...])
    sel3 = jnp.concatenate([sel] * B_REP, axis=0)
    q = q_ref[0] * (HEAD_DIM ** -0.5)

    def cached(ref):
        return lambda g: ref[0, 0, g]

    def fresh(ref):
        return lambda g: ref[0, g]

    def run(k_tiles, v_tiles, delta):
        n = len(k_tiles)
        sel_n = sel3[:, :n * PAGE_SIZE]
        for g in range(B_KV_HEADS):
            qg = jnp.concatenate([q[:, (g * B_REP + r) * HEAD_DIM:(g * B_REP + r + 1) * HEAD_DIM]
                                  for r in range(B_REP)], axis=0).astype(BF16)
            parts = [jnp.dot(qg, kt(g).astype(BF16), preferred_element_type=F32) for kt in k_tiles]
            if delta is not None:
                parts[-1] = parts[-1] + delta[g]
            lg = jnp.where(sel_n, jnp.concatenate(parts, axis=1), NEG_BIG)
            m = m_scr[g]
            m_new = jnp.maximum(m, jnp.max(lg, axis=1, keepdims=True))
            alpha = jnp.exp(m - m_new)
            pr = jnp.where(sel_n, jnp.exp(lg - m_new), 0.0).astype(BF16)
            acc = alpha * acc_scr[g]
            for i, vt in enumerate(v_tiles):
                acc = acc + lax.dot_general(pr[:, i * PAGE_SIZE:(i + 1) * PAGE_SIZE], vt(g).astype(BF16),
                                            (((1,), (1,)), ((), ())), preferred_element_type=F32)
            m_scr[g] = m_new
            l_scr[g] = alpha * l_scr[g] + jnp.sum(pr.astype(F32), axis=1, keepdims=True)
            acc_scr[g] = acc

    ck_tiles = [cached(r) for r in ck_refs]
    cv_tiles = [cached(r) for r in cv_refs]

    @pl.when(p < n_steps - 1)
    def _():
        run(ck_tiles, cv_tiles, None)

    @pl.when(p == n_steps - 1)
    def _():
        run(ck_tiles, cv_tiles, (dl_ref[0, 0], dl_ref[0, 1]))

    @pl.when(p == n_steps)
    def _():
        run([fresh(kn_ref)], [fresh(vn_ref)], (dl_ref[1, 0], dl_ref[1, 1]))
        pieces = [None] * B_HEADS
        for g in range(B_KV_HEADS):
            og = acc_scr[g] / l_scr[g]
            for r in range(B_REP):
                pieces[g * B_REP + r] = og[r * SROWS:(r + 1) * SROWS, :]
        o_ref[0] = jnp.concatenate(pieces, axis=1)


def _dsa_sample(page_table, q_s, qi_s, kw_s, cki, ck, cv, layer, ki_new, k_new, v_new, delta_s, dec_seq, topk):
    nb = q_s.shape[0]
    n_pages = page_table.shape[1]
    pages = math.gcd(n_pages, SAMPLE_PAGES)
    n_steps = n_pages // pages
    width = pages * PAGE_SIZE
    n_keys = (n_steps + 1) * width

    def page(i):
        return lambda b, p, pt: (layer, pt[b, jnp.minimum(p, n_steps - 1) * pages + i], 0, 0)

    def page_kv(i):
        return lambda b, p, pt: (layer, pt[b, jnp.minimum(p, n_steps - 1) * pages + i], 0, 0, 0)

    per_b = lambda b, p, pt: (b, 0, 0)
    keys = pl.pallas_call(
        functools.partial(_dsa_sample_score_kernel, n_steps=n_steps, pages=pages, dec_seq=dec_seq),
        grid_spec=pltpu.PrefetchScalarGridSpec(
            num_scalar_prefetch=1, grid=(nb, n_steps + 1),
            in_specs=[pl.BlockSpec((1, SROWS, IDX_HEADS * IDX_DIM), per_b),
                      pl.BlockSpec((1, SROWS, LANES), per_b)]
            + [pl.BlockSpec((1, 1, IDX_DIM, PAGE_SIZE), page(i)) for i in range(pages)]
            + [pl.BlockSpec((1, IDX_DIM, PAGE_SIZE), per_b)],
            out_specs=pl.BlockSpec((1, SROWS, width), lambda b, p, pt: (b, 0, p))),
        out_shape=jax.ShapeDtypeStruct((nb, SROWS, n_keys), I32),
        compiler_params=_cparams(("arbitrary", "arbitrary")),
        name="dsa_sample_score",
    )(page_table, qi_s, kw_s, *([cki] * pages), ki_new)
    grows = B_REP * SROWS
    return pl.pallas_call(
        functools.partial(_dsa_sample_attend_kernel, n_steps=n_steps, pages=pages, topk=topk),
        grid_spec=pltpu.PrefetchScalarGridSpec(
            num_scalar_prefetch=1, grid=(nb, n_steps + 1),
            in_specs=[pl.BlockSpec((1, SROWS, B_WIDTH), per_b),
                      pl.BlockSpec((1, SROWS, n_keys), per_b)]
            + [pl.BlockSpec((1, 1, B_KV_HEADS, HEAD_DIM, PAGE_SIZE), page_kv(i)) for i in range(pages)] * 2
            + [pl.BlockSpec((1, B_KV_HEADS, HEAD_DIM, PAGE_SIZE), lambda b, p, pt: (b, 0, 0, 0)),
               pl.BlockSpec((1, B_KV_HEADS, HEAD_DIM, PAGE_SIZE), lambda b, p, pt: (b, 0, 0, 0)),
               pl.BlockSpec(delta_s.shape, lambda b, p, pt: (0, 0, 0, 0))],
            out_specs=pl.BlockSpec((1, SROWS, B_WIDTH), per_b),
            scratch_shapes=[pltpu.VMEM((SROWS, 1), I32), pltpu.VMEM((SROWS, 1), I32),
                            pltpu.VMEM((B_KV_HEADS, grows, 1), F32), pltpu.VMEM((B_KV_HEADS, grows, 1), F32),
                            pltpu.VMEM((B_KV_HEADS, grows, HEAD_DIM), F32)]),
        out_shape=jax.ShapeDtypeStruct((nb, SROWS, B_WIDTH), F32),
        compiler_params=_cparams(("arbitrary", "arbitrary")),
        name="dsa_sample_attend",
    )(page_table, q_s, keys, *([ck] * pages), *([cv] * pages), k_new, v_new, delta_s)


def _pool_kernel(c_ref, halo_ref, w_ref, scale_ref, o_ref, *, block, offset):
    i = pl.program_id(1)
    c = c_ref[0]
    halo = jnp.where(i == 0, 0.0, halo_ref[0])
    full = jnp.concatenate([halo, c], axis=0)
    sums = [full]
    for sh in (1, 2, 4, 8):
        sums.append(sums[-1] + pltpu.roll(sums[-1], sh, axis=0))
    lane = lax.broadcasted_iota(I32, (block, C_WIDTH), 1)
    pos = i * block + lax.broadcasted_iota(I32, (block, C_WIDTH), 0) - offset
    win = jnp.zeros((block, C_WIDTH), F32)
    cnt = jnp.ones((block, C_WIDTH), F32)
    for gi, w in enumerate(POOL_WINDOWS):
        in_g = (lane >= gi * C_GROUP_DIM) & (lane < (gi + 1) * C_GROUP_DIM)
        win = jnp.where(in_g, sums[gi + 1][POOL_HALO:, :], win)
        cnt = jnp.where(in_g, jnp.clip(pos + 1, 1, w).astype(F32), cnt)
    d = win / cnt - c
    o_ref[0] = jnp.dot(d.astype(BF16), w_ref[...], preferred_element_type=F32) * scale_ref[...]


def _pool(c, w_bd, scale, block, offset):
    n_seq, length, _ = c.shape
    ratio = block // POOL_HALO
    return pl.pallas_call(
        functools.partial(_pool_kernel, block=block, offset=offset),
        grid=(n_seq, length // block),
        in_specs=[pl.BlockSpec((1, block, C_WIDTH), lambda s, i: (s, i, 0)),
                  pl.BlockSpec((1, POOL_HALO, C_WIDTH), lambda s, i: (s, jnp.maximum(i * ratio - 1, 0), 0)),
                  pl.BlockSpec((C_WIDTH, C_WIDTH), lambda s, i: (0, 0)),
                  pl.BlockSpec((1, C_WIDTH), lambda s, i: (0, 0))],
        out_specs=pl.BlockSpec((1, block, C_WIDTH), lambda s, i: (s, i, 0)),
        out_shape=jax.ShapeDtypeStruct(c.shape, F32),
        compiler_params=_cparams(("parallel", "parallel")),
        name="pool",
    )(c, c, w_bd, scale)


def _outproj_kernel(x_ref, oa_ref, ob_ref, oc_ref, wo_ref, g_ref, wr_ref, br_ref, xo_ref, h_ref, gm_ref):
    mix = jnp.dot(oa_ref[...].astype(BF16), wo_ref[0:A_WIDTH, :], preferred_element_type=F32)
    mix += jnp.dot(ob_ref[...].astype(BF16), wo_ref[A_WIDTH:A_WIDTH + B_WIDTH, :], preferred_element_type=F32)
    mix += jnp.dot(oc_ref[...].astype(BF16), wo_ref[A_WIDTH + B_WIDTH:, :], preferred_element_type=F32)
    x = x_ref[...] + mix
    xo_ref[...] = x
    h = _rms(x, g_ref[...])
    h_ref[...] = h.astype(BF16)
    logits = jnp.dot(h, wr_ref[...], precision=HIGHEST, preferred_element_type=F32) + br_ref[...]
    lane = lax.broadcasted_iota(I32, logits.shape, 1)
    work = logits
    vals, hots = [], []
    for _ in range(TOP_K):
        m = jnp.max(work, axis=1, keepdims=True)
        idx = jnp.min(jnp.where(work == m, lane, N_EXPERTS), axis=1, keepdims=True)
        hot = lane == idx
        vals.append(m)
        hots.append(hot)
        work = jnp.where(hot, -jnp.inf, work)
    es = [jnp.exp(v - vals[0]) for v in vals]
    den = es[0] + es[1] + es[2] + es[3]
    gm = jnp.zeros(logits.shape, F32)
    for e, hot in zip(es, hots):
        gm = jnp.where(hot, e / den, gm)
    gm_ref[...] = gm


def _outproj(x, oa, ob, oc, wo, g, wr, br):
    nt = x.shape[0]
    tile = lambda n: pl.BlockSpec((TOK_TILE, n), lambda i: (i, 0))
    full = lambda a: pl.BlockSpec(a.shape, lambda i: (0, 0))
    return pl.pallas_call(
        _outproj_kernel,
        grid=(nt // TOK_TILE,),
        in_specs=[tile(D_MODEL), tile(A_WIDTH), tile(B_WIDTH), tile(C_WIDTH), full(wo), full(g), full(wr), full(br)],
        out_specs=[tile(D_MODEL), tile(D_MODEL), tile(N_EXPERTS)],
        out_shape=[jax.ShapeDtypeStruct((nt, D_MODEL), F32), jax.ShapeDtypeStruct((nt, D_MODEL), BF16),
                   jax.ShapeDtypeStruct((nt, N_EXPERTS), F32)],
        compiler_params=_cparams(("parallel",)),
        name="outproj",
    )(x, oa, ob, oc, wo, g, wr, br)


def _moe_kernel(h_ref, gm_ref, x_ref, w1_ref, b1_ref, w2_ref, b2_ref, gf_ref, o_ref, acc_scr, *, final):
    e = pl.program_id(1)

    @pl.when(e == 0)
    def _():
        acc_scr[...] = jnp.zeros_like(acc_scr)

    u = jnp.dot(h_ref[...], w1_ref[0, 0].astype(BF16), preferred_element_type=F32) + b1_ref[0, 0]
    glu = jnp.minimum(u[:, :D_FF], SWIGLU_LIMIT)
    lin = jnp.clip(u[:, D_FF:], -SWIGLU_LIMIT, SWIGLU_LIMIT)
    act = (glu * _sigmoid(SWIGLU_ALPHA * glu) * (lin + 1.0)).astype(BF16)
    out = jnp.dot(act, w2_ref[0, 0].astype(BF16), preferred_element_type=F32) + b2_ref[0, 0]
    gm = gm_ref[...]
    lane = lax.broadcasted_iota(I32, gm.shape, 1)
    gate = jnp.sum(jnp.where(lane == e, gm, 0.0), axis=1, keepdims=True)
    acc_scr[...] += gate * out

    @pl.when(e == N_EXPERTS - 1)
    def _():
        x = x_ref[...] + acc_scr[...]
        if final:
            x = _rms(x, gf_ref[...])
        o_ref[...] = x


def _moe(h, gm, x, w1, b1, w2, b2, gf, layer, final):
    nt = x.shape[0]
    tile = lambda n: pl.BlockSpec((TOK_TILE, n), lambda i, e: (i, 0))
    return pl.pallas_call(
        functools.partial(_moe_kernel, final=final),
        grid=(nt // TOK_TILE, N_EXPERTS),
        in_specs=[tile(D_MODEL), tile(N_EXPERTS), tile(D_MODEL),
                  pl.BlockSpec((1, 1, D_MODEL, 2 * D_FF), lambda i, e: (layer, e, 0, 0)),
                  pl.BlockSpec((1, 1, 1, 2 * D_FF), lambda i, e: (layer, e, 0, 0)),
                  pl.BlockSpec((1, 1, D_FF, D_MODEL), lambda i, e: (layer, e, 0, 0)),
                  pl.BlockSpec((1, 1, 1, D_MODEL), lambda i, e: (layer, e, 0, 0)),
                  pl.BlockSpec((1, D_MODEL), lambda i, e: (0, 0))],
        out_specs=tile(D_MODEL),
        out_shape=jax.ShapeDtypeStruct((nt, D_MODEL), F32),
        scratch_shapes=[pltpu.VMEM((TOK_TILE, D_MODEL), F32)],
        compiler_params=_cparams(("parallel", "arbitrary")),
        name="moe",
    )(h, gm, x, w1, b1[:, :, None, :], w2, b2[:, :, None, :], gf)


def _rel_bucket(dist):
    n = jnp.maximum(dist, 0)
    max_exact = N_BUCKETS // 2
    log_ratio = jnp.log(jnp.maximum(n, 1).astype(jnp.float32) / max_exact) / math.log(MAX_DISTANCE / max_exact)
    large = jnp.minimum(max_exact + (log_ratio * (N_BUCKETS - max_exact)).astype(jnp.int32), N_BUCKETS - 1)
    return jnp.where(n < max_exact, n, large)


def _bias_delta(rel_bias, dist):
    rb = rel_bias.astype(F32)
    d = rb[_rel_bucket(dist)] - rb[N_BUCKETS - 1]
    d = jnp.where(((dist >= 0) & (dist < MAX_DISTANCE))[..., None], d, 0.0)
    return jnp.moveaxis(d, -1, 0)


def _group_rows(delta, rows):
    return delta.reshape(B_KV_HEADS, B_REP * rows, delta.shape[-1])


def kernel(x_prompt, x_sample, cache_k, cache_v, cache_kidx, state_wkv, state_shift, state_pool, page_table,
           meta_tokens, rel_bias, norm_mix, norm_ffn, norm_final, w_in, w_out,
           a_mu, a_w0, a_w2, a_a0, a_a2, a_g2, a_kk, a_ka, a_rk, a_ln_w, a_ln_b,
           c_w, c_scale, moe_w_router, moe_b_router, moe_w1, moe_b1, moe_w2, moe_b2):
    assert x_prompt.shape[0] == 1
    depth = w_in.shape[0]
    seq = x_prompt.shape[1]
    nb, dec_seq = x_sample.shape[:2]
    assert dec_seq <= SROWS
    t_p = seq + N_META
    n_s = nb * dec_seq
    n_qblk = -(-t_p // Q_TILE)
    nkp = KEY_CHUNK * ((n_qblk + 4) // 4 + 1)
    nt = _round_up(max(t_p + n_s, Q_TILE * n_qblk), TOK_TILE)
    n_pages = page_table.shape[1]
    past = n_pages * PAGE_SIZE
    topk_p = min(TOPK_MAX, seq // 4)
    topk_s = min(TOPK_MAX, (past + dec_seq) // 4)
    n_phys = cache_k.shape[1]

    x = jnp.concatenate([meta_tokens.astype(F32), x_prompt[0], x_sample.reshape(n_s, D_MODEL),
                         jnp.zeros((nt - t_p - n_s, D_MODEL), F32)], axis=0)

    ones_blk = jnp.kron(jnp.eye(A_HEADS, dtype=F32), jnp.ones((HEAD_DIM, HEAD_DIM), F32))
    qi_ = jnp.arange(Q_TILE)[:, None]
    cj_ = jnp.arange(2 * Q_TILE)[None, :]
    side = TAIL_WIDTH - 2 * Q_TILE
    delta_p = jnp.pad(_bias_delta(rel_bias, Q_TILE + qi_ - cj_), ((0, 0), (0, 0), (side, side)))
    ts_ = jnp.arange(SROWS)[:, None]
    cs_ = jnp.arange(PAGE_SIZE)[None, :]
    delta_s = jnp.stack([_group_rows(_bias_delta(rel_bias, ts_ + PAGE_SIZE - cs_), SROWS),
                         _group_rows(_bias_delta(rel_bias, ts_ - cs_), SROWS)])

    outs = {n: [] for n in ("kp", "vp", "kip", "wkvp", "shp", "plp", "ks", "vs", "kis", "wkvs", "shs", "pls")}
    for l in range(depth):
        wl = w_in[l]
        b0 = A_PROJ
        w_cat = jnp.concatenate([
            wl[:, :A_PROJ], wl[:, b0:b0 + B_WIDTH + 2 * KV_WIDTH + IDX_HEADS * IDX_DIM],
            wl[:, b0 + B_PROJ - IDX_DIM - IDX_HEADS:b0 + B_PROJ],
            jnp.zeros((D_MODEL, LANES - IDX_DIM - IDX_HEADS), F32), wl[:, b0 + B_PROJ:]], axis=1).astype(BF16)
        pa, q, k, v, qi, kw, pc = _inproj(x, norm_mix[l][None], w_cat)

        sl_p = slice(0, t_p)
        sl_s = slice(t_p, t_p + n_s)
        outs["kp"].append(k[sl_p].reshape(1, t_p, B_KV_HEADS, HEAD_DIM))
        outs["vp"].append(v[sl_p].reshape(1, t_p, B_KV_HEADS, HEAD_DIM))
        outs["kip"].append(kw[sl_p, :IDX_DIM][None])
        outs["ks"].append(k[sl_s].reshape(nb, dec_seq, B_KV_HEADS, HEAD_DIM))
        outs["vs"].append(v[sl_s].reshape(nb, dec_seq, B_KV_HEADS, HEAD_DIM))
        outs["kis"].append(kw[sl_s, :IDX_DIM].reshape(nb, dec_seq, IDX_DIM))
        outs["shp"].append(pa[t_p - 1:t_p])
        pa_s = pa[sl_s].reshape(nb, dec_seq, A_PROJ)
        outs["shs"].append(pa_s[:, -1])
        pc_s = pc[sl_s].reshape(nb, dec_seq, C_WIDTH)
        outs["plp"].append(pc[t_p - POOL_BUF:t_p][None])
        pool_full = jnp.concatenate([state_pool[l], pc_s], axis=1)
        outs["pls"].append(pool_full[:, -POOL_BUF:])

        rw = (a_mu[l], a_w0[l], a_w2[l], a_a0[l], a_a2[l], a_g2[l], a_kk[l], a_ka[l], a_rk[l], a_ln_w[l], a_ln_b[l])
        oa_p, st_p = _rwkv(pa[None], jnp.zeros((1, 1, A_PROJ), F32), jnp.zeros((1, HEAD_DIM, A_WIDTH), F32),
                           rw, ones_blk, t_p, RWKV_BLOCK)
        pa_s8 = jnp.pad(pa_s, ((0, 0), (0, SROWS - dec_seq), (0, 0)))
        s0_s = state_wkv[l].astype(F32).transpose(0, 3, 1, 2).reshape(nb, HEAD_DIM, A_WIDTH)
        oa_s, st_s = _rwkv(pa_s8, state_shift[l][:, None, :], s0_s, rw, ones_blk, dec_seq, SROWS)
        untr = lambda s: s.reshape(-1, HEAD_DIM, A_HEADS, HEAD_DIM).transpose(0, 2, 3, 1)
        outs["wkvp"].append(untr(st_p).astype(state_wkv.dtype))
        outs["wkvs"].append(untr(st_s).astype(state_wkv.dtype))

        n_real = min(nt, nkp - KEY_FRONT_PAD)
        front = lambda a: jnp.pad(a[:n_real].astype(BF16), ((KEY_FRONT_PAD, nkp - KEY_FRONT_PAD - n_real), (0, 0)))
        ob_p = _dsa_prompt(q, qi, kw, front(k), front(v), front(kw[:, :IDX_DIM]), delta_p, n_qblk, topk_p)
        rows8 = lambda a: jnp.pad(a[sl_s].reshape(nb, dec_seq, -1), ((0, 0), (0, SROWS - dec_seq), (0, 0)))
        page8 = lambda a: jnp.pad(a[sl_s].reshape(nb, dec_seq, -1), ((0, 0), (0, PAGE_SIZE - dec_seq), (0, 0)))
        kv_page = lambda a: page8(a).reshape(nb, PAGE_SIZE, B_KV_HEADS, HEAD_DIM).transpose(0, 2, 3, 1)
        ob_s = _dsa_sample(page_table, rows8(q), rows8(qi), rows8(kw),
                           cache_kidx.transpose(0, 1, 3, 2), cache_k.transpose(0, 1, 3, 4, 2),
                           cache_v.transpose(0, 1, 3, 4, 2), l, page8(kw[:, :IDX_DIM]).transpose(0, 2, 1),
                           kv_page(k), kv_page(v), delta_s, dec_seq, topk_s)

        w_bd = jax.scipy.linalg.block_diag(*[c_w[l][gi] for gi in range(len(POOL_WINDOWS))]).astype(BF16)
        oc_p = _pool(pc[None], w_bd, c_scale[l][None], TOK_TILE, 0)
        pool_in = jnp.concatenate([jnp.zeros((nb, 1, C_WIDTH), F32), pool_full,
                                   jnp.zeros((nb, 2 * POOL_HALO - 1 - POOL_BUF - dec_seq, C_WIDTH), F32)], axis=1)
        oc_s = _pool(pool_in, w_bd, c_scale[l][None], POOL_HALO, 1)[:, POOL_HALO:POOL_HALO + dec_seq]

        def merge(p_rows, s_rows):
            w = p_rows.shape[-1]
            return jnp.concatenate([p_rows[:t_p], s_rows.reshape(n_s, w), jnp.zeros((nt - t_p - n_s, w), F32)], axis=0)

        oa = merge(oa_p[0], oa_s[:, :dec_seq])
        ob = merge(ob_p, ob_s[:, :dec_seq])
        oc = merge(oc_p[0], oc_s)
        x, h2, gm = _outproj(x, oa, ob, oc, w_out[l].astype(BF16), norm_ffn[l][None], moe_w_router[l],
                             moe_b_router[l][None])
        x = _moe(h2, gm, x, moe_w1, moe_b1, moe_w2, moe_b2, norm_final[None], l, l == depth - 1)

    y_prompt = x[N_META:t_p][None]
    y_sample = x[t_p:t_p + n_s].reshape(nb, dec_seq, D_MODEL)
    st = lambda n: jnp.stack(outs[n])
    return (y_prompt, y_sample, st("kp"), st("vp"), st("kip"), st("wkvp"), st("shp"), st("plp"),
            st("ks"), st("vs"), st("kis"), st("wkvs"), st("shs"), st("pls"))
```

```python
import functools
import math

import jax
import jax.numpy as jnp
import numpy as np
from jax import lax
from jax.experimental import pallas as pl
from jax.experimental.pallas import tpu as pltpu

F32 = jnp.float32
BF16 = jnp.bfloat16
I32 = jnp.int32
HIGHEST = lax.Precision.HIGHEST

D_MODEL = 1024
N_META = 16
HEAD_DIM = 64
A_WIDTH = 384
A_HEADS = 6
LORA_W, LORA_A, LORA_G = 64, 64, 128
A_PROJ = 3 * A_WIDTH + LORA_W + LORA_A + LORA_G
GN_EPS = 64e-5
B_WIDTH = 384
B_HEADS = 6
B_KV_HEADS = 2
B_REP = 3
KV_WIDTH = 128
IDX_HEADS = 4
IDX_DIM = 64
IDX_SCALE = (IDX_HEADS * IDX_DIM) ** -0.5
TOPK_MAX = 256
B_PROJ = B_WIDTH + 2 * KV_WIDTH + IDX_HEADS * IDX_DIM + IDX_DIM + IDX_HEADS
C_WIDTH = 256
C_GROUP_DIM = 64
POOL_WINDOWS = (2, 4, 8, 16)
POOL_BUF = 15
N_BUCKETS = 32
MAX_DISTANCE = 128
N_EXPERTS = 32
TOP_K = 4
D_FF = 1024
SWIGLU_LIMIT = 7.0
SWIGLU_ALPHA = 1.702
RMS_EPS = 1e-5
PAGE_SIZE = 128

LANES = 128
SUBLANES = 8
Q_TILE = 128
KEY_CHUNK = 512
KEY_FRONT_PAD = 128
TAIL_WIDTH = 640
TOP_M = 12
SAMPLE_PAGES = 16
TOK_TILE = 512
RWKV_BLOCK = 128
RWKV_CHUNK = 64
POOL_HALO = 16
VMEM_LIMIT = 56 * 1024 * 1024

INT_MIN = -(2 ** 31)
NEG_INF_KEY = np.int32(np.uint32(0xFF800000) ^ np.uint32(0x7FFFFFFF))
NEG_BIG = -1e30


def _round_up(x, m):
    return (x + m - 1) // m * m


def _cparams(sem):
    return pltpu.CompilerParams(dimension_semantics=sem, vmem_limit_bytes=VMEM_LIMIT)


def _rms(x, g):
    return x * lax.rsqrt(jnp.mean(x * x, axis=-1, keepdims=True) + RMS_EPS) * g


def _sigmoid(x):
    return 1.0 / (1.0 + jnp.exp(-x))


IN_COLS = (A_PROJ, B_WIDTH, KV_WIDTH, KV_WIDTH, IDX_HEADS * IDX_DIM, LANES, C_WIDTH)


def _inproj_kernel(x_ref, g_ref, w_ref, *out_refs):
    hb = _rms(x_ref[...], g_ref[...]).astype(BF16)
    off = 0
    for ref in out_refs:
        n = ref.shape[1]
        ref[...] = jnp.dot(hb, w_ref[:, off:off + n], preferred_element_type=F32)
        off += n


def _inproj(x, g, w):
    nt = x.shape[0]
    return pl.pallas_call(
        _inproj_kernel,
        grid=(nt // TOK_TILE,),
        in_specs=[pl.BlockSpec((TOK_TILE, D_MODEL), lambda i: (i, 0)),
                  pl.BlockSpec((1, D_MODEL), lambda i: (0, 0)),
                  pl.BlockSpec(w.shape, lambda i: (0, 0))],
        out_specs=[pl.BlockSpec((TOK_TILE, n), lambda i: (i, 0)) for n in IN_COLS],
        out_shape=[jax.ShapeDtypeStruct((nt, n), F32) for n in IN_COLS],
        compiler_params=_cparams(("parallel",)),
        name="inproj",
    )(x, g, w)


NN_DIMS = (((1,), (0,)), ((), ()))
NT_DIMS = (((1,), (1,)), ((), ()))
TN_DIMS = (((0,), (0,)), ((), ()))


def _dot3(a, b, dims=NN_DIMS):
    a_hi = a.astype(BF16)
    b_hi = b.astype(BF16)
    a_lo = (a - a_hi.astype(F32)).astype(BF16)
    b_lo = (b - b_hi.astype(F32)).astype(BF16)
    dg = functools.partial(lax.dot_general, dimension_numbers=dims, preferred_element_type=F32)
    return dg(a_hi, b_hi) + (dg(a_hi, b_lo) + dg(a_lo, b_hi))


def _rwkv_chunk_kernel(pa_ref, sh_ref, s0_ref, mu_ref, w0_ref, w2_ref, a0_ref, a2_ref, g2_ref, kk_ref, ka_ref,
                       rk_ref, lnw_ref, lnb_ref, ones_ref, tri_ref, o_ref, s_out_ref, carry_scr, s_scr,
                       *, seq_len, block, chunk):
    blk = pl.program_id(1)

    @pl.when(blk == 0)
    def _():
        carry_scr[...] = sh_ref[0]
        s_scr[...] = s0_ref[0]

    pa = pa_ref[0]
    row = lax.broadcasted_iota(I32, (block, 1), 0)
    prev = jnp.where(row == 0, carry_scr[...], pltpu.roll(pa, 1, axis=0))
    carry_scr[...] = pa[block - 1:block, :]
    xm = pa + (prev - pa) * mu_ref[...]
    r = xm[:, 0:A_WIDTH]
    k = xm[:, A_WIDTH:2 * A_WIDTH]
    v = xm[:, 2 * A_WIDTH:3 * A_WIDTH]
    lw = xm[:, 3 * A_WIDTH:3 * A_WIDTH + LORA_W]
    la = xm[:, 3 * A_WIDTH + LORA_W:3 * A_WIDTH + LORA_W + LORA_A]
    lg = xm[:, 3 * A_WIDTH + LORA_W + LORA_A:]

    def dot_hi(a, b):
        return jnp.dot(a, b, precision=HIGHEST, preferred_element_type=F32)

    z = -(w0_ref[...] + dot_hi(jnp.tanh(lw), w2_ref[...]))
    softplus = jnp.maximum(z, 0.0) + jnp.log(1.0 + jnp.exp(-jnp.abs(z)))
    log_w = -jnp.exp(-softplus - 0.5)
    a = _sigmoid(a0_ref[...] + dot_hi(la, a2_ref[...]))
    g = dot_hi(_sigmoid(lg), g2_ref[...])
    ones_blk = ones_ref[...]
    kk = k * kk_ref[...]
    kk = kk / jnp.maximum(jnp.sqrt(dot_hi(kk * kk, ones_blk)), 1e-12)
    k2 = k * (1.0 + (a - 1.0) * ka_ref[...])
    kka = kk * a

    valid = (blk * block + row) < seq_len
    log_w = jnp.where(valid, log_w, 0.0)
    kk_m, kka_m, k2_m, r_m, v_m = (jnp.where(valid, t, 0.0) for t in (kk, kka, k2, r, v))

    cum = dot_hi(tri_ref[0], log_w)
    cum_end = dot_hi(tri_ref[1], log_w)
    kap = kk_m * jnp.exp(cum - log_w)
    rt = r_m * jnp.exp(cum)
    inv_p = jnp.exp(-cum)
    kt = k2_m * inv_p
    bt = kka_m * inv_p
    to_end = jnp.exp(cum_end - cum)
    k_end = k2_m * to_end
    b_end = kka_m * to_end
    p_end = jnp.exp(cum_end)

    sub = min(16, chunk)
    ti = lax.broadcasted_iota(I32, (chunk, chunk), 0)
    tj = lax.broadcasted_iota(I32, (chunk, chunk), 1)
    heads = range(A_HEADS)
    o_chunks = []
    for c in range(block // chunk):
        rs = slice(c * chunk, (c + 1) * chunk)
        hsl = lambda t, h: t[rs, h * HEAD_DIM:(h + 1) * HEAD_DIM]
        lhs = [jnp.concatenate([hsl(kap, h), hsl(rt, h)], axis=0) for h in heads]
        rhs = [jnp.concatenate([hsl(kt, h), hsl(bt, h)], axis=0) for h in heads]
        vh = [hsl(v_m, h) for h in heads]
        gram = [_dot3(lhs[h], rhs[h], NT_DIMS) for h in heads]
        from_s = [_dot3(lhs[h], s_scr[h], NT_DIMS) for h in heads]
        a1 = [jnp.where(ti > tj, gram[h][:chunk, :chunk], 0.0) for h in heads]
        a2 = [jnp.where(ti > tj, gram[h][:chunk, chunk:], 0.0) for h in heads]
        a3 = [jnp.where(ti >= tj, gram[h][chunk:, :chunk], 0.0) for h in heads]
        a4 = [jnp.where(ti >= tj, gram[h][chunk:, chunk:], 0.0) for h in heads]
        u = [from_s[h][:chunk] + _dot3(a1[h], vh[h]) for h in heads]
        zs = [[] for _ in heads]
        for jb in range(chunk // sub):
            js = slice(jb * sub, (jb + 1) * sub)
            zj = []
            for h in heads:
                t = u[h][js]
                if jb > 0:
                    t = t - _dot3(a2[h][js, :jb * sub], jnp.concatenate(zs[h], axis=0))
                zj.append(t)
            for i in range(sub - 1):
                for h in heads:
                    col = jnp.broadcast_to(a2[h][js, jb * sub + i:jb * sub + i + 1], (sub, HEAD_DIM))
                    zj[h] = zj[h] - col * zj[h][i:i + 1, :]
            for h in heads:
                zs[h].append(zj[h])
        zh = [jnp.concatenate(zs[h], axis=0) for h in heads]
        vz = [jnp.concatenate([vh[h], zh[h]], axis=0) for h in heads]
        o_chunks.append(jnp.concatenate(
            [from_s[h][chunk:] + _dot3(jnp.concatenate([a3[h], -a4[h]], axis=1), vz[h]) for h in heads], axis=1))
        for h in heads:
            ends = jnp.concatenate([hsl(k_end, h), -hsl(b_end, h)], axis=0)
            s_scr[h] = s_scr[h] * hsl(p_end, h)[0:1, :] + _dot3(vz[h], ends, TN_DIMS)
    s_out_ref[0] = s_scr[...]

    o = jnp.concatenate(o_chunks, axis=0) if len(o_chunks) > 1 else o_chunks[0]
    inv = 1.0 / HEAD_DIM
    mean = dot_hi(o, ones_blk) * inv
    cen = o - mean
    var = dot_hi(cen * cen, ones_blk) * inv
    o = cen * lax.rsqrt(var + GN_EPS) * lnw_ref[...] + lnb_ref[...]
    bonus = dot_hi(r * k2 * rk_ref[...], ones_blk) * v
    o_ref[0] = (o + bonus) * g


def _rwkv_chunked(pa, shift_prev, s0, params, ones_blk, seq_len, block, chunk):
    n_seq, length, _ = pa.shape
    n_blk = -(-seq_len // block)
    idx = np.arange(block)
    same = (idx[:, None] // chunk) == (idx[None, :] // chunk)
    tri = jnp.asarray(np.stack([same & (idx[None, :] <= idx[:, None]), same]).astype(np.float32))
    row_spec = lambda n: pl.BlockSpec((1, n), lambda s, i: (0, 0))
    mat_spec = lambda a: pl.BlockSpec(a.shape, lambda s, i: (0,) * a.ndim)
    mu, w0, w2, a0, a2, g2, k_k, k_a, r_k, ln_w, ln_b = params
    kern = functools.partial(_rwkv_chunk_kernel, seq_len=seq_len, block=block, chunk=chunk)
    state_spec = pl.BlockSpec((1, A_HEADS, HEAD_DIM, HEAD_DIM), lambda s, i: (s, 0, 0, 0))
    return pl.pallas_call(
        kern,
        grid=(n_seq, n_blk),
        in_specs=[pl.BlockSpec((1, block, A_PROJ), lambda s, i: (s, i, 0)),
                  pl.BlockSpec((1, 1, A_PROJ), lambda s, i: (s, 0, 0)),
                  state_spec,
                  row_spec(A_PROJ), row_spec(A_WIDTH), mat_spec(w2), row_spec(A_WIDTH), mat_spec(a2),
                  mat_spec(g2), row_spec(A_WIDTH), row_spec(A_WIDTH), row_spec(A_WIDTH), row_spec(A_WIDTH),
                  row_spec(A_WIDTH), mat_spec(ones_blk), mat_spec(tri)],
        out_specs=[pl.BlockSpec((1, block, A_WIDTH), lambda s, i: (s, i, 0)), state_spec],
        out_shape=[jax.ShapeDtypeStruct((n_seq, length, A_WIDTH), F32),
                   jax.ShapeDtypeStruct((n_seq, A_HEADS, HEAD_DIM, HEAD_DIM), F32)],
        scratch_shapes=[pltpu.VMEM((1, A_PROJ), F32), pltpu.VMEM((A_HEADS, HEAD_DIM, HEAD_DIM), F32)],
        compiler_params=_cparams(("arbitrary", "arbitrary")),
        name="rwkv",
    )(pa, shift_prev, s0, mu[None], w0[None], w2, a0[None], a2, g2, k_k[None], k_a[None], r_k[None],
      ln_w[None], ln_b[None], ones_blk, tri)


def _sortable_key(s):
    bits = pltpu.bitcast(s + 0.0, I32)
    return bits ^ ((bits >> 31) & jnp.int32(0x7FFFFFFF))


def _index_scores(qi_b, wi, ki_b, transposed=False):
    s = None
    for h in range(IDX_HEADS):
        d = lax.dot_general(qi_b[:, h * IDX_DIM:(h + 1) * IDX_DIM], ki_b,
                            (((1,), (0 if transposed else 1,)), ((), ())), preferred_element_type=F32)
        t = jnp.maximum(d, 0.0) * wi[:, h:h + 1]
        s = t if s is None else s + t
    return s * IDX_SCALE


def _topk_threshold(count_fn, count_tie_fn, rows, topk):
    kf = jnp.float32(topk)
    c0 = count_fn(jnp.zeros((rows, 1), I32), False)
    prefix0 = jnp.where(c0 >= kf, jnp.int32(0), jnp.int32(INT_MIN))

    def bit_step(i, prefix):
        cand = prefix | lax.shift_left(jnp.int32(1), jnp.int32(30) - i)
        return jnp.where(count_fn(cand, False) >= kf, cand, prefix)

    tau = lax.fori_loop(0, 31, bit_step, prefix0)
    c_gt = count_fn(tau, True)
    c_ge = count_fn(tau, False)
    need = kf - c_gt
    excess = jnp.max(c_ge - c_gt - need)

    def tie_search(_):
        def jbit(i, jl):
            cand = jl | lax.shift_left(jnp.int32(1), jnp.int32(14) - i)
            return jnp.where(count_tie_fn(tau, cand) < need, cand, jl)
        return lax.fori_loop(0, 15, jbit, jnp.zeros((rows, 1), I32))

    jlim = lax.cond(excess > 0.0, tie_search, lambda _: jnp.full((rows, 1), 2 ** 30, I32), 0)
    return tau, jlim


def _select(keys, aidx, tau, jlim):
    return ((keys > tau) | ((keys == tau) & (aidx <= jlim))) & (keys > jnp.int32(NEG_INF_KEY))


def _softmax_step(state, qg, kt, vt, sel_rows, delta):
    m, l, acc = state
    lg = lax.dot_general(qg, kt, (((1,), (1,)), ((), ())), preferred_element_type=F32)
    if delta is not None:
        lg = lg + delta
    lg = jnp.where(sel_rows, lg, NEG_BIG)
    m_new = jnp.maximum(m, jnp.max(lg, axis=1, keepdims=True))
    alpha = jnp.exp(m - m_new)
    p = jnp.where(sel_rows, jnp.exp(lg - m_new), 0.0)
    l = alpha * l + jnp.sum(p, axis=1, keepdims=True)
    acc = alpha * acc + jnp.dot(p.astype(BF16), vt, preferred_element_type=F32)
    return m_new, l, acc


def _dsa_prompt_kernel(q_ref, qi_ref, kw_ref, kp_ref, vp_ref, kip_ref, dl_ref, o_ref, key_scr, cand_scr, *, topk):
    j = pl.program_id(0)
    rows = Q_TILE
    n_chunks = (j + 5) // 4
    qi_b = qi_ref[...].astype(BF16)
    wi = kw_ref[:, IDX_DIM:IDX_DIM + IDX_HEADS]
    qpos = j * Q_TILE + lax.broadcasted_iota(I32, (rows, 1), 0)
    lane_c = lax.broadcasted_iota(I32, (1, KEY_CHUNK), 1)

    def score_chunk(c, carry):
        a0 = pl.multiple_of(c * KEY_CHUNK, KEY_CHUNK)
        s = _index_scores(qi_b, wi, kip_ref[pl.ds(a0, KEY_CHUNK), :])
        kpos = a0 - KEY_FRONT_PAD + lane_c
        s = jnp.where((kpos <= qpos) & (kpos >= 0), s, -jnp.inf)
        key = _sortable_key(s)
        key_scr[:, pl.ds(a0, KEY_CHUNK)] = key
        return carry

    lax.fori_loop(0, n_chunks, score_chunk, 0)

    key_scr[:, pl.ds(pl.multiple_of(n_chunks * KEY_CHUNK, KEY_CHUNK), KEY_CHUNK)] = jnp.full(
        (rows, KEY_CHUNK), NEG_INF_KEY, I32)

    n_fold = KEY_CHUNK // LANES

    def chunk_at(ref, c):
        return ref[:, pl.ds(pl.multiple_of(c * KEY_CHUNK, KEY_CHUNK), KEY_CHUNK)]

    def count32(cand, strict):
        def body(c, acc):
            kk = chunk_at(key_scr, c)
            hit = (kk > cand) if strict else (kk >= cand)
            m = jnp.where(hit, 1.0, 0.0)
            for i in range(n_fold):
                acc = acc + m[:, i * LANES:(i + 1) * LANES]
            return acc
        acc = lax.fori_loop(0, n_chunks, body, jnp.zeros((rows, LANES), F32))
        return jnp.sum(acc, axis=1, keepdims=True)

    kf = jnp.float32(topk)

    def kth_largest(count_ge):
        def bit_step(i, prefix):
            cand = prefix | lax.shift_left(jnp.int32(1), jnp.int32(30) - i)
            return jnp.where(count_ge(cand) >= kf, cand, prefix)
        nonneg = count_ge(jnp.zeros((rows, 1), I32)) >= kf
        return lax.fori_loop(0, 31, bit_step, jnp.where(nonneg, jnp.int32(0), jnp.int32(INT_MIN)))

    pair = 2 * SUBLANES
    for rp in range(rows // pair):
        r0 = rp * pair

        def insert_chunk(c, lists, r0=r0):
            a0 = pl.multiple_of(c * KEY_CHUNK, KEY_CHUNK)
            blk = key_scr[r0:r0 + pair, pl.ds(a0, KEY_CHUNK)]
            lists = list(lists)
            for half in range(2):
                for t in range(n_fold):
                    x = blk[half * SUBLANES:(half + 1) * SUBLANES, t * LANES:(t + 1) * LANES]
                    for i in range(TOP_M):
                        a = lists[half * TOP_M + i]
                        lists[half * TOP_M + i] = jnp.maximum(a, x)
                        x = jnp.minimum(a, x)
            return tuple(lists)

        lists = lax.fori_loop(0, n_chunks, insert_chunk,
                              tuple(jnp.full((SUBLANES, LANES), INT_MIN, I32) for _ in range(2 * TOP_M)))
        for half in range(2):
            for i in range(TOP_M):
                cand_scr[r0 + half * SUBLANES:r0 + (half + 1) * SUBLANES, i * LANES:(i + 1) * LANES] = lists[half * TOP_M + i]

    def count_cand(cand):
        acc = jnp.zeros((rows, LANES), F32)
        for i in range(TOP_M):
            acc = acc + jnp.where(cand_scr[:, i * LANES:(i + 1) * LANES] >= cand, 1.0, 0.0)
        return jnp.sum(acc, axis=1, keepdims=True)

    tau_c = kth_largest(count_cand)
    gt_c = count32(tau_c, True)
    ge_c = count32(tau_c, False)
    wrong = jnp.max(jnp.where((gt_c < kf) & (ge_c >= kf), 0.0, 1.0))

    def full_search(_):
        t = kth_largest(lambda cand: count32(cand, False))
        return t, count32(t, True), count32(t, False)

    tau, c_gt, c_ge = lax.cond(wrong > 0.0, full_search, lambda _: (tau_c, gt_c, ge_c), 0)
    need = kf - c_gt
    excess = jnp.max(c_ge - c_gt - need)

    def tie_pass(_):
        ri = lax.broadcasted_iota(I32, (LANES, 2 * LANES), 0)
        ci = lax.broadcasted_iota(I32, (LANES, 2 * LANES), 1)
        tri = jnp.where((ri <= ci) | (ci >= LANES), 1.0, 0.0).astype(BF16)

        def body(c, carry):
            seen, jacc = carry
            a0 = pl.multiple_of(c * KEY_CHUNK, KEY_CHUNK)
            kk = key_scr[:, pl.ds(a0, KEY_CHUNK)]
            for i in range(n_fold):
                tie = kk[:, i * LANES:(i + 1) * LANES] == tau
                pre = jnp.dot(jnp.where(tie, 1.0, 0.0).astype(BF16), tri, preferred_element_type=F32)
                rank = jnp.where(tie, seen + pre[:, :LANES], 1e9)
                aidx = a0 + i * LANES + lax.broadcasted_iota(I32, (1, LANES), 1)
                jacc = jnp.maximum(jacc, jnp.where(rank <= need, aidx, -1))
                seen = seen + pre[:, LANES:]
            return seen, jacc
        _, jacc = lax.fori_loop(0, n_chunks, body, (jnp.zeros((rows, LANES), F32), jnp.full((rows, LANES), -1, I32)))
        return jnp.max(jacc, axis=1, keepdims=True)

    jlim = lax.cond(excess > 0.0, tie_pass, lambda _: jnp.full((rows, 1), 2 ** 30, I32), 0)
    jlim = jnp.where(tau == jnp.int32(NEG_INF_KEY), jnp.int32(-1), jlim)
    tau_b = jnp.broadcast_to(tau, (rows, LANES))
    tau_m1_b = tau_b - 1

    q = q_ref[...] * (HEAD_DIM ** -0.5)
    qh = [q[:, h * HEAD_DIM:(h + 1) * HEAD_DIM].astype(BF16) for h in range(B_HEADS)]

    def init_state():
        return (jnp.full((rows, 1), NEG_BIG, F32), jnp.zeros((rows, 1), F32), jnp.zeros((rows, HEAD_DIM), F32))

    def attend(a0, width, states, delta):
        kk = key_scr[:, pl.ds(a0, width)]
        aidx = a0 + lax.broadcasted_iota(I32, (1, width), 1)
        thr = jnp.where(aidx <= jlim, jnp.concatenate([tau_m1_b] * (width // LANES), axis=1),
                        jnp.concatenate([tau_b] * (width // LANES), axis=1))
        madd = jnp.where(kk > thr, 0.0, NEG_BIG)
        kts = [kp_ref[pl.ds(a0, width), g * HEAD_DIM:(g + 1) * HEAD_DIM] for g in range(B_KV_HEADS)]
        vts = [vp_ref[pl.ds(a0, width), g * HEAD_DIM:(g + 1) * HEAD_DIM] for g in range(B_KV_HEADS)]
        lgs = [lax.dot_general(qh[h], kts[h // B_REP], (((1,), (1,)), ((), ())), preferred_element_type=F32)
               for h in range(B_HEADS)]
        mid = []
        for h in range(B_HEADS):
            m, l, _ = states[h]
            lg = lgs[h] + madd
            if delta is not None:
                lg = lg + delta[h]
            m_new = jnp.maximum(m, jnp.max(lg, axis=1, keepdims=True))
            alpha = jnp.exp(m - m_new)
            p = jnp.exp(lg - m_new)
            mid.append((m_new, alpha, alpha * l + jnp.sum(p, axis=1, keepdims=True), p.astype(BF16)))
        out = []
        for h in range(B_HEADS):
            m_new, alpha, l, pb = mid[h]
            acc = alpha * states[h][2] + jnp.dot(pb, vts[h // B_REP], preferred_element_type=F32)
            out.append((m_new, l, acc))
        return tuple(out)

    def far_chunk(c, states):
        return attend(pl.multiple_of(c * KEY_CHUNK, KEY_CHUNK), KEY_CHUNK, states, None)

    n_far = j // 4
    states = lax.fori_loop(0, n_far, far_chunk, tuple(init_state() for _ in range(B_HEADS)))
    d0 = pl.multiple_of((3 - j % 4) * Q_TILE, Q_TILE)
    delta = [dl_ref[h, :, pl.ds(d0, TAIL_WIDTH)] for h in range(B_HEADS)]
    states = attend(pl.multiple_of(n_far * KEY_CHUNK, KEY_CHUNK), TAIL_WIDTH, states, delta)

    o_ref[...] = jnp.concatenate([acc / l for _, l, acc in states], axis=1)


def _dsa_prompt(q, qi, kw, kp, vp, kip, delta, n_qblk, topk):
    nkp = kp.shape[0]
    full = lambda a: pl.BlockSpec(a.shape, lambda j: (0,) * a.ndim)
    return pl.pallas_call(
        functools.partial(_dsa_prompt_kernel, topk=topk),
        grid=(n_qblk,),
        in_specs=[pl.BlockSpec((Q_TILE, B_WIDTH), lambda j: (j, 0)),
                  pl.BlockSpec((Q_TILE, IDX_HEADS * IDX_DIM), lambda j: (j, 0)),
                  pl.BlockSpec((Q_TILE, LANES), lambda j: (j, 0)),
                  full(kp), full(vp), full(kip), full(delta)],
        out_specs=pl.BlockSpec((Q_TILE, B_WIDTH), lambda j: (j, 0)),
        out_shape=jax.ShapeDtypeStruct((n_qblk * Q_TILE, B_WIDTH), F32),
        scratch_shapes=[pltpu.VMEM((Q_TILE, nkp), I32), pltpu.VMEM((Q_TILE, TOP_M * LANES), I32)],
        compiler_params=_cparams(("arbitrary",)),
        name="dsa_prompt",
    )(q, qi, kw, kp, vp, kip, delta)


SROWS = 8


def _dsa_sample_score_kernel(pt_ref, qi_ref, kw_ref, *refs, n_steps, pages, dec_seq):
    cki_refs, kin_ref, key_ref = refs[:pages], refs[pages], refs[pages + 1]
    p = pl.program_id(1)
    qi_b = qi_ref[0].astype(BF16)
    wi = kw_ref[0][:, IDX_DIM:IDX_DIM + IDX_HEADS]
    t = lax.broadcasted_iota(I32, (SROWS, 1), 0)
    c = lax.broadcasted_iota(I32, (1, PAGE_SIZE), 1)

    @pl.when(p < n_steps)
    def _():
        for i in range(pages):
            s = _index_scores(qi_b, wi, cki_refs[i][0, 0].astype(BF16), transposed=True)
            key_ref[0, :, i * PAGE_SIZE:(i + 1) * PAGE_SIZE] = _sortable_key(jnp.where(t < dec_seq, s, -jnp.inf))

    @pl.when(p == n_steps)
    def _():
        s = _index_scores(qi_b, wi, kin_ref[0].astype(BF16), transposed=True)
        key_ref[0, :, 0:PAGE_SIZE] = _sortable_key(jnp.where((t < dec_seq) & (c <= t), s, -jnp.inf))
        if pages > 1:
            key_ref[0, :, PAGE_SIZE:] = jnp.full((SROWS, (pages - 1) * PAGE_SIZE), NEG_INF_KEY, I32)


def _dsa_sample_attend_kernel(pt_ref, q_ref, key_ref, *refs, n_steps, pages, topk):
    ck_refs, cv_refs = refs[:pages], refs[pages:2 * pages]
    kn_ref, vn_ref, dl_ref, o_ref, tau_scr, jl_scr, m_scr, l_scr, acc_scr = refs[2 * pages:]
    p = pl.program_id(1)
    width = pages * PAGE_SIZE

    @pl.when(p == 0)
    def _():
        keys = key_ref[0]
        aidx = lax.broadcasted_iota(I32, (1, keys.shape[1]), 1)

        def count_fn(cand, strict):
            hit = (keys > cand) if strict else (keys >= cand)
            return jnp.sum(jnp.where(hit, 1.0, 0.0), axis=1, keepdims=True)

        def count_tie_fn(tau, jl):
            return jnp.sum(jnp.where((keys == tau) & (aidx < jl), 1.0, 0.0), axis=1, keepdims=True)

        tau, jlim = _topk_threshold(count_fn, count_tie_fn, SROWS, topk)
        tau_scr[...] = tau
        jl_scr[...] = jlim
        m_scr[...] = jnp.full(m_scr.shape, NEG_BIG, F32)
        l_scr[...] = jnp.zeros(l_scr.shape, F32)
        acc_scr[...] = jnp.zeros(acc_scr.shape, F32)

    a0 = pl.multiple_of(p * width, width)
    kk = key_ref[0, :, pl.ds(a0, width)]
    aidx = a0 + lax.broadcasted_iota(I32, (1, width), 1)
    sel = _select(kk, aidx, tau_scr[...], jl_scr[...])
    sel3 = jnp.concatenate([sel] * B_REP, axis=0)
    q = q_ref[0] * (HEAD_DIM ** -0.5)

    def cached(ref):
        return lambda g: ref[0, 0, g]

    def fresh(ref):
        return lambda g: ref[0, g]

    def run(k_tiles, v_tiles, delta):
        n = len(k_tiles)
        sel_n = sel3[:, :n * PAGE_SIZE]
        for g in range(B_KV_HEADS):
            qg = jnp.concatenate([q[:, (g * B_REP + r) * HEAD_DIM:(g * B_REP + r + 1) * HEAD_DIM]
                                  for r in range(B_REP)], axis=0).astype(BF16)
            parts = [jnp.dot(qg, kt(g).astype(BF16), preferred_element_type=F32) for kt in k_tiles]
            if delta is not None:
                parts[-1] = parts[-1] + delta[g]
            lg = jnp.where(sel_n, jnp.concatenate(parts, axis=1), NEG_BIG)
            m = m_scr[g]
            m_new = jnp.maximum(m, jnp.max(lg, axis=1, keepdims=True))
            alpha = jnp.exp(m - m_new)
            pr = jnp.where(sel_n, jnp.exp(lg - m_new), 0.0).astype(BF16)
            acc = alpha * acc_scr[g]
            for i, vt in enumerate(v_tiles):
                acc = acc + lax.dot_general(pr[:, i * PAGE_SIZE:(i + 1) * PAGE_SIZE], vt(g).astype(BF16),
                                            (((1,), (1,)), ((), ())), preferred_element_type=F32)
            m_scr[g] = m_new
            l_scr[g] = alpha * l_scr[g] + jnp.sum(pr.astype(F32), axis=1, keepdims=True)
            acc_scr[g] = acc

    ck_tiles = [cached(r) for r in ck_refs]
    cv_tiles = [cached(r) for r in cv_refs]

    @pl.when(p < n_steps - 1)
    def _():
        run(ck_tiles, cv_tiles, None)

    @pl.when(p == n_steps - 1)
    def _():
        run(ck_tiles, cv_tiles, (dl_ref[0, 0], dl_ref[0, 1]))

    @pl.when(p == n_steps)
    def _():
        run([fresh(kn_ref)], [fresh(vn_ref)], (dl_ref[1, 0], dl_ref[1, 1]))
        pieces = [None] * B_HEADS
        for g in range(B_KV_HEADS):
            og = acc_scr[g] / l_scr[g]
            for r in range(B_REP):
                pieces[g * B_REP + r] = og[r * SROWS:(r + 1) * SROWS, :]
        o_ref[0] = jnp.concatenate(pieces, axis=1)


def _dsa_sample(page_table, q_s, qi_s, kw_s, cki, ck, cv, layer, ki_new, k_new, v_new, delta_s, dec_seq, topk):
    nb = q_s.shape[0]
    n_pages = page_table.shape[1]
    pages = math.gcd(n_pages, SAMPLE_PAGES)
    n_steps = n_pages // pages
    width = pages * PAGE_SIZE
    n_keys = (n_steps + 1) * width

    def page(i):
        return lambda b, p, pt: (layer, pt[b, jnp.minimum(p, n_steps - 1) * pages + i], 0, 0)

    def page_kv(i):
        return lambda b, p, pt: (layer, pt[b, jnp.minimum(p, n_steps - 1) * pages + i], 0, 0, 0)

    per_b = lambda b, p, pt: (b, 0, 0)
    keys = pl.pallas_call(
        functools.partial(_dsa_sample_score_kernel, n_steps=n_steps, pages=pages, dec_seq=dec_seq),
        grid_spec=pltpu.PrefetchScalarGridSpec(
            num_scalar_prefetch=1, grid=(nb, n_steps + 1),
            in_specs=[pl.BlockSpec((1, SROWS, IDX_HEADS * IDX_DIM), per_b),
                      pl.BlockSpec((1, SROWS, LANES), per_b)]
            + [pl.BlockSpec((1, 1, IDX_DIM, PAGE_SIZE), page(i)) for i in range(pages)]
            + [pl.BlockSpec((1, IDX_DIM, PAGE_SIZE), per_b)],
            out_specs=pl.BlockSpec((1, SROWS, width), lambda b, p, pt: (b, 0, p))),
        out_shape=jax.ShapeDtypeStruct((nb, SROWS, n_keys), I32),
        compiler_params=_cparams(("arbitrary", "arbitrary")),
        name="dsa_sample_score",
    )(page_table, qi_s, kw_s, *([cki] * pages), ki_new)
    grows = B_REP * SROWS
    return pl.pallas_call(
        functools.partial(_dsa_sample_attend_kernel, n_steps=n_steps, pages=pages, topk=topk),
        grid_spec=pltpu.PrefetchScalarGridSpec(
            num_scalar_prefetch=1, grid=(nb, n_steps + 1),
            in_specs=[pl.BlockSpec((1, SROWS, B_WIDTH), per_b),
                      pl.BlockSpec((1, SROWS, n_keys), per_b)]
            + [pl.BlockSpec((1, 1, B_KV_HEADS, HEAD_DIM, PAGE_SIZE), page_kv(i)) for i in range(pages)] * 2
            + [pl.BlockSpec((1, B_KV_HEADS, HEAD_DIM, PAGE_SIZE), lambda b, p, pt: (b, 0, 0, 0)),
               pl.BlockSpec((1, B_KV_HEADS, HEAD_DIM, PAGE_SIZE), lambda b, p, pt: (b, 0, 0, 0)),
               pl.BlockSpec(delta_s.shape, lambda b, p, pt: (0, 0, 0, 0))],
            out_specs=pl.BlockSpec((1, SROWS, B_WIDTH), per_b),
            scratch_shapes=[pltpu.VMEM((SROWS, 1), I32), pltpu.VMEM((SROWS, 1), I32),
                            pltpu.VMEM((B_KV_HEADS, grows, 1), F32), pltpu.VMEM((B_KV_HEADS, grows, 1), F32),
                            pltpu.VMEM((B_KV_HEADS, grows, HEAD_DIM), F32)]),
        out_shape=jax.ShapeDtypeStruct((nb, SROWS, B_WIDTH), F32),
        compiler_params=_cparams(("arbitrary", "arbitrary")),
        name="dsa_sample_attend",
    )(page_table, q_s, keys, *([ck] * pages), *([cv] * pages), k_new, v_new, delta_s)


def _pool_kernel(c_ref, halo_ref, w_ref, scale_ref, o_ref, *, block, offset):
    i = pl.program_id(1)
    c = c_ref[0]
    halo = jnp.where(i == 0, 0.0, halo_ref[0])
    full = jnp.concatenate([halo, c], axis=0)
    sums = [full]
    for sh in (1, 2, 4, 8):
        sums.append(sums[-1] + pltpu.roll(sums[-1], sh, axis=0))
    lane = lax.broadcasted_iota(I32, (block, C_WIDTH), 1)
    pos = i * block + lax.broadcasted_iota(I32, (block, C_WIDTH), 0) - offset
    win = jnp.zeros((block, C_WIDTH), F32)
    cnt = jnp.ones((block, C_WIDTH), F32)
    for gi, w in enumerate(POOL_WINDOWS):
        in_g = (lane >= gi * C_GROUP_DIM) & (lane < (gi + 1) * C_GROUP_DIM)
        win = jnp.where(in_g, sums[gi + 1][POOL_HALO:, :], win)
        cnt = jnp.where(in_g, jnp.clip(pos + 1, 1, w).astype(F32), cnt)
    d = win / cnt - c
    o_ref[0] = jnp.dot(d.astype(BF16), w_ref[...], preferred_element_type=F32) * scale_ref[...]


def _pool(c, w_bd, scale, block, offset):
    n_seq, length, _ = c.shape
    ratio = block // POOL_HALO
    return pl.pallas_call(
        functools.partial(_pool_kernel, block=block, offset=offset),
        grid=(n_seq, length // block),
        in_specs=[pl.BlockSpec((1, block, C_WIDTH), lambda s, i: (s, i, 0)),
                  pl.BlockSpec((1, POOL_HALO, C_WIDTH), lambda s, i: (s, jnp.maximum(i * ratio - 1, 0), 0)),
                  pl.BlockSpec((C_WIDTH, C_WIDTH), lambda s, i: (0, 0)),
                  pl.BlockSpec((1, C_WIDTH), lambda s, i: (0, 0))],
        out_specs=pl.BlockSpec((1, block, C_WIDTH), lambda s, i: (s, i, 0)),
        out_shape=jax.ShapeDtypeStruct(c.shape, F32),
        compiler_params=_cparams(("parallel", "parallel")),
        name="pool",
    )(c, c, w_bd, scale)


def _outproj_kernel(x_ref, oa_ref, ob_ref, oc_ref, wo_ref, g_ref, wr_ref, br_ref, xo_ref, h_ref, gm_ref):
    mix = jnp.dot(oa_ref[...].astype(BF16), wo_ref[0:A_WIDTH, :], preferred_element_type=F32)
    mix += jnp.dot(ob_ref[...].astype(BF16), wo_ref[A_WIDTH:A_WIDTH + B_WIDTH, :], preferred_element_type=F32)
    mix += jnp.dot(oc_ref[...].astype(BF16), wo_ref[A_WIDTH + B_WIDTH:, :], preferred_element_type=F32)
    x = x_ref[...] + mix
    xo_ref[...] = x
    h = _rms(x, g_ref[...])
    h_ref[...] = h.astype(BF16)
    logits = jnp.dot(h, wr_ref[...], precision=HIGHEST, preferred_element_type=F32) + br_ref[...]
    lane = lax.broadcasted_iota(I32, logits.shape, 1)
    work = logits
    vals, hots = [], []
    for _ in range(TOP_K):
        m = jnp.max(work, axis=1, keepdims=True)
        idx = jnp.min(jnp.where(work == m, lane, N_EXPERTS), axis=1, keepdims=True)
        hot = lane == idx
        vals.append(m)
        hots.append(hot)
        work = jnp.where(hot, -jnp.inf, work)
    es = [jnp.exp(v - vals[0]) for v in vals]
    den = es[0] + es[1] + es[2] + es[3]
    gm = jnp.zeros(logits.shape, F32)
    for e, hot in zip(es, hots):
        gm = jnp.where(hot, e / den, gm)
    gm_ref[...] = gm


def _outproj(x, oa, ob, oc, wo, g, wr, br):
    nt = x.shape[0]
    tile = lambda n: pl.BlockSpec((TOK_TILE, n), lambda i: (i, 0))
    full = lambda a: pl.BlockSpec(a.shape, lambda i: (0, 0))
    return pl.pallas_call(
        _outproj_kernel,
        grid=(nt // TOK_TILE,),
        in_specs=[tile(D_MODEL), tile(A_WIDTH), tile(B_WIDTH), tile(C_WIDTH), full(wo), full(g), full(wr), full(br)],
        out_specs=[tile(D_MODEL), tile(D_MODEL), tile(N_EXPERTS)],
        out_shape=[jax.ShapeDtypeStruct((nt, D_MODEL), F32), jax.ShapeDtypeStruct((nt, D_MODEL), BF16),
                   jax.ShapeDtypeStruct((nt, N_EXPERTS), F32)],
        compiler_params=_cparams(("parallel",)),
        name="outproj",
    )(x, oa, ob, oc, wo, g, wr, br)


def _moe_kernel(h_ref, gm_ref, x_ref, w1_ref, b1_ref, w2_ref, b2_ref, gf_ref, o_ref, acc_scr, *, final):
    e = pl.program_id(1)

    @pl.when(e == 0)
    def _():
        acc_scr[...] = jnp.zeros_like(acc_scr)

    u = jnp.dot(h_ref[...], w1_ref[0, 0].astype(BF16), preferred_element_type=F32) + b1_ref[0, 0]
    glu = jnp.minimum(u[:, :D_FF], SWIGLU_LIMIT)
    lin = jnp.clip(u[:, D_FF:], -SWIGLU_LIMIT, SWIGLU_LIMIT)
    act = (glu * _sigmoid(SWIGLU_ALPHA * glu) * (lin + 1.0)).astype(BF16)
    out = jnp.dot(act, w2_ref[0, 0].astype(BF16), preferred_element_type=F32) + b2_ref[0, 0]
    gm = gm_ref[...]
    lane = lax.broadcasted_iota(I32, gm.shape, 1)
    gate = jnp.sum(jnp.where(lane == e, gm, 0.0), axis=1, keepdims=True)
    acc_scr[...] += gate * out

    @pl.when(e == N_EXPERTS - 1)
    def _():
        x = x_ref[...] + acc_scr[...]
        if final:
            x = _rms(x, gf_ref[...])
        o_ref[...] = x


def _moe(h, gm, x, w1, b1, w2, b2, gf, layer, final):
    nt = x.shape[0]
    tile = lambda n: pl.BlockSpec((TOK_TILE, n), lambda i, e: (i, 0))
    return pl.pallas_call(
        functools.partial(_moe_kernel, final=final),
        grid=(nt // TOK_TILE, N_EXPERTS),
        in_specs=[tile(D_MODEL), tile(N_EXPERTS), tile(D_MODEL),
                  pl.BlockSpec((1, 1, D_MODEL, 2 * D_FF), lambda i, e: (layer, e, 0, 0)),
                  pl.BlockSpec((1, 1, 1, 2 * D_FF), lambda i, e: (layer, e, 0, 0)),
                  pl.BlockSpec((1, 1, D_FF, D_MODEL), lambda i, e: (layer, e, 0, 0)),
                  pl.BlockSpec((1, 1, 1, D_MODEL), lambda i, e: (layer, e, 0, 0)),
                  pl.BlockSpec((1, D_MODEL), lambda i, e: (0, 0))],
        out_specs=tile(D_MODEL),
        out_shape=jax.ShapeDtypeStruct((nt, D_MODEL), F32),
        scratch_shapes=[pltpu.VMEM((TOK_TILE, D_MODEL), F32)],
        compiler_params=_cparams(("parallel", "arbitrary")),
        name="moe",
    )(h, gm, x, w1, b1[:, :, None, :], w2, b2[:, :, None, :], gf)


def _rel_bucket(dist):
    n = jnp.maximum(dist, 0)
    max_exact = N_BUCKETS // 2
    log_ratio = jnp.log(jnp.maximum(n, 1).astype(jnp.float32) / max_exact) / math.log(MAX_DISTANCE / max_exact)
    large = jnp.minimum(max_exact + (log_ratio * (N_BUCKETS - max_exact)).astype(jnp.int32), N_BUCKETS - 1)
    return jnp.where(n < max_exact, n, large)


def _bias_delta(rel_bias, dist):
    rb = rel_bias.astype(F32)
    d = rb[_rel_bucket(dist)] - rb[N_BUCKETS - 1]
    d = jnp.where(((dist >= 0) & (dist < MAX_DISTANCE))[..., None], d, 0.0)
    return jnp.moveaxis(d, -1, 0)


def _group_rows(delta, rows):
    return delta.reshape(B_KV_HEADS, B_REP * rows, delta.shape[-1])


def kernel(x_prompt, x_sample, cache_k, cache_v, cache_kidx, state_wkv, state_shift, state_pool, page_table,
           meta_tokens, rel_bias, norm_mix, norm_ffn, norm_final, w_in, w_out,
           a_mu, a_w0, a_w2, a_a0, a_a2, a_g2, a_kk, a_ka, a_rk, a_ln_w, a_ln_b,
           c_w, c_scale, moe_w_router, moe_b_router, moe_w1, moe_b1, moe_w2, moe_b2):
    assert x_prompt.shape[0] == 1
    depth = w_in.shape[0]
    seq = x_prompt.shape[1]
    nb, dec_seq = x_sample.shape[:2]
    assert dec_seq <= SROWS
    t_p = seq + N_META
    n_s = nb * dec_seq
    n_qblk = -(-t_p // Q_TILE)
    nkp = KEY_CHUNK * ((n_qblk + 4) // 4 + 1)
    nt = _round_up(max(t_p + n_s, Q_TILE * n_qblk), TOK_TILE)
    n_pages = page_table.shape[1]
    past = n_pages * PAGE_SIZE
    topk_p = min(TOPK_MAX, seq // 4)
    topk_s = min(TOPK_MAX, (past + dec_seq) // 4)
    n_phys = cache_k.shape[1]

    x = jnp.concatenate([meta_tokens.astype(F32), x_prompt[0], x_sample.reshape(n_s, D_MODEL),
                         jnp.zeros((nt - t_p - n_s, D_MODEL), F32)], axis=0)

    ones_blk = jnp.kron(jnp.eye(A_HEADS, dtype=F32), jnp.ones((HEAD_DIM, HEAD_DIM), F32))
    qi_ = jnp.arange(Q_TILE)[:, None]
    cj_ = jnp.arange(2 * Q_TILE)[None, :]
    side = TAIL_WIDTH - 2 * Q_TILE
    delta_p = jnp.pad(_bias_delta(rel_bias, Q_TILE + qi_ - cj_), ((0, 0), (0, 0), (side, side)))
    ts_ = jnp.arange(SROWS)[:, None]
    cs_ = jnp.arange(PAGE_SIZE)[None, :]
    delta_s = jnp.stack([_group_rows(_bias_delta(rel_bias, ts_ + PAGE_SIZE - cs_), SROWS),
                         _group_rows(_bias_delta(rel_bias, ts_ - cs_), SROWS)])

    outs = {n: [] for n in ("kp", "vp", "kip", "wkvp", "shp", "plp", "ks", "vs", "kis", "wkvs", "shs", "pls")}
    for l in range(depth):
        wl = w_in[l]
        b0 = A_PROJ
        w_cat = jnp.concatenate([
            wl[:, :A_PROJ], wl[:, b0:b0 + B_WIDTH + 2 * KV_WIDTH + IDX_HEADS * IDX_DIM],
            wl[:, b0 + B_PROJ - IDX_DIM - IDX_HEADS:b0 + B_PROJ],
            jnp.zeros((D_MODEL, LANES - IDX_DIM - IDX_HEADS), F32), wl[:, b0 + B_PROJ:]], axis=1).astype(BF16)
        pa, q, k, v, qi, kw, pc = _inproj(x, norm_mix[l][None], w_cat)

        sl_p = slice(0, t_p)
        sl_s = slice(t_p, t_p + n_s)
        outs["kp"].append(k[sl_p].reshape(1, t_p, B_KV_HEADS, HEAD_DIM))
        outs["vp"].append(v[sl_p].reshape(1, t_p, B_KV_HEADS, HEAD_DIM))
        outs["kip"].append(kw[sl_p, :IDX_DIM][None])
        outs["ks"].append(k[sl_s].reshape(nb, dec_seq, B_KV_HEADS, HEAD_DIM))
        outs["vs"].append(v[sl_s].reshape(nb, dec_seq, B_KV_HEADS, HEAD_DIM))
        outs["kis"].append(kw[sl_s, :IDX_DIM].reshape(nb, dec_seq, IDX_DIM))
        outs["shp"].append(pa[t_p - 1:t_p])
        pa_s = pa[sl_s].reshape(nb, dec_seq, A_PROJ)
        outs["shs"].append(pa_s[:, -1])
        pc_s = pc[sl_s].reshape(nb, dec_seq, C_WIDTH)
        outs["plp"].append(pc[t_p - POOL_BUF:t_p][None])
        pool_full = jnp.concatenate([state_pool[l], pc_s], axis=1)
        outs["pls"].append(pool_full[:, -POOL_BUF:])

        rw = (a_mu[l], a_w0[l], a_w2[l], a_a0[l], a_a2[l], a_g2[l], a_kk[l], a_ka[l], a_rk[l], a_ln_w[l], a_ln_b[l])
        oa_p, st_p = _rwkv_chunked(pa[None], jnp.zeros((1, 1, A_PROJ), F32),
                                   jnp.zeros((1, A_HEADS, HEAD_DIM, HEAD_DIM), F32),
                                   rw, ones_blk, t_p, RWKV_BLOCK, RWKV_CHUNK)
        pa_s8 = jnp.pad(pa_s, ((0, 0), (0, SROWS - dec_seq), (0, 0)))
        oa_s, st_s = _rwkv_chunked(pa_s8, state_shift[l][:, None, :], state_wkv[l].astype(F32),
                                   rw, ones_blk, dec_seq, SROWS, SROWS)
        outs["wkvp"].append(st_p.astype(state_wkv.dtype))
        outs["wkvs"].append(st_s.astype(state_wkv.dtype))

        n_real = min(nt, nkp - KEY_FRONT_PAD)
        front = lambda a: jnp.pad(a[:n_real].astype(BF16), ((KEY_FRONT_PAD, nkp - KEY_FRONT_PAD - n_real), (0, 0)))
        ob_p = _dsa_prompt(q, qi, kw, front(k), front(v), front(kw[:, :IDX_DIM]), delta_p, n_qblk, topk_p)
        rows8 = lambda a: jnp.pad(a[sl_s].reshape(nb, dec_seq, -1), ((0, 0), (0, SROWS - dec_seq), (0, 0)))
        page8 = lambda a: jnp.pad(a[sl_s].reshape(nb, dec_seq, -1), ((0, 0), (0, PAGE_SIZE - dec_seq), (0, 0)))
        kv_page = lambda a: page8(a).reshape(nb, PAGE_SIZE, B_KV_HEADS, HEAD_DIM).transpose(0, 2, 3, 1)
        ob_s = _dsa_sample(page_table, rows8(q), rows8(qi), rows8(kw),
                           cache_kidx.transpose(0, 1, 3, 2), cache_k.transpose(0, 1, 3, 4, 2),
                           cache_v.transpose(0, 1, 3, 4, 2), l, page8(kw[:, :IDX_DIM]).transpose(0, 2, 1),
                           kv_page(k), kv_page(v), delta_s, dec_seq, topk_s)

        w_bd = jax.scipy.linalg.block_diag(*[c_w[l][gi] for gi in range(len(POOL_WINDOWS))]).astype(BF16)
        oc_p = _pool(pc[None], w_bd, c_scale[l][None], TOK_TILE, 0)
        pool_in = jnp.concatenate([jnp.zeros((nb, 1, C_WIDTH), F32), pool_full,
                                   jnp.zeros((nb, 2 * POOL_HALO - 1 - POOL_BUF - dec_seq, C_WIDTH), F32)], axis=1)
        oc_s = _pool(pool_in, w_bd, c_scale[l][None], POOL_HALO, 1)[:, POOL_HALO:POOL_HALO + dec_seq]

        def merge(p_rows, s_rows):
            w = p_rows.shape[-1]
            return jnp.concatenate([p_rows[:t_p], s_rows.reshape(n_s, w), jnp.zeros((nt - t_p - n_s, w), F32)], axis=0)

        oa = merge(oa_p[0], oa_s[:, :dec_seq])
        ob = merge(ob_p, ob_s[:, :dec_seq])
        oc = merge(oc_p[0], oc_s)
        x, h2, gm = _outproj(x, oa, ob, oc, w_out[l].astype(BF16), norm_ffn[l][None], moe_w_router[l],
                             moe_b_router[l][None])
        x = _moe(h2, gm, x, moe_w1, moe_b1, moe_w2, moe_b2, norm_final[None], l, l == depth - 1)

    y_prompt = x[N_META:t_p][None]
    y_sample = x[t_p:t_p + n_s].reshape(nb, dec_seq, D_MODEL)
    st = lambda n: jnp.stack(outs[n])
    return (y_prompt, y_sample, st("kp"), st("vp"), st("kip"), st("wkvp"), st("shp"), st("plp"),
            st("ks"), st("vs"), st("kis"), st("wkvs"), st("shs"), st("pls"))
```

```python
import functools
import math

import jax
import jax.numpy as jnp
import numpy as np
from jax import lax
from jax.experimental import pallas as pl
from jax.experimental.pallas import tpu as pltpu

F32 = jnp.float32
BF16 = jnp.bfloat16
I32 = jnp.int32
HIGHEST = lax.Precision.HIGHEST

D_MODEL = 1024
N_META = 16
HEAD_DIM = 64
A_WIDTH = 384
A_HEADS = 6
LORA_W, LORA_A, LORA_G = 64, 64, 128
A_PROJ = 3 * A_WIDTH + LORA_W + LORA_A + LORA_G
GN_EPS = 64e-5
B_WIDTH = 384
B_HEADS = 6
B_KV_HEADS = 2
B_REP = 3
KV_WIDTH = 128
IDX_HEADS = 4
IDX_DIM = 64
IDX_SCALE = (IDX_HEADS * IDX_DIM) ** -0.5
TOPK_MAX = 256
B_PROJ = B_WIDTH + 2 * KV_WIDTH + IDX_HEADS * IDX_DIM + IDX_DIM + IDX_HEADS
C_WIDTH = 256
C_GROUP_DIM = 64
POOL_WINDOWS = (2, 4, 8, 16)
POOL_BUF = 15
N_BUCKETS = 32
MAX_DISTANCE = 128
N_EXPERTS = 32
TOP_K = 4
D_FF = 1024
SWIGLU_LIMIT = 7.0
SWIGLU_ALPHA = 1.702
RMS_EPS = 1e-5
PAGE_SIZE = 128

LANES = 128
SUBLANES = 8
Q_TILE = 128
KEY_CHUNK = 512
KEY_FRONT_PAD = 128
TAIL_WIDTH = 640
TOP_M = 12
SAMPLE_PAGES = 16
TOK_TILE = 512
RWKV_BLOCK = 128
RWKV_CHUNK = 64
POOL_HALO = 16
VMEM_LIMIT = 56 * 1024 * 1024

INT_MIN = -(2 ** 31)
NEG_INF_KEY = np.int32(np.uint32(0xFF800000) ^ np.uint32(0x7FFFFFFF))
NEG_BIG = -1e30


def _round_up(x, m):
    return (x + m - 1) // m * m


def _cparams(sem):
    return pltpu.CompilerParams(dimension_semantics=sem, vmem_limit_bytes=VMEM_LIMIT)


def _rms(x, g):
    return x * lax.rsqrt(jnp.mean(x * x, axis=-1, keepdims=True) + RMS_EPS) * g


def _sigmoid(x):
    return 1.0 / (1.0 + jnp.exp(-x))


IN_COLS = (A_PROJ, B_WIDTH, KV_WIDTH, KV_WIDTH, IDX_HEADS * IDX_DIM, LANES, C_WIDTH)


def _inproj_kernel(x_ref, g_ref, w_ref, *out_refs):
    hb = _rms(x_ref[...], g_ref[...]).astype(BF16)
    off = 0
    for ref in out_refs:
        n = ref.shape[1]
        ref[...] = jnp.dot(hb, w_ref[:, off:off + n], preferred_element_type=F32)
        off += n


def _inproj(x, g, w):
    nt = x.shape[0]
    return pl.pallas_call(
        _inproj_kernel,
        grid=(nt // TOK_TILE,),
        in_specs=[pl.BlockSpec((TOK_TILE, D_MODEL), lambda i: (i, 0)),
                  pl.BlockSpec((1, D_MODEL), lambda i: (0, 0)),
                  pl.BlockSpec(w.shape, lambda i: (0, 0))],
        out_specs=[pl.BlockSpec((TOK_TILE, n), lambda i: (i, 0)) for n in IN_COLS],
        out_shape=[jax.ShapeDtypeStruct((nt, n), F32) for n in IN_COLS],
        compiler_params=_cparams(("parallel",)),
        name="inproj",
    )(x, g, w)


NN_DIMS = (((1,), (0,)), ((), ()))
NT_DIMS = (((1,), (1,)), ((), ()))
TN_DIMS = (((0,), (0,)), ((), ()))


def _dot3(a, b, dims=NN_DIMS):
    a_hi = a.astype(BF16)
    b_hi = b.astype(BF16)
    a_lo = (a - a_hi.astype(F32)).astype(BF16)
    b_lo = (b - b_hi.astype(F32)).astype(BF16)
    dg = functools.partial(lax.dot_general, dimension_numbers=dims, preferred_element_type=F32)
    return dg(a_hi, b_hi) + (dg(a_hi, b_lo) + dg(a_lo, b_hi))


def _rwkv_chunk_kernel(pa_ref, sh_ref, s0_ref, mu_ref, w0_ref, w2_ref, a0_ref, a2_ref, g2_ref, kk_ref, ka_ref,
                       rk_ref, lnw_ref, lnb_ref, ones_ref, tri_ref, o_ref, s_out_ref, carry_scr, s_scr,
                       *, seq_len, block, chunk):
    blk = pl.program_id(1)

    @pl.when(blk == 0)
    def _():
        carry_scr[...] = sh_ref[0]
        s_scr[...] = s0_ref[0]

    pa = pa_ref[0]
    row = lax.broadcasted_iota(I32, (block, 1), 0)
    prev = jnp.where(row == 0, carry_scr[...], pltpu.roll(pa, 1, axis=0))
    carry_scr[...] = pa[block - 1:block, :]
    xm = pa + (prev - pa) * mu_ref[...]
    r = xm[:, 0:A_WIDTH]
    k = xm[:, A_WIDTH:2 * A_WIDTH]
    v = xm[:, 2 * A_WIDTH:3 * A_WIDTH]
    lw = xm[:, 3 * A_WIDTH:3 * A_WIDTH + LORA_W]
    la = xm[:, 3 * A_WIDTH + LORA_W:3 * A_WIDTH + LORA_W + LORA_A]
    lg = xm[:, 3 * A_WIDTH + LORA_W + LORA_A:]

    def dot_hi(a, b):
        return jnp.dot(a, b, precision=HIGHEST, preferred_element_type=F32)

    z = -(w0_ref[...] + dot_hi(jnp.tanh(lw), w2_ref[...]))
    softplus = jnp.maximum(z, 0.0) + jnp.log(1.0 + jnp.exp(-jnp.abs(z)))
    log_w = -jnp.exp(-softplus - 0.5)
    a = _sigmoid(a0_ref[...] + dot_hi(la, a2_ref[...]))
    g = dot_hi(_sigmoid(lg), g2_ref[...])
    ones_blk = ones_ref[...]
    kk = k * kk_ref[...]
    kk = kk / jnp.maximum(jnp.sqrt(dot_hi(kk * kk, ones_blk)), 1e-12)
    k2 = k * (1.0 + (a - 1.0) * ka_ref[...])
    kka = kk * a

    valid = (blk * block + row) < seq_len
    log_w = jnp.where(valid, log_w, 0.0)
    kk_m, kka_m, k2_m, r_m, v_m = (jnp.where(valid, t, 0.0) for t in (kk, kka, k2, r, v))

    cum = dot_hi(tri_ref[0], log_w)
    cum_end = dot_hi(tri_ref[1], log_w)
    kap = kk_m * jnp.exp(cum - log_w)
    rt = r_m * jnp.exp(cum)
    inv_p = jnp.exp(-cum)
    kt = k2_m * inv_p
    bt = kka_m * inv_p
    to_end = jnp.exp(cum_end - cum)
    k_end = k2_m * to_end
    b_end = kka_m * to_end
    p_end = jnp.exp(cum_end)

    sub = min(16, chunk)
    ti = lax.broadcasted_iota(I32, (chunk, chunk), 0)
    tj = lax.broadcasted_iota(I32, (chunk, chunk), 1)
    heads = range(A_HEADS)
    o_chunks = []
    for c in range(block // chunk):
        rs = slice(c * chunk, (c + 1) * chunk)
        hsl = lambda t, h: t[rs, h * HEAD_DIM:(h + 1) * HEAD_DIM]
        lhs = [jnp.concatenate([hsl(kap, h), hsl(rt, h)], axis=0) for h in heads]
        rhs = [jnp.concatenate([hsl(kt, h), hsl(bt, h)], axis=0) for h in heads]
        vh = [hsl(v_m, h) for h in heads]
        gram = [_dot3(lhs[h], rhs[h], NT_DIMS) for h in heads]
        from_s = [_dot3(lhs[h], s_scr[h], NT_DIMS) for h in heads]
        a1 = [jnp.where(ti > tj, gram[h][:chunk, :chunk], 0.0) for h in heads]
        a2 = [jnp.where(ti > tj, gram[h][:chunk, chunk:], 0.0) for h in heads]
        a3 = [jnp.where(ti >= tj, gram[h][chunk:, :chunk], 0.0) for h in heads]
        a4 = [jnp.where(ti >= tj, gram[h][chunk:, chunk:], 0.0) for h in heads]
        u = [from_s[h][:chunk] + _dot3(a1[h], vh[h]) for h in heads]
        zs = [[] for _ in heads]
        for jb in range(chunk // sub):
            js = slice(jb * sub, (jb + 1) * sub)
            zj = []
            for h in heads:
                t = u[h][js]
                if jb > 0:
                    t = t - _dot3(a2[h][js, :jb * sub], jnp.concatenate(zs[h], axis=0))
                zj.append(t)
            for i in range(sub - 1):
                for h in heads:
                    col = jnp.broadcast_to(a2[h][js, jb * sub + i:jb * sub + i + 1], (sub, HEAD_DIM))
                    zj[h] = zj[h] - col * zj[h][i:i + 1, :]
            for h in heads:
                zs[h].append(zj[h])
        zh = [jnp.concatenate(zs[h], axis=0) for h in heads]
        vz = [jnp.concatenate([vh[h], zh[h]], axis=0) for h in heads]
        o_chunks.append(jnp.concatenate(
            [from_s[h][chunk:] + _dot3(jnp.concatenate([a3[h], -a4[h]], axis=1), vz[h]) for h in heads], axis=1))
        for h in heads:
            ends = jnp.concatenate([hsl(k_end, h), -hsl(b_end, h)], axis=0)
            s_scr[h] = s_scr[h] * hsl(p_end, h)[0:1, :] + _dot3(vz[h], ends, TN_DIMS)
    s_out_ref[0] = s_scr[...]

    o = jnp.concatenate(o_chunks, axis=0) if len(o_chunks) > 1 else o_chunks[0]
    inv = 1.0 / HEAD_DIM
    mean = dot_hi(o, ones_blk) * inv
    cen = o - mean
    var = dot_hi(cen * cen, ones_blk) * inv
    o = cen * lax.rsqrt(var + GN_EPS) * lnw_ref[...] + lnb_ref[...]
    bonus = dot_hi(r * k2 * rk_ref[...], ones_blk) * v
    o_ref[0] = (o + bonus) * g


def _rwkv_chunked(pa, shift_prev, s0, params, ones_blk, seq_len, block, chunk):
    n_seq, length, _ = pa.shape
    n_blk = -(-seq_len // block)
    idx = np.arange(block)
    same = (idx[:, None] // chunk) == (idx[None, :] // chunk)
    tri = jnp.asarray(np.stack([same & (idx[None, :] <= idx[:, None]), same]).astype(np.float32))
    row_spec = lambda n: pl.BlockSpec((1, n), lambda s, i: (0, 0))
    mat_spec = lambda a: pl.BlockSpec(a.shape, lambda s, i: (0,) * a.ndim)
    mu, w0, w2, a0, a2, g2, k_k, k_a, r_k, ln_w, ln_b = params
    kern = functools.partial(_rwkv_chunk_kernel, seq_len=seq_len, block=block, chunk=chunk)
    state_spec = pl.BlockSpec((1, A_HEADS, HEAD_DIM, HEAD_DIM), lambda s, i: (s, 0, 0, 0))
    return pl.pallas_call(
        kern,
        grid=(n_seq, n_blk),
        in_specs=[pl.BlockSpec((1, block, A_PROJ), lambda s, i: (s, i, 0)),
                  pl.BlockSpec((1, 1, A_PROJ), lambda s, i: (s, 0, 0)),
                  state_spec,
                  row_spec(A_PROJ), row_spec(A_WIDTH), mat_spec(w2), row_spec(A_WIDTH), mat_spec(a2),
                  mat_spec(g2), row_spec(A_WIDTH), row_spec(A_WIDTH), row_spec(A_WIDTH), row_spec(A_WIDTH),
                  row_spec(A_WIDTH), mat_spec(ones_blk), mat_spec(tri)],
        out_specs=[pl.BlockSpec((1, block, A_WIDTH), lambda s, i: (s, i, 0)), state_spec],
        out_shape=[jax.ShapeDtypeStruct((n_seq, length, A_WIDTH), F32),
                   jax.ShapeDtypeStruct((n_seq, A_HEADS, HEAD_DIM, HEAD_DIM), F32)],
        scratch_shapes=[pltpu.VMEM((1, A_PROJ), F32), pltpu.VMEM((A_HEADS, HEAD_DIM, HEAD_DIM), F32)],
        compiler_params=_cparams(("arbitrary", "arbitrary")),
        name="rwkv",
    )(pa, shift_prev, s0, mu[None], w0[None], w2, a0[None], a2, g2, k_k[None], k_a[None], r_k[None],
      ln_w[None], ln_b[None], ones_blk, tri)


def _sortable_key(s):
    bits = pltpu.bitcast(s + 0.0, I32)
    return bits ^ ((bits >> 31) & jnp.int32(0x7FFFFFFF))


def _index_scores(qi_b, wi, ki_b, transposed=False):
    s = None
    for h in range(IDX_HEADS):
        d = lax.dot_general(qi_b[:, h * IDX_DIM:(h + 1) * IDX_DIM], ki_b,
                            (((1,), (0 if transposed else 1,)), ((), ())), preferred_element_type=F32)
        t = jnp.maximum(d, 0.0) * wi[:, h:h + 1]
        s = t if s is None else s + t
    return s


def _topk_threshold(count_fn, count_tie_fn, rows, topk):
    kf = jnp.float32(topk)
    c0 = count_fn(jnp.zeros((rows, 1), I32), False)
    prefix0 = jnp.where(c0 >= kf, jnp.int32(0), jnp.int32(INT_MIN))

    def bit_step(i, prefix):
        cand = prefix | lax.shift_left(jnp.int32(1), jnp.int32(30) - i)
        return jnp.where(count_fn(cand, False) >= kf, cand, prefix)

    tau = lax.fori_loop(0, 31, bit_step, prefix0)
    c_gt = count_fn(tau, True)
    c_ge = count_fn(tau, False)
    need = kf - c_gt
    excess = jnp.max(c_ge - c_gt - need)

    def tie_search(_):
        def jbit(i, jl):
            cand = jl | lax.shift_left(jnp.int32(1), jnp.int32(14) - i)
            return jnp.where(count_tie_fn(tau, cand) < need, cand, jl)
        return lax.fori_loop(0, 15, jbit, jnp.zeros((rows, 1), I32))

    jlim = lax.cond(excess > 0.0, tie_search, lambda _: jnp.full((rows, 1), 2 ** 30, I32), 0)
    return tau, jlim


def _select(keys, aidx, tau, jlim):
    return ((keys > tau) | ((keys == tau) & (aidx <= jlim))) & (keys > jnp.int32(NEG_INF_KEY))


def _softmax_step(state, qg, kt, vt, sel_rows, delta):
    m, l, acc = state
    lg = lax.dot_general(qg, kt, (((1,), (1,)), ((), ())), preferred_element_type=F32)
    if delta is not None:
        lg = lg + delta
    lg = jnp.where(sel_rows, lg, NEG_BIG)
    m_new = jnp.maximum(m, jnp.max(lg, axis=1, keepdims=True))
    alpha = jnp.exp(m - m_new)
    p = jnp.where(sel_rows, jnp.exp(lg - m_new), 0.0)
    l = alpha * l + jnp.sum(p, axis=1, keepdims=True)
    acc = alpha * acc + jnp.dot(p.astype(BF16), vt, preferred_element_type=F32)
    return m_new, l, acc


def _dsa_prompt_kernel(q_ref, qi_ref, kw_ref, kp_ref, vp_ref, kip_ref, dl_ref, o_ref, key_scr, cand_scr, *, topk):
    j = pl.program_id(0)
    rows = Q_TILE
    n_chunks = (j + 5) // 4
    qi_b = qi_ref[...].astype(BF16)
    wi = kw_ref[:, IDX_DIM:IDX_DIM + IDX_HEADS] * IDX_SCALE
    qpos = j * Q_TILE + lax.broadcasted_iota(I32, (rows, 1), 0)
    lane_c = lax.broadcasted_iota(I32, (1, KEY_CHUNK), 1)

    def score_chunk(c, masked):
        a0 = pl.multiple_of(c * KEY_CHUNK, KEY_CHUNK)
        s = _index_scores(qi_b, wi, kip_ref[pl.ds(a0, KEY_CHUNK), :])
        if masked:
            kpos = a0 - KEY_FRONT_PAD + lane_c
            s = jnp.where((kpos <= qpos) & (kpos >= 0), s, -jnp.inf)
        key_scr[:, pl.ds(a0, KEY_CHUNK)] = _sortable_key(s)

    n_plain = jnp.clip((j * Q_TILE + KEY_FRONT_PAD - KEY_CHUNK) // KEY_CHUNK, 0, n_chunks - 1)
    score_chunk(0, True)
    lax.fori_loop(1, 1 + n_plain, lambda c, x: (score_chunk(c, False), x)[1], 0)
    lax.fori_loop(1 + n_plain, n_chunks, lambda c, x: (score_chunk(c, True), x)[1], 0)

    key_scr[:, pl.ds(pl.multiple_of(n_chunks * KEY_CHUNK, KEY_CHUNK), KEY_CHUNK)] = jnp.full(
        (rows, KEY_CHUNK), NEG_INF_KEY, I32)

    n_fold = KEY_CHUNK // LANES

    def chunk_at(ref, c):
        return ref[:, pl.ds(pl.multiple_of(c * KEY_CHUNK, KEY_CHUNK), KEY_CHUNK)]

    def count32(cand, strict):
        def body(c, acc):
            kk = chunk_at(key_scr, c)
            hit = (kk > cand) if strict else (kk >= cand)
            m = jnp.where(hit, 1.0, 0.0)
            for i in range(n_fold):
                acc = acc + m[:, i * LANES:(i + 1) * LANES]
            return acc
        acc = lax.fori_loop(0, n_chunks, body, jnp.zeros((rows, LANES), F32))
        return jnp.sum(acc, axis=1, keepdims=True)

    kf = jnp.float32(topk)

    def kth_largest(count_ge):
        def bit_step(i, prefix):
            cand = prefix | lax.shift_left(jnp.int32(1), jnp.int32(30) - i)
            return jnp.where(count_ge(cand) >= kf, cand, prefix)
        nonneg = count_ge(jnp.zeros((rows, 1), I32)) >= kf
        return lax.fori_loop(0, 31, bit_step, jnp.where(nonneg, jnp.int32(0), jnp.int32(INT_MIN)))

    pair = 2 * SUBLANES
    for rp in range(rows // pair):
        r0 = rp * pair

        def insert_chunk(c, lists, r0=r0):
            a0 = pl.multiple_of(c * KEY_CHUNK, KEY_CHUNK)
            blk = key_scr[r0:r0 + pair, pl.ds(a0, KEY_CHUNK)]
            blk = pltpu.bitcast(blk ^ ((blk >> 31) & jnp.int32(0x7FFFFFFF)), F32)
            lists = list(lists)
            for half in range(2):
                for t in range(n_fold):
                    x = blk[half * SUBLANES:(half + 1) * SUBLANES, t * LANES:(t + 1) * LANES]
                    for i in range(TOP_M):
                        a = lists[half * TOP_M + i]
                        lists[half * TOP_M + i] = jnp.maximum(a, x)
                        x = jnp.minimum(a, x)
            return tuple(lists)

        lists = lax.fori_loop(0, n_chunks, insert_chunk,
                              tuple(jnp.full((SUBLANES, LANES), -jnp.inf, F32) for _ in range(2 * TOP_M)))
        for half in range(2):
            for i in range(TOP_M):
                cand_scr[r0 + half * SUBLANES:r0 + (half + 1) * SUBLANES, i * LANES:(i + 1) * LANES] = _sortable_key(
                    lists[half * TOP_M + i])

    def count_cand(cand):
        acc = jnp.zeros((rows, LANES), F32)
        for i in range(TOP_M):
            acc = acc + jnp.where(cand_scr[:, i * LANES:(i + 1) * LANES] >= cand, 1.0, 0.0)
        return jnp.sum(acc, axis=1, keepdims=True)

    tau_c = kth_largest(count_cand)
    gt_c = count32(tau_c, True)
    ge_c = count32(tau_c, False)
    wrong = jnp.max(jnp.where((gt_c < kf) & (ge_c >= kf), 0.0, 1.0))

    def full_search(_):
        t = kth_largest(lambda cand: count32(cand, False))
        return t, count32(t, True), count32(t, False)

    tau, c_gt, c_ge = lax.cond(wrong > 0.0, full_search, lambda _: (tau_c, gt_c, ge_c), 0)
    need = kf - c_gt
    excess = jnp.max(c_ge - c_gt - need)

    def tie_pass(_):
        ri = lax.broadcasted_iota(I32, (LANES, 2 * LANES), 0)
        ci = lax.broadcasted_iota(I32, (LANES, 2 * LANES), 1)
        tri = jnp.where((ri <= ci) | (ci >= LANES), 1.0, 0.0).astype(BF16)

        def body(c, carry):
            seen, jacc = carry
            a0 = pl.multiple_of(c * KEY_CHUNK, KEY_CHUNK)
            kk = key_scr[:, pl.ds(a0, KEY_CHUNK)]
            ties = [kk[:, i * LANES:(i + 1) * LANES] == tau for i in range(n_fold)]
            pres = [jnp.dot(jnp.where(t, 1.0, 0.0).astype(BF16), tri, preferred_element_type=F32) for t in ties]
            for i in range(n_fold):
                rank = jnp.where(ties[i], seen + pres[i][:, :LANES], 1e9)
                aidx = a0 + i * LANES + lax.broadcasted_iota(I32, (1, LANES), 1)
                jacc = jnp.maximum(jacc, jnp.where(rank <= need, aidx, -1))
                seen = seen + pres[i][:, LANES:]
            return seen, jacc
        _, jacc = lax.fori_loop(0, n_chunks, body, (jnp.zeros((rows, LANES), F32), jnp.full((rows, LANES), -1, I32)))
        return jnp.max(jacc, axis=1, keepdims=True)

    jlim = lax.cond(excess > 0.0, tie_pass, lambda _: jnp.full((rows, 1), 2 ** 30, I32), 0)
    jlim = jnp.where(tau == jnp.int32(NEG_INF_KEY), jnp.int32(-1), jlim)
    tau_b = jnp.broadcast_to(tau, (rows, LANES))
    tau_m1_b = tau_b - 1

    q = q_ref[...] * (HEAD_DIM ** -0.5)
    qh = [q[:, h * HEAD_DIM:(h + 1) * HEAD_DIM].astype(BF16) for h in range(B_HEADS)]

    def init_state():
        return (jnp.full((rows, 1), NEG_BIG, F32), jnp.zeros((rows, 1), F32), jnp.zeros((rows, HEAD_DIM), F32))

    def attend(a0, width, states, delta):
        kk = key_scr[:, pl.ds(a0, width)]
        aidx = a0 + lax.broadcasted_iota(I32, (1, width), 1)
        thr = jnp.where(aidx <= jlim, jnp.concatenate([tau_m1_b] * (width // LANES), axis=1),
                        jnp.concatenate([tau_b] * (width // LANES), axis=1))
        madd = jnp.where(kk > thr, 0.0, NEG_BIG)
        kts = [kp_ref[pl.ds(a0, width), g * HEAD_DIM:(g + 1) * HEAD_DIM] for g in range(B_KV_HEADS)]
        vts = [vp_ref[pl.ds(a0, width), g * HEAD_DIM:(g + 1) * HEAD_DIM] for g in range(B_KV_HEADS)]
        lgs = [lax.dot_general(qh[h], kts[h // B_REP], (((1,), (1,)), ((), ())), preferred_element_type=F32)
               for h in range(B_HEADS)]
        mid = []
        for h in range(B_HEADS):
            m, l, _ = states[h]
            lg = lgs[h] + madd
            if delta is not None:
                lg = lg + delta[h]
            m_new = jnp.maximum(m, jnp.max(lg, axis=1, keepdims=True))
            alpha = jnp.exp(m - m_new)
            p = jnp.exp(lg - m_new)
            mid.append((m_new, alpha, alpha * l + jnp.sum(p, axis=1, keepdims=True), p.astype(BF16)))
        out = []
        for h in range(B_HEADS):
            m_new, alpha, l, pb = mid[h]
            acc = alpha * states[h][2] + jnp.dot(pb, vts[h // B_REP], preferred_element_type=F32)
            out.append((m_new, l, acc))
        return tuple(out)

    def far_chunk(c, states):
        return attend(pl.multiple_of(c * KEY_CHUNK, KEY_CHUNK), KEY_CHUNK, states, None)

    n_far = j // 4
    states = lax.fori_loop(0, n_far, far_chunk, tuple(init_state() for _ in range(B_HEADS)))
    d0 = pl.multiple_of((3 - j % 4) * Q_TILE, Q_TILE)
    delta = [dl_ref[h, :, pl.ds(d0, TAIL_WIDTH)] for h in range(B_HEADS)]
    states = attend(pl.multiple_of(n_far * KEY_CHUNK, KEY_CHUNK), TAIL_WIDTH, states, delta)

    o_ref[...] = jnp.concatenate([acc / l for _, l, acc in states], axis=1)


def _dsa_prompt(q, qi, kw, kp, vp, kip, delta, n_qblk, topk):
    nkp = kp.shape[0]
    full = lambda a: pl.BlockSpec(a.shape, lambda j: (0,) * a.ndim)
    return pl.pallas_call(
        functools.partial(_dsa_prompt_kernel, topk=topk),
        grid=(n_qblk,),
        in_specs=[pl.BlockSpec((Q_TILE, B_WIDTH), lambda j: (j, 0)),
                  pl.BlockSpec((Q_TILE, IDX_HEADS * IDX_DIM), lambda j: (j, 0)),
                  pl.BlockSpec((Q_TILE, LANES), lambda j: (j, 0)),
                  full(kp), full(vp), full(kip), full(delta)],
        out_specs=pl.BlockSpec((Q_TILE, B_WIDTH), lambda j: (j, 0)),
        out_shape=jax.ShapeDtypeStruct((n_qblk * Q_TILE, B_WIDTH), F32),
        scratch_shapes=[pltpu.VMEM((Q_TILE, nkp), I32), pltpu.VMEM((Q_TILE, TOP_M * LANES), I32)],
        compiler_params=_cparams(("arbitrary",)),
        name="dsa_prompt",
    )(q, qi, kw, kp, vp, kip, delta)


SROWS = 8


def _dsa_sample_score_kernel(pt_ref, qi_ref, kw_ref, *refs, n_steps, pages, dec_seq):
    cki_refs, kin_ref, key_ref = refs[:pages], refs[pages], refs[pages + 1]
    p = pl.program_id(1)
    qi_b = qi_ref[0].astype(BF16)
    wi = kw_ref[0][:, IDX_DIM:IDX_DIM + IDX_HEADS] * IDX_SCALE
    t = lax.broadcasted_iota(I32, (SROWS, 1), 0)
    c = lax.broadcasted_iota(I32, (1, PAGE_SIZE), 1)

    @pl.when(p < n_steps)
    def _():
        ki_all = jnp.concatenate([r[0, 0].astype(BF16) for r in cki_refs], axis=1)
        s = _index_scores(qi_b, wi, ki_all, transposed=True)
        key_ref[0] = _sortable_key(jnp.where(t < dec_seq, s, -jnp.inf))

    @pl.when(p == n_steps)
    def _():
        s = _index_scores(qi_b, wi, kin_ref[0].astype(BF16), transposed=True)
        key_ref[0, :, 0:PAGE_SIZE] = _sortable_key(jnp.where((t < dec_seq) & (c <= t), s, -jnp.inf))
        if pages > 1:
            key_ref[0, :, PAGE_SIZE:] = jnp.full((SROWS, (pages - 1) * PAGE_SIZE), NEG_INF_KEY, I32)


def _dsa_sample_attend_kernel(pt_ref, q_ref, key_ref, *refs, n_steps, pages, topk):
    ck_refs, cv_refs = refs[:pages], refs[pages:2 * pages]
    kn_ref, vn_ref, dl_ref, o_ref, tau_scr, jl_scr, m_scr, l_scr, acc_scr = refs[2 * pages:]
    p = pl.program_id(1)
    width = pages * PAGE_SIZE

    @pl.when(p == 0)
    def _():
        keys = key_ref[0]
        aidx = lax.broadcasted_iota(I32, (1, keys.shape[1]), 1)

        def count_fn(cand, strict):
            hit = (keys > cand) if strict else (keys >= cand)
            return jnp.sum(jnp.where(hit, 1.0, 0.0), axis=1, keepdims=True)

        def count_tie_fn(tau, jl):
            return jnp.sum(jnp.where((keys == tau) & (aidx < jl), 1.0, 0.0), axis=1, keepdims=True)

        tau, jlim = _topk_threshold(count_fn, count_tie_fn, SROWS, topk)
        tau_scr[...] = tau
        jl_scr[...] = jlim
        m_scr[...] = jnp.full(m_scr.shape, NEG_BIG, F32)
        l_scr[...] = jnp.zeros(l_scr.shape, F32)
        acc_scr[...] = jnp.zeros(acc_scr.shape, F32)

    a0 = pl.multiple_of(p * width, width)
    kk = key_ref[0, :, pl.ds(a0, width)]
    aidx = a0 + lax.broadcasted_iota(I32, (1, width), 1)
    sel = _select(kk, aidx, tau_scr[...], jl_scr[...])
    sel3 = jnp.concatenate([sel] * B_REP, axis=0)
    q = q_ref[0] * (HEAD_DIM ** -0.5)

    def cached(ref):
        return lambda g: ref[0, 0, g]

    def fresh(ref):
        return lambda g: ref[0, g]

    def run(k_tiles, v_tiles, delta):
        n = len(k_tiles)
        sel_n = sel3[:, :n * PAGE_SIZE]
        for g in range(B_KV_HEADS):
            qg = jnp.concatenate([q[:, (g * B_REP + r) * HEAD_DIM:(g * B_REP + r + 1) * HEAD_DIM]
                                  for r in range(B_REP)], axis=0).astype(BF16)
            k_all = jnp.concatenate([kt(g).astype(BF16) for kt in k_tiles], axis=1)
            lg = jnp.dot(qg, k_all, preferred_element_type=F32)
            if delta is not None:
                pad = [jnp.zeros((lg.shape[0], (n - 1) * PAGE_SIZE), F32)] if n > 1 else []
                lg = lg + jnp.concatenate(pad + [delta[g]], axis=1)
            lg = jnp.where(sel_n, lg, NEG_BIG)
            m = m_scr[g]
            m_new = jnp.maximum(m, jnp.max(lg, axis=1, keepdims=True))
            alpha = jnp.exp(m - m_new)
            pr = jnp.where(sel_n, jnp.exp(lg - m_new), 0.0).astype(BF16)
            v_all = jnp.concatenate([vt(g).astype(BF16) for vt in v_tiles], axis=1)
            acc = alpha * acc_scr[g] + lax.dot_general(pr, v_all, NT_DIMS, preferred_element_type=F32)
            m_scr[g] = m_new
            l_scr[g] = alpha * l_scr[g] + jnp.sum(pr.astype(F32), axis=1, keepdims=True)
            acc_scr[g] = acc

    ck_tiles = [cached(r) for r in ck_refs]
    cv_tiles = [cached(r) for r in cv_refs]

    @pl.when(p < n_steps - 1)
    def _():
        run(ck_tiles, cv_tiles, None)

    @pl.when(p == n_steps - 1)
    def _():
        run(ck_tiles, cv_tiles, (dl_ref[0, 0], dl_ref[0, 1]))

    @pl.when(p == n_steps)
    def _():
        run([fresh(kn_ref)], [fresh(vn_ref)], (dl_ref[1, 0], dl_ref[1, 1]))
        pieces = [None] * B_HEADS
        for g in range(B_KV_HEADS):
            og = acc_scr[g] / l_scr[g]
            for r in range(B_REP):
                pieces[g * B_REP + r] = og[r * SROWS:(r + 1) * SROWS, :]
        o_ref[0] = jnp.concatenate(pieces, axis=1)


def _dsa_sample(page_table, q_s, qi_s, kw_s, cki, ck, cv, layer, ki_new, k_new, v_new, delta_s, dec_seq, topk):
    nb = q_s.shape[0]
    n_pages = page_table.shape[1]
    pages = math.gcd(n_pages, SAMPLE_PAGES)
    n_steps = n_pages // pages
    width = pages * PAGE_SIZE
    n_keys = (n_steps + 1) * width

    def page(i):
        return lambda b, p, pt: (layer, pt[b, jnp.minimum(p, n_steps - 1) * pages + i], 0, 0)

    def page_kv(i):
        return lambda b, p, pt: (layer, pt[b, jnp.minimum(p, n_steps - 1) * pages + i], 0, 0, 0)

    per_b = lambda b, p, pt: (b, 0, 0)
    keys = pl.pallas_call(
        functools.partial(_dsa_sample_score_kernel, n_steps=n_steps, pages=pages, dec_seq=dec_seq),
        grid_spec=pltpu.PrefetchScalarGridSpec(
            num_scalar_prefetch=1, grid=(nb, n_steps + 1),
            in_specs=[pl.BlockSpec((1, SROWS, IDX_HEADS * IDX_DIM), per_b),
                      pl.BlockSpec((1, SROWS, LANES), per_b)]
            + [pl.BlockSpec((1, 1, IDX_DIM, PAGE_SIZE), page(i)) for i in range(pages)]
            + [pl.BlockSpec((1, IDX_DIM, PAGE_SIZE), per_b)],
            out_specs=pl.BlockSpec((1, SROWS, width), lambda b, p, pt: (b, 0, p))),
        out_shape=jax.ShapeDtypeStruct((nb, SROWS, n_keys), I32),
        compiler_params=_cparams(("arbitrary", "arbitrary")),
        name="dsa_sample_score",
    )(page_table, qi_s, kw_s, *([cki] * pages), ki_new)
    grows = B_REP * SROWS
    return pl.pallas_call(
        functools.partial(_dsa_sample_attend_kernel, n_steps=n_steps, pages=pages, topk=topk),
        grid_spec=pltpu.PrefetchScalarGridSpec(
            num_scalar_prefetch=1, grid=(nb, n_steps + 1),
            in_specs=[pl.BlockSpec((1, SROWS, B_WIDTH), per_b),
                      pl.BlockSpec((1, SROWS, n_keys), per_b)]
            + [pl.BlockSpec((1, 1, B_KV_HEADS, HEAD_DIM, PAGE_SIZE), page_kv(i)) for i in range(pages)] * 2
            + [pl.BlockSpec((1, B_KV_HEADS, HEAD_DIM, PAGE_SIZE), lambda b, p, pt: (b, 0, 0, 0)),
               pl.BlockSpec((1, B_KV_HEADS, HEAD_DIM, PAGE_SIZE), lambda b, p, pt: (b, 0, 0, 0)),
               pl.BlockSpec(delta_s.shape, lambda b, p, pt: (0, 0, 0, 0))],
            out_specs=pl.BlockSpec((1, SROWS, B_WIDTH), per_b),
            scratch_shapes=[pltpu.VMEM((SROWS, 1), I32), pltpu.VMEM((SROWS, 1), I32),
                            pltpu.VMEM((B_KV_HEADS, grows, 1), F32), pltpu.VMEM((B_KV_HEADS, grows, 1), F32),
                            pltpu.VMEM((B_KV_HEADS, grows, HEAD_DIM), F32)]),
        out_shape=jax.ShapeDtypeStruct((nb, SROWS, B_WIDTH), F32),
        compiler_params=_cparams(("arbitrary", "arbitrary")),
        name="dsa_sample_attend",
    )(page_table, q_s, keys, *([ck] * pages), *([cv] * pages), k_new, v_new, delta_s)


def _pool_kernel(c_ref, halo_ref, w_ref, scale_ref, o_ref, *, block, offset):
    i = pl.program_id(1)
    c = c_ref[0]
    halo = jnp.where(i == 0, 0.0, halo_ref[0])
    full = jnp.concatenate([halo, c], axis=0)
    sums = [full]
    for sh in (1, 2, 4, 8):
        sums.append(sums[-1] + pltpu.roll(sums[-1], sh, axis=0))
    lane = lax.broadcasted_iota(I32, (block, C_WIDTH), 1)
    pos = i * block + lax.broadcasted_iota(I32, (block, C_WIDTH), 0) - offset
    win = jnp.zeros((block, C_WIDTH), F32)
    cnt = jnp.ones((block, C_WIDTH), F32)
    for gi, w in enumerate(POOL_WINDOWS):
        in_g = (lane >= gi * C_GROUP_DIM) & (lane < (gi + 1) * C_GROUP_DIM)
        win = jnp.where(in_g, sums[gi + 1][POOL_HALO:, :], win)
        cnt = jnp.where(in_g, jnp.clip(pos + 1, 1, w).astype(F32), cnt)
    d = win / cnt - c
    o_ref[0] = jnp.dot(d.astype(BF16), w_ref[...], preferred_element_type=F32) * scale_ref[...]


def _pool(c, w_bd, scale, block, offset):
    n_seq, length, _ = c.shape
    ratio = block // POOL_HALO
    return pl.pallas_call(
        functools.partial(_pool_kernel, block=block, offset=offset),
        grid=(n_seq, length // block),
        in_specs=[pl.BlockSpec((1, block, C_WIDTH), lambda s, i: (s, i, 0)),
                  pl.BlockSpec((1, POOL_HALO, C_WIDTH), lambda s, i: (s, jnp.maximum(i * ratio - 1, 0), 0)),
                  pl.BlockSpec((C_WIDTH, C_WIDTH), lambda s, i: (0, 0)),
                  pl.BlockSpec((1, C_WIDTH), lambda s, i: (0, 0))],
        out_specs=pl.BlockSpec((1, block, C_WIDTH), lambda s, i: (s, i, 0)),
        out_shape=jax.ShapeDtypeStruct(c.shape, F32),
        compiler_params=_cparams(("parallel", "parallel")),
        name="pool",
    )(c, c, w_bd, scale)


def _outproj_kernel(x_ref, oa_ref, ob_ref, oc_ref, wo_ref, g_ref, wr_ref, br_ref, xo_ref, h_ref, gm_ref):
    mix = jnp.dot(oa_ref[...].astype(BF16), wo_ref[0:A_WIDTH, :], preferred_element_type=F32)
    mix += jnp.dot(ob_ref[...].astype(BF16), wo_ref[A_WIDTH:A_WIDTH + B_WIDTH, :], preferred_element_type=F32)
    mix += jnp.dot(oc_ref[...].astype(BF16), wo_ref[A_WIDTH + B_WIDTH:, :], preferred_element_type=F32)
    x = x_ref[...] + mix
    xo_ref[...] = x
    h = _rms(x, g_ref[...])
    h_ref[...] = h.astype(BF16)
    logits = jnp.dot(h, wr_ref[...], precision=HIGHEST, preferred_element_type=F32) + br_ref[...]
    lane = lax.broadcasted_iota(I32, logits.shape, 1)
    work = logits
    vals, hots = [], []
    for _ in range(TOP_K):
        m = jnp.max(work, axis=1, keepdims=True)
        idx = jnp.min(jnp.where(work == m, lane, N_EXPERTS), axis=1, keepdims=True)
        hot = lane == idx
        vals.append(m)
        hots.append(hot)
        work = jnp.where(hot, -jnp.inf, work)
    es = [jnp.exp(v - vals[0]) for v in vals]
    den = es[0] + es[1] + es[2] + es[3]
    gm = jnp.zeros(logits.shape, F32)
    for e, hot in zip(es, hots):
        gm = jnp.where(hot, e / den, gm)
    gm_ref[...] = gm


def _outproj(x, oa, ob, oc, wo, g, wr, br):
    nt = x.shape[0]
    tile = lambda n: pl.BlockSpec((TOK_TILE, n), lambda i: (i, 0))
    full = lambda a: pl.BlockSpec(a.shape, lambda i: (0, 0))
    return pl.pallas_call(
        _outproj_kernel,
        grid=(nt // TOK_TILE,),
        in_specs=[tile(D_MODEL), tile(A_WIDTH), tile(B_WIDTH), tile(C_WIDTH), full(wo), full(g), full(wr), full(br)],
        out_specs=[tile(D_MODEL), tile(D_MODEL), tile(N_EXPERTS)],
        out_shape=[jax.ShapeDtypeStruct((nt, D_MODEL), F32), jax.ShapeDtypeStruct((nt, D_MODEL), BF16),
                   jax.ShapeDtypeStruct((nt, N_EXPERTS), F32)],
        compiler_params=_cparams(("parallel",)),
        name="outproj",
    )(x, oa, ob, oc, wo, g, wr, br)


def _moe_kernel(h_ref, gm_ref, x_ref, w1_ref, b1_ref, w2_ref, b2_ref, gf_ref, o_ref, acc_scr, *, final):
    e = pl.program_id(1)

    @pl.when(e == 0)
    def _():
        acc_scr[...] = jnp.zeros_like(acc_scr)

    u = jnp.dot(h_ref[...], w1_ref[0, 0].astype(BF16), preferred_element_type=F32) + b1_ref[0, 0]
    glu = jnp.minimum(u[:, :D_FF], SWIGLU_LIMIT)
    lin = jnp.clip(u[:, D_FF:], -SWIGLU_LIMIT, SWIGLU_LIMIT)
    act = (glu * _sigmoid(SWIGLU_ALPHA * glu) * (lin + 1.0)).astype(BF16)
    out = jnp.dot(act, w2_ref[0, 0].astype(BF16), preferred_element_type=F32) + b2_ref[0, 0]
    gm = gm_ref[...]
    lane = lax.broadcasted_iota(I32, gm.shape, 1)
    gate = jnp.sum(jnp.where(lane == e, gm, 0.0), axis=1, keepdims=True)
    acc_scr[...] += gate * out

    @pl.when(e == N_EXPERTS - 1)
    def _():
        x = x_ref[...] + acc_scr[...]
        if final:
            x = _rms(x, gf_ref[...])
        o_ref[...] = x


def _moe(h, gm, x, w1, b1, w2, b2, gf, layer, final):
    nt = x.shape[0]
    tile = lambda n: pl.BlockSpec((TOK_TILE, n), lambda i, e: (i, 0))
    return pl.pallas_call(
        functools.partial(_moe_kernel, final=final),
        grid=(nt // TOK_TILE, N_EXPERTS),
        in_specs=[tile(D_MODEL), tile(N_EXPERTS), tile(D_MODEL),
                  pl.BlockSpec((1, 1, D_MODEL, 2 * D_FF), lambda i, e: (layer, e, 0, 0)),
                  pl.BlockSpec((1, 1, 1, 2 * D_FF), lambda i, e: (layer, e, 0, 0)),
                  pl.BlockSpec((1, 1, D_FF, D_MODEL), lambda i, e: (layer, e, 0, 0)),
                  pl.BlockSpec((1, 1, 1, D_MODEL), lambda i, e: (layer, e, 0, 0)),
                  pl.BlockSpec((1, D_MODEL), lambda i, e: (0, 0))],
        out_specs=tile(D_MODEL),
        out_shape=jax.ShapeDtypeStruct((nt, D_MODEL), F32),
        scratch_shapes=[pltpu.VMEM((TOK_TILE, D_MODEL), F32)],
        compiler_params=_cparams(("parallel", "arbitrary")),
        name="moe",
    )(h, gm, x, w1, b1[:, :, None, :], w2, b2[:, :, None, :], gf)


def _rel_bucket(dist):
    n = jnp.maximum(dist, 0)
    max_exact = N_BUCKETS // 2
    log_ratio = jnp.log(jnp.maximum(n, 1).astype(jnp.float32) / max_exact) / math.log(MAX_DISTANCE / max_exact)
    large = jnp.minimum(max_exact + (log_ratio * (N_BUCKETS - max_exact)).astype(jnp.int32), N_BUCKETS - 1)
    return jnp.where(n < max_exact, n, large)


def _bias_delta(rel_bias, dist):
    rb = rel_bias.astype(F32)
    d = jnp.dot(jax.nn.one_hot(_rel_bucket(dist), N_BUCKETS, dtype=F32), rb, precision=HIGHEST) - rb[N_BUCKETS - 1]
    d = jnp.where(((dist >= 0) & (dist < MAX_DISTANCE))[..., None], d, 0.0)
    return jnp.moveaxis(d, -1, 0)


def _group_rows(delta, rows):
    return delta.reshape(B_KV_HEADS, B_REP * rows, delta.shape[-1])


def kernel(x_prompt, x_sample, cache_k, cache_v, cache_kidx, state_wkv, state_shift, state_pool, page_table,
           meta_tokens, rel_bias, norm_mix, norm_ffn, norm_final, w_in, w_out,
           a_mu, a_w0, a_w2, a_a0, a_a2, a_g2, a_kk, a_ka, a_rk, a_ln_w, a_ln_b,
           c_w, c_scale, moe_w_router, moe_b_router, moe_w1, moe_b1, moe_w2, moe_b2):
    assert x_prompt.shape[0] == 1
    depth = w_in.shape[0]
    seq = x_prompt.shape[1]
    nb, dec_seq = x_sample.shape[:2]
    assert dec_seq <= SROWS
    t_p = seq + N_META
    n_s = nb * dec_seq
    n_qblk = -(-t_p // Q_TILE)
    nkp = KEY_CHUNK * ((n_qblk + 4) // 4 + 1)
    nt = _round_up(max(t_p + n_s, Q_TILE * n_qblk), TOK_TILE)
    n_pages = page_table.shape[1]
    past = n_pages * PAGE_SIZE
    topk_p = min(TOPK_MAX, seq // 4)
    topk_s = min(TOPK_MAX, (past + dec_seq) // 4)
    n_phys = cache_k.shape[1]

    x = jnp.concatenate([meta_tokens.astype(F32), x_prompt[0], x_sample.reshape(n_s, D_MODEL),
                         jnp.zeros((nt - t_p - n_s, D_MODEL), F32)], axis=0)

    ones_blk = jnp.kron(jnp.eye(A_HEADS, dtype=F32), jnp.ones((HEAD_DIM, HEAD_DIM), F32))
    qi_ = jnp.arange(Q_TILE)[:, None]
    cj_ = jnp.arange(2 * Q_TILE)[None, :]
    side = TAIL_WIDTH - 2 * Q_TILE
    delta_p = jnp.pad(_bias_delta(rel_bias, Q_TILE + qi_ - cj_), ((0, 0), (0, 0), (side, side)))
    ts_ = jnp.arange(SROWS)[:, None]
    cs_ = jnp.arange(PAGE_SIZE)[None, :]
    delta_s = jnp.stack([_group_rows(_bias_delta(rel_bias, ts_ + PAGE_SIZE - cs_), SROWS),
                         _group_rows(_bias_delta(rel_bias, ts_ - cs_), SROWS)])

    outs = {n: [] for n in ("kp", "vp", "kip", "wkvp", "shp", "plp", "ks", "vs", "kis", "wkvs", "shs", "pls")}
    for l in range(depth):
        wl = w_in[l]
        b0 = A_PROJ
        w_cat = jnp.concatenate([
            wl[:, :A_PROJ], wl[:, b0:b0 + B_WIDTH + 2 * KV_WIDTH + IDX_HEADS * IDX_DIM],
            wl[:, b0 + B_PROJ - IDX_DIM - IDX_HEADS:b0 + B_PROJ],
            jnp.zeros((D_MODEL, LANES - IDX_DIM - IDX_HEADS), F32), wl[:, b0 + B_PROJ:]], axis=1).astype(BF16)
        pa, q, k, v, qi, kw, pc = _inproj(x, norm_mix[l][None], w_cat)

        sl_p = slice(0, t_p)
        sl_s = slice(t_p, t_p + n_s)
        outs["kp"].append(k[sl_p].reshape(1, t_p, B_KV_HEADS, HEAD_DIM))
        outs["vp"].append(v[sl_p].reshape(1, t_p, B_KV_HEADS, HEAD_DIM))
        outs["kip"].append(kw[sl_p, :IDX_DIM][None])
        outs["ks"].append(k[sl_s].reshape(nb, dec_seq, B_KV_HEADS, HEAD_DIM))
        outs["vs"].append(v[sl_s].reshape(nb, dec_seq, B_KV_HEADS, HEAD_DIM))
        outs["kis"].append(kw[sl_s, :IDX_DIM].reshape(nb, dec_seq, IDX_DIM))
        outs["shp"].append(pa[t_p - 1:t_p])
        pa_s = pa[sl_s].reshape(nb, dec_seq, A_PROJ)
        outs["shs"].append(pa_s[:, -1])
        pc_s = pc[sl_s].reshape(nb, dec_seq, C_WIDTH)
        outs["plp"].append(pc[t_p - POOL_BUF:t_p][None])
        pool_full = jnp.concatenate([state_pool[l], pc_s], axis=1)
        outs["pls"].append(pool_full[:, -POOL_BUF:])

        rw = (a_mu[l], a_w0[l], a_w2[l], a_a0[l], a_a2[l], a_g2[l], a_kk[l], a_ka[l], a_rk[l], a_ln_w[l], a_ln_b[l])
        oa_p, st_p = _rwkv_chunked(pa[None], jnp.zeros((1, 1, A_PROJ), F32),
                                   jnp.zeros((1, A_HEADS, HEAD_DIM, HEAD_DIM), F32),
                                   rw, ones_blk, t_p, RWKV_BLOCK, RWKV_CHUNK)
        pa_s8 = jnp.pad(pa_s, ((0, 0), (0, SROWS - dec_seq), (0, 0)))
        oa_s, st_s = _rwkv_chunked(pa_s8, state_shift[l][:, None, :], state_wkv[l].astype(F32),
                                   rw, ones_blk, dec_seq, SROWS, SROWS)
        outs["wkvp"].append(st_p.astype(state_wkv.dtype))
        outs["wkvs"].append(st_s.astype(state_wkv.dtype))

        n_real = min(nt, nkp - KEY_FRONT_PAD)
        front = lambda a: jnp.pad(a[:n_real].astype(BF16), ((KEY_FRONT_PAD, nkp - KEY_FRONT_PAD - n_real), (0, 0)))
        ob_p = _dsa_prompt(q, qi, kw, front(k), front(v), front(kw[:, :IDX_DIM]), delta_p, n_qblk, topk_p)
        rows8 = lambda a: jnp.pad(a[sl_s].reshape(nb, dec_seq, -1), ((0, 0), (0, SROWS - dec_seq), (0, 0)))
        page8 = lambda a: jnp.pad(a[sl_s].reshape(nb, dec_seq, -1), ((0, 0), (0, PAGE_SIZE - dec_seq), (0, 0)))
        kv_page = lambda a: page8(a).reshape(nb, PAGE_SIZE, B_KV_HEADS, HEAD_DIM).transpose(0, 2, 3, 1)
        ob_s = _dsa_sample(page_table, rows8(q), rows8(qi), rows8(kw),
                           cache_kidx.transpose(0, 1, 3, 2), cache_k.transpose(0, 1, 3, 4, 2),
                           cache_v.transpose(0, 1, 3, 4, 2), l, page8(kw[:, :IDX_DIM]).transpose(0, 2, 1),
                           kv_page(k), kv_page(v), delta_s, dec_seq, topk_s)

        w_bd = jax.scipy.linalg.block_diag(*[c_w[l][gi] for gi in range(len(POOL_WINDOWS))]).astype(BF16)
        oc_p = _pool(pc[None], w_bd, c_scale[l][None], TOK_TILE, 0)
        pool_in = jnp.concatenate([jnp.zeros((nb, 1, C_WIDTH), F32), pool_full,
                                   jnp.zeros((nb, 2 * POOL_HALO - 1 - POOL_BUF - dec_seq, C_WIDTH), F32)], axis=1)
        oc_s = _pool(pool_in, w_bd, c_scale[l][None], POOL_HALO, 1)[:, POOL_HALO:POOL_HALO + dec_seq]

        def merge(p_rows, s_rows):
            w = p_rows.shape[-1]
            return jnp.concatenate([p_rows[:t_p], s_rows.reshape(n_s, w), jnp.zeros((nt - t_p - n_s, w), F32)], axis=0)

        oa = merge(oa_p[0], oa_s[:, :dec_seq])
        ob = merge(ob_p, ob_s[:, :dec_seq])
        oc = merge(oc_p[0], oc_s)
        x, h2, gm = _outproj(x, oa, ob, oc, w_out[l].astype(BF16), norm_ffn[l][None], moe_w_router[l],
                             moe_b_router[l][None])
        x = _moe(h2, gm, x, moe_w1, moe_b1, moe_w2, moe_b2, norm_final[None], l, l == depth - 1)

    y_prompt = x[N_META:t_p][None]
    y_sample = x[t_p:t_p + n_s].reshape(nb, dec_seq, D_MODEL)
    st = lambda n: jnp.stack(outs[n])
    return (y_prompt, y_sample, st("kp"), st("vp"), st("kip"), st("wkvp"), st("shp"), st("plp"),
            st("ks"), st("vs"), st("kis"), st("wkvs"), st("shs"), st("pls"))
```

```python
import functools
import math

import jax
import jax.numpy as jnp
import numpy as np
from jax import lax
from jax.experimental import pallas as pl
from jax.experimental.pallas import tpu as pltpu

F32 = jnp.float32
BF16 = jnp.bfloat16
I32 = jnp.int32
HIGHEST = lax.Precision.HIGHEST

D_MODEL = 1024
N_META = 16
HEAD_DIM = 64
A_WIDTH = 384
A_HEADS = 6
LORA_W, LORA_A, LORA_G = 64, 64, 128
A_PROJ = 3 * A_WIDTH + LORA_W + LORA_A + LORA_G
GN_EPS = 64e-5
B_WIDTH = 384
B_HEADS = 6
B_KV_HEADS = 2
B_REP = 3
KV_WIDTH = 128
IDX_HEADS = 4
IDX_DIM = 64
IDX_SCALE = (IDX_HEADS * IDX_DIM) ** -0.5
TOPK_MAX = 256
B_PROJ = B_WIDTH + 2 * KV_WIDTH + IDX_HEADS * IDX_DIM + IDX_DIM + IDX_HEADS
C_WIDTH = 256
C_GROUP_DIM = 64
POOL_WINDOWS = (2, 4, 8, 16)
POOL_BUF = 15
N_BUCKETS = 32
MAX_DISTANCE = 128
N_EXPERTS = 32
TOP_K = 4
D_FF = 1024
SWIGLU_LIMIT = 7.0
SWIGLU_ALPHA = 1.702
RMS_EPS = 1e-5
PAGE_SIZE = 128

LANES = 128
SUBLANES = 8
Q_TILE = 128
KEY_CHUNK = 512
KEY_FRONT_PAD = 128
TAIL_WIDTH = 640
TOP_M = 12
SAMPLE_PAGES = 16
TOK_TILE = 512
RWKV_BLOCK = 128
RWKV_CHUNK = 64
POOL_HALO = 16
VMEM_LIMIT = 56 * 1024 * 1024

INT_MIN = -(2 ** 31)
NEG_INF_KEY = np.int32(np.uint32(0xFF800000) ^ np.uint32(0x7FFFFFFF))
NEG_BIG = -1e30


def _round_up(x, m):
    return (x + m - 1) // m * m


def _cparams(sem):
    return pltpu.CompilerParams(dimension_semantics=sem, vmem_limit_bytes=VMEM_LIMIT)


def _rms(x, g):
    return x * lax.rsqrt(jnp.mean(x * x, axis=-1, keepdims=True) + RMS_EPS) * g


def _sigmoid(x):
    return 1.0 / (1.0 + jnp.exp(-x))


IN_COLS = (A_PROJ, B_WIDTH, KV_WIDTH, KV_WIDTH, IDX_HEADS * IDX_DIM, LANES, C_WIDTH)


def _inproj_kernel(x_ref, g_ref, w_ref, *out_refs):
    hb = _rms(x_ref[...], g_ref[...]).astype(BF16)
    off = 0
    for ref in out_refs:
        n = ref.shape[1]
        ref[...] = jnp.dot(hb, w_ref[:, off:off + n], preferred_element_type=F32)
        off += n


def _inproj(x, g, w):
    nt = x.shape[0]
    return pl.pallas_call(
        _inproj_kernel,
        grid=(nt // TOK_TILE,),
        in_specs=[pl.BlockSpec((TOK_TILE, D_MODEL), lambda i: (i, 0)),
                  pl.BlockSpec((1, D_MODEL), lambda i: (0, 0)),
                  pl.BlockSpec(w.shape, lambda i: (0, 0))],
        out_specs=[pl.BlockSpec((TOK_TILE, n), lambda i: (i, 0)) for n in IN_COLS],
        out_shape=[jax.ShapeDtypeStruct((nt, n), F32) for n in IN_COLS],
        compiler_params=_cparams(("parallel",)),
        name="inproj",
    )(x, g, w)


NN_DIMS = (((1,), (0,)), ((), ()))
NT_DIMS = (((1,), (1,)), ((), ()))
TN_DIMS = (((0,), (0,)), ((), ()))


def _dot3(a, b, dims=NN_DIMS):
    a_hi = a.astype(BF16)
    b_hi = b.astype(BF16)
    a_lo = (a - a_hi.astype(F32)).astype(BF16)
    b_lo = (b - b_hi.astype(F32)).astype(BF16)
    dg = functools.partial(lax.dot_general, dimension_numbers=dims, preferred_element_type=F32)
    return dg(a_hi, b_hi) + (dg(a_hi, b_lo) + dg(a_lo, b_hi))


def _rwkv_chunk_kernel(pa_ref, sh_ref, s0_ref, mu_ref, w0_ref, w2_ref, a0_ref, a2_ref, g2_ref, kk_ref, ka_ref,
                       rk_ref, lnw_ref, lnb_ref, ones_ref, tri_ref, o_ref, s_out_ref, carry_scr, s_scr,
                       *, seq_len, block, chunk):
    blk = pl.program_id(1)

    @pl.when(blk == 0)
    def _():
        carry_scr[...] = sh_ref[0]
        s_scr[...] = s0_ref[0]

    pa = pa_ref[0]
    row = lax.broadcasted_iota(I32, (block, 1), 0)
    prev = jnp.where(row == 0, carry_scr[...], pltpu.roll(pa, 1, axis=0))
    carry_scr[...] = pa[block - 1:block, :]
    xm = pa + (prev - pa) * mu_ref[...]
    r = xm[:, 0:A_WIDTH]
    k = xm[:, A_WIDTH:2 * A_WIDTH]
    v = xm[:, 2 * A_WIDTH:3 * A_WIDTH]
    lw = xm[:, 3 * A_WIDTH:3 * A_WIDTH + LORA_W]
    la = xm[:, 3 * A_WIDTH + LORA_W:3 * A_WIDTH + LORA_W + LORA_A]
    lg = xm[:, 3 * A_WIDTH + LORA_W + LORA_A:]

    def dot_hi(a, b):
        return jnp.dot(a, b, precision=HIGHEST, preferred_element_type=F32)

    z = -(w0_ref[...] + dot_hi(jnp.tanh(lw), w2_ref[...]))
    softplus = jnp.maximum(z, 0.0) + jnp.log(1.0 + jnp.exp(-jnp.abs(z)))
    log_w = -jnp.exp(-softplus - 0.5)
    a = _sigmoid(a0_ref[...] + dot_hi(la, a2_ref[...]))
    g = dot_hi(_sigmoid(lg), g2_ref[...])
    ones_blk = ones_ref[...]
    kk = k * kk_ref[...]
    kk = kk / jnp.maximum(jnp.sqrt(dot_hi(kk * kk, ones_blk)), 1e-12)
    k2 = k * (1.0 + (a - 1.0) * ka_ref[...])
    kka = kk * a

    valid = (blk * block + row) < seq_len
    log_w = jnp.where(valid, log_w, 0.0)
    kk_m, kka_m, k2_m, r_m, v_m = (jnp.where(valid, t, 0.0) for t in (kk, kka, k2, r, v))

    cum = dot_hi(tri_ref[0], log_w)
    cum_end = dot_hi(tri_ref[1], log_w)
    kap = kk_m * jnp.exp(cum - log_w)
    rt = r_m * jnp.exp(cum)
    inv_p = jnp.exp(-cum)
    kt = k2_m * inv_p
    bt = kka_m * inv_p
    to_end = jnp.exp(cum_end - cum)
    k_end = k2_m * to_end
    b_end = kka_m * to_end
    p_end = jnp.exp(cum_end)

    sub = min(16, chunk)
    ti = lax.broadcasted_iota(I32, (chunk, chunk), 0)
    tj = lax.broadcasted_iota(I32, (chunk, chunk), 1)
    heads = range(A_HEADS)
    o_chunks = []
    for c in range(block // chunk):
        rs = slice(c * chunk, (c + 1) * chunk)
        hsl = lambda t, h: t[rs, h * HEAD_DIM:(h + 1) * HEAD_DIM]
        lhs = [jnp.concatenate([hsl(kap, h), hsl(rt, h)], axis=0) for h in heads]
        rhs = [jnp.concatenate([hsl(kt, h), hsl(bt, h)], axis=0) for h in heads]
        vh = [hsl(v_m, h) for h in heads]
        gram = [_dot3(lhs[h], rhs[h], NT_DIMS) for h in heads]
        from_s = [_dot3(lhs[h], s_scr[h], NT_DIMS) for h in heads]
        a1 = [jnp.where(ti > tj, gram[h][:chunk, :chunk], 0.0) for h in heads]
        a2 = [jnp.where(ti > tj, gram[h][:chunk, chunk:], 0.0) for h in heads]
        a3 = [jnp.where(ti >= tj, gram[h][chunk:, :chunk], 0.0) for h in heads]
        a4 = [jnp.where(ti >= tj, gram[h][chunk:, chunk:], 0.0) for h in heads]
        u = [from_s[h][:chunk] + _dot3(a1[h], vh[h]) for h in heads]
        zs = [[] for _ in heads]
        for jb in range(chunk // sub):
            js = slice(jb * sub, (jb + 1) * sub)
            zj = []
            for h in heads:
                t = u[h][js]
                if jb > 0:
                    t = t - _dot3(a2[h][js, :jb * sub], jnp.concatenate(zs[h], axis=0))
                zj.append(t)
            for i in range(sub - 1):
                for h in heads:
                    col = jnp.broadcast_to(a2[h][js, jb * sub + i:jb * sub + i + 1], (sub, HEAD_DIM))
                    zj[h] = zj[h] - col * zj[h][i:i + 1, :]
            for h in heads:
                zs[h].append(zj[h])
        zh = [jnp.concatenate(zs[h], axis=0) for h in heads]
        vz = [jnp.concatenate([vh[h], zh[h]], axis=0) for h in heads]
        o_chunks.append(jnp.concatenate(
            [from_s[h][chunk:] + _dot3(jnp.concatenate([a3[h], -a4[h]], axis=1), vz[h]) for h in heads], axis=1))
        for h in heads:
            ends = jnp.concatenate([hsl(k_end, h), -hsl(b_end, h)], axis=0)
            s_scr[h] = s_scr[h] * hsl(p_end, h)[0:1, :] + _dot3(vz[h], ends, TN_DIMS)
    s_out_ref[0] = s_scr[...]

    o = jnp.concatenate(o_chunks, axis=0) if len(o_chunks) > 1 else o_chunks[0]
    inv = 1.0 / HEAD_DIM
    mean = dot_hi(o, ones_blk) * inv
    cen = o - mean
    var = dot_hi(cen * cen, ones_blk) * inv
    o = cen * lax.rsqrt(var + GN_EPS) * lnw_ref[...] + lnb_ref[...]
    bonus = dot_hi(r * k2 * rk_ref[...], ones_blk) * v
    o_ref[0] = (o + bonus) * g


def _rwkv_chunked(pa, shift_prev, s0, params, ones_blk, seq_len, block, chunk):
    n_seq, length, _ = pa.shape
    n_blk = -(-seq_len // block)
    idx = np.arange(block)
    same = (idx[:, None] // chunk) == (idx[None, :] // chunk)
    tri = jnp.asarray(np.stack([same & (idx[None, :] <= idx[:, None]), same]).astype(np.float32))
    row_spec = lambda n: pl.BlockSpec((1, n), lambda s, i: (0, 0))
    mat_spec = lambda a: pl.BlockSpec(a.shape, lambda s, i: (0,) * a.ndim)
    mu, w0, w2, a0, a2, g2, k_k, k_a, r_k, ln_w, ln_b = params
    kern = functools.partial(_rwkv_chunk_kernel, seq_len=seq_len, block=block, chunk=chunk)
    state_spec = pl.BlockSpec((1, A_HEADS, HEAD_DIM, HEAD_DIM), lambda s, i: (s, 0, 0, 0))
    return pl.pallas_call(
        kern,
        grid=(n_seq, n_blk),
        in_specs=[pl.BlockSpec((1, block, A_PROJ), lambda s, i: (s, i, 0)),
                  pl.BlockSpec((1, 1, A_PROJ), lambda s, i: (s, 0, 0)),
                  state_spec,
                  row_spec(A_PROJ), row_spec(A_WIDTH), mat_spec(w2), row_spec(A_WIDTH), mat_spec(a2),
                  mat_spec(g2), row_spec(A_WIDTH), row_spec(A_WIDTH), row_spec(A_WIDTH), row_spec(A_WIDTH),
                  row_spec(A_WIDTH), mat_spec(ones_blk), mat_spec(tri)],
        out_specs=[pl.BlockSpec((1, block, A_WIDTH), lambda s, i: (s, i, 0)), state_spec],
        out_shape=[jax.ShapeDtypeStruct((n_seq, length, A_WIDTH), F32),
                   jax.ShapeDtypeStruct((n_seq, A_HEADS, HEAD_DIM, HEAD_DIM), F32)],
        scratch_shapes=[pltpu.VMEM((1, A_PROJ), F32), pltpu.VMEM((A_HEADS, HEAD_DIM, HEAD_DIM), F32)],
        compiler_params=_cparams(("arbitrary", "arbitrary")),
        name="rwkv",
    )(pa, shift_prev, s0, mu[None], w0[None], w2, a0[None], a2, g2, k_k[None], k_a[None], r_k[None],
      ln_w[None], ln_b[None], ones_blk, tri)


def _sortable_key(s):
    bits = pltpu.bitcast(s + 0.0, I32)
    return bits ^ ((bits >> 31) & jnp.int32(0x7FFFFFFF))


def _index_scores(qi_b, wi, ki_b, transposed=False):
    s = None
    for h in range(IDX_HEADS):
        d = lax.dot_general(qi_b[:, h * IDX_DIM:(h + 1) * IDX_DIM], ki_b,
                            (((1,), (0 if transposed else 1,)), ((), ())), preferred_element_type=F32)
        t = jnp.maximum(d, 0.0) * wi[:, h:h + 1]
        s = t if s is None else s + t
    return s


def _topk_threshold(count_fn, count_tie_fn, rows, topk):
    kf = jnp.float32(topk)
    c0 = count_fn(jnp.zeros((rows, 1), I32), False)
    prefix0 = jnp.where(c0 >= kf, jnp.int32(0), jnp.int32(INT_MIN))

    def bit_step(i, prefix):
        cand = prefix | lax.shift_left(jnp.int32(1), jnp.int32(30) - i)
        return jnp.where(count_fn(cand, False) >= kf, cand, prefix)

    tau = lax.fori_loop(0, 31, bit_step, prefix0)
    c_gt = count_fn(tau, True)
    c_ge = count_fn(tau, False)
    need = kf - c_gt
    excess = jnp.max(c_ge - c_gt - need)

    def tie_search(_):
        def jbit(i, jl):
            cand = jl | lax.shift_left(jnp.int32(1), jnp.int32(14) - i)
            return jnp.where(count_tie_fn(tau, cand) < need, cand, jl)
        return lax.fori_loop(0, 15, jbit, jnp.zeros((rows, 1), I32))

    jlim = lax.cond(excess > 0.0, tie_search, lambda _: jnp.full((rows, 1), 2 ** 30, I32), 0)
    return tau, jlim


def _select(keys, aidx, tau, jlim):
    return ((keys > tau) | ((keys == tau) & (aidx <= jlim))) & (keys > jnp.int32(NEG_INF_KEY))


def _softmax_step(state, qg, kt, vt, sel_rows, delta):
    m, l, acc = state
    lg = lax.dot_general(qg, kt, (((1,), (1,)), ((), ())), preferred_element_type=F32)
    if delta is not None:
        lg = lg + delta
    lg = jnp.where(sel_rows, lg, NEG_BIG)
    m_new = jnp.maximum(m, jnp.max(lg, axis=1, keepdims=True))
    alpha = jnp.exp(m - m_new)
    p = jnp.where(sel_rows, jnp.exp(lg - m_new), 0.0)
    l = alpha * l + jnp.sum(p, axis=1, keepdims=True)
    acc = alpha * acc + jnp.dot(p.astype(BF16), vt, preferred_element_type=F32)
    return m_new, l, acc


def _dsa_prompt_kernel(q_ref, qi_ref, kw_ref, kp_ref, vp_ref, kip_ref, dl_ref, o_ref, key_scr, cand_scr, *, topk):
    j = pl.program_id(0)
    rows = Q_TILE
    n_chunks = (j + 5) // 4
    qi_b = qi_ref[...].astype(BF16)
    wi = kw_ref[:, IDX_DIM:IDX_DIM + IDX_HEADS] * IDX_SCALE
    qpos = j * Q_TILE + lax.broadcasted_iota(I32, (rows, 1), 0)
    lane_c = lax.broadcasted_iota(I32, (1, KEY_CHUNK), 1)

    def score_chunk(c, masked):
        a0 = pl.multiple_of(c * KEY_CHUNK, KEY_CHUNK)
        s = _index_scores(qi_b, wi, kip_ref[pl.ds(a0, KEY_CHUNK), :])
        if masked:
            kpos = a0 - KEY_FRONT_PAD + lane_c
            s = jnp.where((kpos <= qpos) & (kpos >= 0), s, -jnp.inf)
        key_scr[:, pl.ds(a0, KEY_CHUNK)] = _sortable_key(s)

    n_plain = jnp.clip((j * Q_TILE + KEY_FRONT_PAD - KEY_CHUNK) // KEY_CHUNK, 0, n_chunks - 1)
    score_chunk(0, True)
    lax.fori_loop(1, 1 + n_plain, lambda c, x: (score_chunk(c, False), x)[1], 0)
    lax.fori_loop(1 + n_plain, n_chunks, lambda c, x: (score_chunk(c, True), x)[1], 0)

    key_scr[:, pl.ds(pl.multiple_of(n_chunks * KEY_CHUNK, KEY_CHUNK), KEY_CHUNK)] = jnp.full(
        (rows, KEY_CHUNK), NEG_INF_KEY, I32)

    n_fold = KEY_CHUNK // LANES

    def chunk_at(ref, c):
        return ref[:, pl.ds(pl.multiple_of(c * KEY_CHUNK, KEY_CHUNK), KEY_CHUNK)]

    def count32(cand, strict):
        def body(c, acc):
            kk = chunk_at(key_scr, c)
            hit = (kk > cand) if strict else (kk >= cand)
            m = jnp.where(hit, 1.0, 0.0)
            for i in range(n_fold):
                acc = acc + m[:, i * LANES:(i + 1) * LANES]
            return acc
        acc = lax.fori_loop(0, n_chunks, body, jnp.zeros((rows, LANES), F32))
        return jnp.sum(acc, axis=1, keepdims=True)

    kf = jnp.float32(topk)

    def kth_largest(count_ge):
        def bit_step(i, prefix):
            cand = prefix | lax.shift_left(jnp.int32(1), jnp.int32(30) - i)
            return jnp.where(count_ge(cand) >= kf, cand, prefix)
        nonneg = count_ge(jnp.zeros((rows, 1), I32)) >= kf
        return lax.fori_loop(0, 31, bit_step, jnp.where(nonneg, jnp.int32(0), jnp.int32(INT_MIN)))

    pair = 2 * SUBLANES
    for rp in range(rows // pair):
        r0 = rp * pair

        def insert_chunk(c, lists, r0=r0):
            a0 = pl.multiple_of(c * KEY_CHUNK, KEY_CHUNK)
            blk = key_scr[r0:r0 + pair, pl.ds(a0, KEY_CHUNK)]
            blk = pltpu.bitcast(blk ^ ((blk >> 31) & jnp.int32(0x7FFFFFFF)), F32)
            lists = list(lists)
            for half in range(2):
                for t in range(n_fold):
                    x = blk[half * SUBLANES:(half + 1) * SUBLANES, t * LANES:(t + 1) * LANES]
                    for i in range(TOP_M):
                        a = lists[half * TOP_M + i]
                        lists[half * TOP_M + i] = jnp.maximum(a, x)
                        x = jnp.minimum(a, x)
            return tuple(lists)

        lists = lax.fori_loop(0, n_chunks, insert_chunk,
                              tuple(jnp.full((SUBLANES, LANES), -jnp.inf, F32) for _ in range(2 * TOP_M)))
        for half in range(2):
            for i in range(TOP_M):
                cand_scr[r0 + half * SUBLANES:r0 + (half + 1) * SUBLANES, i * LANES:(i + 1) * LANES] = _sortable_key(
                    lists[half * TOP_M + i])

    def count_cand(cand):
        acc = jnp.zeros((rows, LANES), F32)
        for i in range(TOP_M):
            acc = acc + jnp.where(cand_scr[:, i * LANES:(i + 1) * LANES] >= cand, 1.0, 0.0)
        return jnp.sum(acc, axis=1, keepdims=True)

    tau_c = kth_largest(count_cand)
    gt_c = count32(tau_c, True)
    ge_c = count32(tau_c, False)
    wrong = jnp.max(jnp.where((gt_c < kf) & (ge_c >= kf), 0.0, 1.0))

    def full_search(_):
        t = kth_largest(lambda cand: count32(cand, False))
        return t, count32(t, True), count32(t, False)

    tau, c_gt, c_ge = lax.cond(wrong > 0.0, full_search, lambda _: (tau_c, gt_c, ge_c), 0)
    need = kf - c_gt
    excess = jnp.max(c_ge - c_gt - need)

    def tie_pass(_):
        ri = lax.broadcasted_iota(I32, (LANES, 2 * LANES), 0)
        ci = lax.broadcasted_iota(I32, (LANES, 2 * LANES), 1)
        tri = jnp.where((ri <= ci) | (ci >= LANES), 1.0, 0.0).astype(BF16)

        def body(c, carry):
            seen, jacc = carry
            a0 = pl.multiple_of(c * KEY_CHUNK, KEY_CHUNK)
            kk = key_scr[:, pl.ds(a0, KEY_CHUNK)]
            ties = [kk[:, i * LANES:(i + 1) * LANES] == tau for i in range(n_fold)]
            pres = [jnp.dot(jnp.where(t, 1.0, 0.0).astype(BF16), tri, preferred_element_type=F32) for t in ties]
            for i in range(n_fold):
                rank = jnp.where(ties[i], seen + pres[i][:, :LANES], 1e9)
                aidx = a0 + i * LANES + lax.broadcasted_iota(I32, (1, LANES), 1)
                jacc = jnp.maximum(jacc, jnp.where(rank <= need, aidx, -1))
                seen = seen + pres[i][:, LANES:]
            return seen, jacc
        _, jacc = lax.fori_loop(0, n_chunks, body, (jnp.zeros((rows, LANES), F32), jnp.full((rows, LANES), -1, I32)))
        return jnp.max(jacc, axis=1, keepdims=True)

    jlim = lax.cond(excess > 0.0, tie_pass, lambda _: jnp.full((rows, 1), 2 ** 30, I32), 0)
    jlim = jnp.where(tau == jnp.int32(NEG_INF_KEY), jnp.int32(-1), jlim)
    tau_b = jnp.broadcast_to(tau, (rows, LANES))
    tau_m1_b = tau_b - 1

    q = q_ref[...] * (HEAD_DIM ** -0.5)
    qh = [q[:, h * HEAD_DIM:(h + 1) * HEAD_DIM].astype(BF16) for h in range(B_HEADS)]

    def init_state():
        return (jnp.full((rows, 1), NEG_BIG, F32), jnp.zeros((rows, 1), F32), jnp.zeros((rows, HEAD_DIM), F32))

    def attend(a0, width, states, delta):
        kk = key_scr[:, pl.ds(a0, width)]
        aidx = a0 + lax.broadcasted_iota(I32, (1, width), 1)
        thr = jnp.where(aidx <= jlim, jnp.concatenate([tau_m1_b] * (width // LANES), axis=1),
                        jnp.concatenate([tau_b] * (width // LANES), axis=1))
        madd = jnp.where(kk > thr, 0.0, NEG_BIG)
        kts = [kp_ref[pl.ds(a0, width), g * HEAD_DIM:(g + 1) * HEAD_DIM] for g in range(B_KV_HEADS)]
        vts = [vp_ref[pl.ds(a0, width), g * HEAD_DIM:(g + 1) * HEAD_DIM] for g in range(B_KV_HEADS)]
        lgs = [lax.dot_general(qh[h], kts[h // B_REP], (((1,), (1,)), ((), ())), preferred_element_type=F32)
               for h in range(B_HEADS)]
        mid = []
        for h in range(B_HEADS):
            m, l, _ = states[h]
            lg = lgs[h] + madd
            if delta is not None:
                lg = lg + delta[h]
            m_new = jnp.maximum(m, jnp.max(lg, axis=1, keepdims=True))
            alpha = jnp.exp(m - m_new)
            p = jnp.exp(lg - m_new)
            mid.append((m_new, alpha, alpha * l + jnp.sum(p, axis=1, keepdims=True), p.astype(BF16)))
        out = []
        for h in range(B_HEADS):
            m_new, alpha, l, pb = mid[h]
            acc = alpha * states[h][2] + jnp.dot(pb, vts[h // B_REP], preferred_element_type=F32)
            out.append((m_new, l, acc))
        return tuple(out)

    def far_chunk(c, states):
        return attend(pl.multiple_of(c * KEY_CHUNK, KEY_CHUNK), KEY_CHUNK, states, None)

    n_far = j // 4
    states = lax.fori_loop(0, n_far, far_chunk, tuple(init_state() for _ in range(B_HEADS)))
    d0 = pl.multiple_of((3 - j % 4) * Q_TILE, Q_TILE)
    delta = [dl_ref[h, :, pl.ds(d0, TAIL_WIDTH)] for h in range(B_HEADS)]
    states = attend(pl.multiple_of(n_far * KEY_CHUNK, KEY_CHUNK), TAIL_WIDTH, states, delta)

    o_ref[...] = jnp.concatenate([acc / l for _, l, acc in states], axis=1)


def _dsa_prompt(q, qi, kw, kp, vp, kip, delta, n_qblk, topk):
    nkp = kp.shape[0]
    full = lambda a: pl.BlockSpec(a.shape, lambda j: (0,) * a.ndim)
    return pl.pallas_call(
        functools.partial(_dsa_prompt_kernel, topk=topk),
        grid=(n_qblk,),
        in_specs=[pl.BlockSpec((Q_TILE, B_WIDTH), lambda j: (j, 0)),
                  pl.BlockSpec((Q_TILE, IDX_HEADS * IDX_DIM), lambda j: (j, 0)),
                  pl.BlockSpec((Q_TILE, LANES), lambda j: (j, 0)),
                  full(kp), full(vp), full(kip), full(delta)],
        out_specs=pl.BlockSpec((Q_TILE, B_WIDTH), lambda j: (j, 0)),
        out_shape=jax.ShapeDtypeStruct((n_qblk * Q_TILE, B_WIDTH), F32),
        scratch_shapes=[pltpu.VMEM((Q_TILE, nkp), I32), pltpu.VMEM((Q_TILE, TOP_M * LANES), I32)],
        compiler_params=_cparams(("arbitrary",)),
        name="dsa_prompt",
    )(q, qi, kw, kp, vp, kip, delta)


SROWS = 8


def _dsa_sample_score_kernel(pt_ref, qi_ref, kw_ref, *refs, n_steps, pages, dec_seq):
    cki_refs, kin_ref, key_ref = refs[:pages], refs[pages], refs[pages + 1]
    p = pl.program_id(1)
    qi_b = qi_ref[0].astype(BF16)
    wi = kw_ref[0][:, IDX_DIM:IDX_DIM + IDX_HEADS] * IDX_SCALE
    t = lax.broadcasted_iota(I32, (SROWS, 1), 0)
    c = lax.broadcasted_iota(I32, (1, PAGE_SIZE), 1)

    @pl.when(p < n_steps)
    def _():
        ki_all = jnp.concatenate([r[0, 0].astype(BF16) for r in cki_refs], axis=1)
        s = _index_scores(qi_b, wi, ki_all, transposed=True)
        key_ref[0] = _sortable_key(jnp.where(t < dec_seq, s, -jnp.inf))

    @pl.when(p == n_steps)
    def _():
        s = _index_scores(qi_b, wi, kin_ref[0].astype(BF16), transposed=True)
        key_ref[0, :, 0:PAGE_SIZE] = _sortable_key(jnp.where((t < dec_seq) & (c <= t), s, -jnp.inf))
        if pages > 1:
            key_ref[0, :, PAGE_SIZE:] = jnp.full((SROWS, (pages - 1) * PAGE_SIZE), NEG_INF_KEY, I32)


def _dsa_sample_attend_kernel(pt_ref, q_ref, key_ref, *refs, n_steps, pages, topk):
    ck_refs, cv_refs = refs[:pages], refs[pages:2 * pages]
    kn_ref, vn_ref, dl_ref, o_ref, tau_scr, jl_scr, m_scr, l_scr, acc_scr = refs[2 * pages:]
    p = pl.program_id(1)
    width = pages * PAGE_SIZE

    @pl.when(p == 0)
    def _():
        keys = key_ref[0]
        aidx = lax.broadcasted_iota(I32, (1, keys.shape[1]), 1)

        def count_fn(cand, strict):
            hit = (keys > cand) if strict else (keys >= cand)
            return jnp.sum(jnp.where(hit, 1.0, 0.0), axis=1, keepdims=True)

        def count_tie_fn(tau, jl):
            return jnp.sum(jnp.where((keys == tau) & (aidx < jl), 1.0, 0.0), axis=1, keepdims=True)

        tau, jlim = _topk_threshold(count_fn, count_tie_fn, SROWS, topk)
        tau_scr[...] = tau
        jl_scr[...] = jlim
        m_scr[...] = jnp.full(m_scr.shape, NEG_BIG, F32)
        l_scr[...] = jnp.zeros(l_scr.shape, F32)
        acc_scr[...] = jnp.zeros(acc_scr.shape, F32)

    a0 = pl.multiple_of(p * width, width)
    kk = key_ref[0, :, pl.ds(a0, width)]
    aidx = a0 + lax.broadcasted_iota(I32, (1, width), 1)
    sel = _select(kk, aidx, tau_scr[...], jl_scr[...])
    sel3 = jnp.concatenate([sel] * B_REP, axis=0)
    q = q_ref[0] * (HEAD_DIM ** -0.5)

    def cached(ref):
        return lambda g: ref[0, 0, g]

    def fresh(ref):
        return lambda g: ref[0, g]

    def run(k_tiles, v_tiles, delta):
        n = len(k_tiles)
        sel_n = sel3[:, :n * PAGE_SIZE]
        for g in range(B_KV_HEADS):
            qg = jnp.concatenate([q[:, (g * B_REP + r) * HEAD_DIM:(g * B_REP + r + 1) * HEAD_DIM]
                                  for r in range(B_REP)], axis=0).astype(BF16)
            k_all = jnp.concatenate([kt(g).astype(BF16) for kt in k_tiles], axis=1)
            lg = jnp.dot(qg, k_all, preferred_element_type=F32)
            if delta is not None:
                pad = [jnp.zeros((lg.shape[0], (n - 1) * PAGE_SIZE), F32)] if n > 1 else []
                lg = lg + jnp.concatenate(pad + [delta[g]], axis=1)
            lg = jnp.where(sel_n, lg, NEG_BIG)
            m = m_scr[g]
            m_new = jnp.maximum(m, jnp.max(lg, axis=1, keepdims=True))
            alpha = jnp.exp(m - m_new)
            pr = jnp.where(sel_n, jnp.exp(lg - m_new), 0.0).astype(BF16)
            v_all = jnp.concatenate([vt(g).astype(BF16) for vt in v_tiles], axis=1)
            acc = alpha * acc_scr[g] + lax.dot_general(pr, v_all, NT_DIMS, preferred_element_type=F32)
            m_scr[g] = m_new
            l_scr[g] = alpha * l_scr[g] + jnp.sum(pr.astype(F32), axis=1, keepdims=True)
            acc_scr[g] = acc

    ck_tiles = [cached(r) for r in ck_refs]
    cv_tiles = [cached(r) for r in cv_refs]

    @pl.when(p < n_steps - 1)
    def _():
        run(ck_tiles, cv_tiles, None)

    @pl.when(p == n_steps - 1)
    def _():
        run(ck_tiles, cv_tiles, (dl_ref[0, 0], dl_ref[0, 1]))

    @pl.when(p == n_steps)
    def _():
        run([fresh(kn_ref)], [fresh(vn_ref)], (dl_ref[1, 0], dl_ref[1, 1]))
        pieces = [None] * B_HEADS
        for g in range(B_KV_HEADS):
            og = acc_scr[g] / l_scr[g]
            for r in range(B_REP):
                pieces[g * B_REP + r] = og[r * SROWS:(r + 1) * SROWS, :]
        o_ref[0] = jnp.concatenate(pieces, axis=1)


def _dsa_sample(page_table, q_s, qi_s, kw_s, cki, ck, cv, layer, ki_new, k_new, v_new, delta_s, dec_seq, topk):
    nb = q_s.shape[0]
    n_pages = page_table.shape[1]
    pages = math.gcd(n_pages, SAMPLE_PAGES)
    n_steps = n_pages // pages
    width = pages * PAGE_SIZE
    n_keys = (n_steps + 1) * width

    def page(i):
        return lambda b, p, pt: (layer, pt[b, jnp.minimum(p, n_steps - 1) * pages + i], 0, 0)

    def page_kv(i):
        return lambda b, p, pt: (layer, pt[b, jnp.minimum(p, n_steps - 1) * pages + i], 0, 0, 0)

    per_b = lambda b, p, pt: (b, 0, 0)
    keys = pl.pallas_call(
        functools.partial(_dsa_sample_score_kernel, n_steps=n_steps, pages=pages, dec_seq=dec_seq),
        grid_spec=pltpu.PrefetchScalarGridSpec(
            num_scalar_prefetch=1, grid=(nb, n_steps + 1),
            in_specs=[pl.BlockSpec((1, SROWS, IDX_HEADS * IDX_DIM), per_b),
                      pl.BlockSpec((1, SROWS, LANES), per_b)]
            + [pl.BlockSpec((1, 1, IDX_DIM, PAGE_SIZE), page(i)) for i in range(pages)]
            + [pl.BlockSpec((1, IDX_DIM, PAGE_SIZE), per_b)],
            out_specs=pl.BlockSpec((1, SROWS, width), lambda b, p, pt: (b, 0, p))),
        out_shape=jax.ShapeDtypeStruct((nb, SROWS, n_keys), I32),
        compiler_params=_cparams(("arbitrary", "arbitrary")),
        name="dsa_sample_score",
    )(page_table, qi_s, kw_s, *([cki] * pages), ki_new)
    grows = B_REP * SROWS
    return pl.pallas_call(
        functools.partial(_dsa_sample_attend_kernel, n_steps=n_steps, pages=pages, topk=topk),
        grid_spec=pltpu.PrefetchScalarGridSpec(
            num_scalar_prefetch=1, grid=(nb, n_steps + 1),
            in_specs=[pl.BlockSpec((1, SROWS, B_WIDTH), per_b),
                      pl.BlockSpec((1, SROWS, n_keys), per_b)]
            + [pl.BlockSpec((1, 1, B_KV_HEADS, HEAD_DIM, PAGE_SIZE), page_kv(i)) for i in range(pages)] * 2
            + [pl.BlockSpec((1, B_KV_HEADS, HEAD_DIM, PAGE_SIZE), lambda b, p, pt: (b, 0, 0, 0)),
               pl.BlockSpec((1, B_KV_HEADS, HEAD_DIM, PAGE_SIZE), lambda b, p, pt: (b, 0, 0, 0)),
               pl.BlockSpec(delta_s.shape, lambda b, p, pt: (0, 0, 0, 0))],
            out_specs=pl.BlockSpec((1, SROWS, B_WIDTH), per_b),
            scratch_shapes=[pltpu.VMEM((SROWS, 1), I32), pltpu.VMEM((SROWS, 1), I32),
                            pltpu.VMEM((B_KV_HEADS, grows, 1), F32), pltpu.VMEM((B_KV_HEADS, grows, 1), F32),
                            pltpu.VMEM((B_KV_HEADS, grows, HEAD_DIM), F32)]),
        out_shape=jax.ShapeDtypeStruct((nb, SROWS, B_WIDTH), F32),
        compiler_params=_cparams(("arbitrary", "arbitrary")),
        name="dsa_sample_attend",
    )(page_table, q_s, keys, *([ck] * pages), *([cv] * pages), k_new, v_new, delta_s)


def _pool_kernel(c_ref, halo_ref, w_ref, scale_ref, o_ref, *, block, offset):
    i = pl.program_id(1)
    c = c_ref[0]
    halo = jnp.where(i == 0, 0.0, halo_ref[0])
    full = jnp.concatenate([halo, c], axis=0)
    sums = [full]
    for sh in (1, 2, 4, 8):
        sums.append(sums[-1] + pltpu.roll(sums[-1], sh, axis=0))
    lane = lax.broadcasted_iota(I32, (block, C_WIDTH), 1)
    pos = i * block + lax.broadcasted_iota(I32, (block, C_WIDTH), 0) - offset
    win = jnp.zeros((block, C_WIDTH), F32)
    cnt = jnp.ones((block, C_WIDTH), F32)
    for gi, w in enumerate(POOL_WINDOWS):
        in_g = (lane >= gi * C_GROUP_DIM) & (lane < (gi + 1) * C_GROUP_DIM)
        win = jnp.where(in_g, sums[gi + 1][POOL_HALO:, :], win)
        cnt = jnp.where(in_g, jnp.clip(pos + 1, 1, w).astype(F32), cnt)
    d = win / cnt - c
    o_ref[0] = jnp.dot(d.astype(BF16), w_ref[...], preferred_element_type=F32) * scale_ref[...]


def _pool(c, w_bd, scale, block, offset):
    n_seq, length, _ = c.shape
    ratio = block // POOL_HALO
    return pl.pallas_call(
        functools.partial(_pool_kernel, block=block, offset=offset),
        grid=(n_seq, length // block),
        in_specs=[pl.BlockSpec((1, block, C_WIDTH), lambda s, i: (s, i, 0)),
                  pl.BlockSpec((1, POOL_HALO, C_WIDTH), lambda s, i: (s, jnp.maximum(i * ratio - 1, 0), 0)),
                  pl.BlockSpec((C_WIDTH, C_WIDTH), lambda s, i: (0, 0)),
                  pl.BlockSpec((1, C_WIDTH), lambda s, i: (0, 0))],
        out_specs=pl.BlockSpec((1, block, C_WIDTH), lambda s, i: (s, i, 0)),
        out_shape=jax.ShapeDtypeStruct(c.shape, F32),
        compiler_params=_cparams(("parallel", "parallel")),
        name="pool",
    )(c, c, w_bd, scale)


def _outproj_kernel(x_ref, oa_ref, ob_ref, oc_ref, wo_ref, g_ref, wr_ref, br_ref, xo_ref, h_ref, gm_ref):
    mix = jnp.dot(oa_ref[...].astype(BF16), wo_ref[0:A_WIDTH, :], preferred_element_type=F32)
    mix += jnp.dot(ob_ref[...].astype(BF16), wo_ref[A_WIDTH:A_WIDTH + B_WIDTH, :], preferred_element_type=F32)
    mix += jnp.dot(oc_ref[...].astype(BF16), wo_ref[A_WIDTH + B_WIDTH:, :], preferred_element_type=F32)
    x = x_ref[...] + mix
    xo_ref[...] = x
    h = _rms(x, g_ref[...])
    h_ref[...] = h.astype(BF16)
    logits = jnp.dot(h, wr_ref[...], precision=HIGHEST, preferred_element_type=F32) + br_ref[...]
    lane = lax.broadcasted_iota(I32, logits.shape, 1)
    work = logits
    vals, hots = [], []
    for _ in range(TOP_K):
        m = jnp.max(work, axis=1, keepdims=True)
        idx = jnp.min(jnp.where(work == m, lane, N_EXPERTS), axis=1, keepdims=True)
        hot = lane == idx
        vals.append(m)
        hots.append(hot)
        work = jnp.where(hot, -jnp.inf, work)
    es = [jnp.exp(v - vals[0]) for v in vals]
    den = es[0] + es[1] + es[2] + es[3]
    gm = jnp.zeros(logits.shape, F32)
    for e, hot in zip(es, hots):
        gm = jnp.where(hot, e / den, gm)
    gm_ref[...] = gm


def _outproj(x, oa, ob, oc, wo, g, wr, br):
    nt = x.shape[0]
    tile = lambda n: pl.BlockSpec((TOK_TILE, n), lambda i: (i, 0))
    full = lambda a: pl.BlockSpec(a.shape, lambda i: (0, 0))
    return pl.pallas_call(
        _outproj_kernel,
        grid=(nt // TOK_TILE,),
        in_specs=[tile(D_MODEL), tile(A_WIDTH), tile(B_WIDTH), tile(C_WIDTH), full(wo), full(g), full(wr), full(br)],
        out_specs=[tile(D_MODEL), tile(D_MODEL), tile(N_EXPERTS)],
        out_shape=[jax.ShapeDtypeStruct((nt, D_MODEL), F32), jax.ShapeDtypeStruct((nt, D_MODEL), BF16),
                   jax.ShapeDtypeStruct((nt, N_EXPERTS), F32)],
        compiler_params=_cparams(("parallel",)),
        name="outproj",
    )(x, oa, ob, oc, wo, g, wr, br)


MOE_GROUP = 1536
MOE_WINDOW = 240


def _moe_rank_kernel(gm_ref, su_ref, eye_ref, rank_ref, gate_ref, cnt_ref, carry_scr, *, tiles_per_group):
    i = pl.program_id(0)

    @pl.when(i % tiles_per_group == 0)
    def _():
        carry_scr[...] = jnp.zeros_like(carry_scr)

    gm = gm_ref[...]
    sel = jnp.where(gm > 0.0, 1.0, 0.0).astype(BF16)
    before = lax.dot_general(sel, su_ref[...], TN_DIMS, preferred_element_type=F32)
    sel_t = lax.dot_general(sel, eye_ref[...], TN_DIMS, preferred_element_type=F32)
    gate_ref[...] = _dot3(gm, eye_ref[...].astype(F32), TN_DIMS)
    carry = carry_scr[...]
    rank_ref[...] = jnp.where(sel_t > 0.5, carry[:, 0:1] + before, -1.0)
    carry = carry + jnp.sum(sel_t, axis=1, keepdims=True)
    carry_scr[...] = carry
    cnt_ref[0] = carry


def _moe_rank(gm):
    nt = gm.shape[0]
    tile = TOK_TILE
    tpg = MOE_GROUP // tile
    idx = np.arange(tile)
    su = jnp.asarray((idx[:, None] < idx[None, :]).astype(np.float32), BF16)
    eye = jnp.asarray(np.eye(tile, dtype=np.float32), BF16)
    return pl.pallas_call(
        functools.partial(_moe_rank_kernel, tiles_per_group=tpg),
        grid=(nt // tile,),
        in_specs=[pl.BlockSpec((tile, N_EXPERTS), lambda i: (i, 0)),
                  pl.BlockSpec((tile, tile), lambda i: (0, 0)), pl.BlockSpec((tile, tile), lambda i: (0, 0))],
        out_specs=[pl.BlockSpec((N_EXPERTS, tile), lambda i: (0, i)), pl.BlockSpec((N_EXPERTS, tile), lambda i: (0, i)),
                   pl.BlockSpec((1, N_EXPERTS, LANES), lambda i: (i // tpg, 0, 0))],
        out_shape=[jax.ShapeDtypeStruct((N_EXPERTS, nt), F32), jax.ShapeDtypeStruct((N_EXPERTS, nt), F32),
                   jax.ShapeDtypeStruct((nt // MOE_GROUP, N_EXPERTS, LANES), F32)],
        scratch_shapes=[pltpu.VMEM((N_EXPERTS, LANES), F32)],
        compiler_params=_cparams(("arbitrary",)),
        name="moe_rank",
    )(gm, su, eye)


def _one_hot_rows(rank_row, first_row):
    want = (first_row + lax.broadcasted_iota(I32, (MOE_WINDOW, 1), 0)).astype(F32)
    return jnp.where(rank_row == want, 1.0, 0.0)


def _moe_expert_kernel(ie_ref, ij_ref, io_ref, is_ref, iv_ref, h_ref, rank_ref, gate_ref, w1_ref, b1_ref, w2_ref,
                       b2_ref, y_ref, w1b_scr, w2b_scr):
    i = pl.program_id(0)

    @pl.when((i == 0) | (ie_ref[i] != ie_ref[jnp.maximum(i - 1, 0)]))
    def _():
        w1b_scr[...] = w1_ref[0, 0].astype(BF16)
        w2b_scr[...] = w2_ref[0, 0].astype(BF16)

    @pl.when(iv_ref[i] > 0)
    def _():
        hot = _one_hot_rows(rank_ref[0], is_ref[i] * MOE_WINDOW)
        rows = jnp.dot(hot.astype(BF16), h_ref[...], preferred_element_type=F32).astype(BF16)
        gate = jnp.sum(hot * gate_ref[0], axis=1, keepdims=True)
        u = jnp.dot(rows, w1b_scr[...], preferred_element_type=F32) + b1_ref[0, 0]
        glu = jnp.minimum(u[:, :D_FF], SWIGLU_LIMIT)
        lin = jnp.clip(u[:, D_FF:], -SWIGLU_LIMIT, SWIGLU_LIMIT)
        act = (glu * _sigmoid(SWIGLU_ALPHA * glu) * (lin + 1.0)).astype(BF16)
        out = jnp.dot(act, w2b_scr[...], preferred_element_type=F32) + b2_ref[0, 0]
        y_ref[...] = (out * gate).astype(BF16)


def _moe_combine_kernel(ie_ref, ij_ref, io_ref, is_ref, iv_ref, y_ref, rank_ref, x_ref, gf_ref, o_ref, acc_scr,
                        *, n_items, final):
    i = pl.program_id(0)
    valid = iv_ref[i] > 0
    nxt = jnp.minimum(i + 1, n_items - 1)

    @pl.when((i == 0) | (ij_ref[i] != ij_ref[jnp.maximum(i - 1, 0)]))
    def _():
        acc_scr[...] = jnp.zeros_like(acc_scr)

    @pl.when(valid)
    def _():
        hot = _one_hot_rows(rank_ref[0], is_ref[i] * MOE_WINDOW)
        acc_scr[...] += lax.dot_general(hot.astype(BF16), y_ref[...], TN_DIMS, preferred_element_type=F32)

    @pl.when(valid & ((i == n_items - 1) | (ij_ref[nxt] != ij_ref[i]) | (iv_ref[nxt] == 0)))
    def _():
        x = x_ref[...] + acc_scr[...]
        if final:
            x = _rms(x, gf_ref[...])
        o_ref[...] = x


def _moe_items(cnt, order_by_group):
    n_g, n_e = cnt.shape
    n_items = n_g * n_e + -(-(n_g * MOE_GROUP * TOP_K) // MOE_WINDOW)
    max_win = -(-MOE_GROUP // MOE_WINDOW)
    c = cnt.T
    n_win = (c + MOE_WINDOW - 1) // MOE_WINDOW
    seg_off = (jnp.cumsum(n_win.reshape(-1)) - n_win.reshape(-1)).reshape(n_e, n_g)
    e_id, g_id, s_id = jnp.meshgrid(jnp.arange(n_e), jnp.arange(n_g), jnp.arange(max_win), indexing="ij")
    ok = s_id < n_win[:, :, None]
    off = seg_off[:, :, None] + s_id
    if order_by_group:
        e_id, g_id, s_id, ok, off = (jnp.swapaxes(t, 0, 1) for t in (e_id, g_id, s_id, ok, off))
    flat = [t.reshape(-1).astype(I32) for t in (e_id, g_id, off, s_id, ok)]
    pick = jnp.argsort(1 - flat[4], stable=True)[:n_items]
    arrs = [t[pick] for t in flat]
    last = jnp.maximum(jnp.sum(arrs[4]) - 1, 0)
    e_a, g_a, o_a, s_a, v_a = arrs
    keep = v_a > 0
    return tuple(jnp.where(keep, t, t[last]) for t in (e_a, g_a, o_a, s_a)) + (v_a,), n_items


def _moe_routed(h, gm, x, w1, b1, w2, b2, gf, layer, final):
    nt = x.shape[0]
    n_g = nt // MOE_GROUP
    rank_t, gate_t, cnt3 = _moe_rank(gm)
    cnt = cnt3[:, :, 0].astype(I32)
    rank3 = rank_t[:, None, :]
    gate3 = gate_t[:, None, :]
    items_e, n_items = _moe_items(cnt, False)
    win_spec = pl.BlockSpec((MOE_WINDOW, D_MODEL), lambda i, ie, ij, io, is_, iv: (io[i], 0))
    per_eg = pl.BlockSpec((1, 1, MOE_GROUP), lambda i, ie, ij, io, is_, iv: (ie[i], 0, ij[i]))
    grp = lambda n: pl.BlockSpec((MOE_GROUP, n), lambda i, ie, ij, io, is_, iv: (ij[i], 0))
    wspec = lambda a, b: pl.BlockSpec((1, 1, a, b), lambda i, ie, ij, io, is_, iv: (layer, ie[i], 0, 0))
    y = pl.pallas_call(
        _moe_expert_kernel,
        grid_spec=pltpu.PrefetchScalarGridSpec(
            num_scalar_prefetch=5, grid=(n_items,),
            in_specs=[grp(D_MODEL), per_eg, per_eg, wspec(D_MODEL, 2 * D_FF), wspec(1, 2 * D_FF),
                      wspec(D_FF, D_MODEL), wspec(1, D_MODEL)],
            out_specs=win_spec,
            scratch_shapes=[pltpu.VMEM((D_MODEL, 2 * D_FF), BF16), pltpu.VMEM((D_FF, D_MODEL), BF16)]),
        out_shape=jax.ShapeDtypeStruct((n_items * MOE_WINDOW, D_MODEL), BF16),
        compiler_params=_cparams(("arbitrary",)),
        name="moe_expert",
    )(*items_e, h, rank3, gate3, w1, b1[:, :, None, :], w2, b2[:, :, None, :])
    items_g, _ = _moe_items(cnt, True)
    return pl.pallas_call(
        functools.partial(_moe_combine_kernel, n_items=n_items, final=final),
        grid_spec=pltpu.PrefetchScalarGridSpec(
            num_scalar_prefetch=5, grid=(n_items,),
            in_specs=[win_spec, per_eg, grp(D_MODEL),
                      pl.BlockSpec((1, D_MODEL), lambda i, ie, ij, io, is_, iv: (0, 0))],
            out_specs=grp(D_MODEL),
            scratch_shapes=[pltpu.VMEM((MOE_GROUP, D_MODEL), F32)]),
        out_shape=jax.ShapeDtypeStruct((nt, D_MODEL), F32),
        compiler_params=_cparams(("arbitrary",)),
        name="moe_combine",
    )(*items_g, y, rank3, x, gf)


def _rel_bucket(dist):
    n = jnp.maximum(dist, 0)
    max_exact = N_BUCKETS // 2
    log_ratio = jnp.log(jnp.maximum(n, 1).astype(jnp.float32) / max_exact) / math.log(MAX_DISTANCE / max_exact)
    large = jnp.minimum(max_exact + (log_ratio * (N_BUCKETS - max_exact)).astype(jnp.int32), N_BUCKETS - 1)
    return jnp.where(n < max_exact, n, large)


def _bias_delta(rel_bias, dist):
    rb = rel_bias.astype(F32)
    d = jnp.dot(jax.nn.one_hot(_rel_bucket(dist), N_BUCKETS, dtype=F32), rb, precision=HIGHEST) - rb[N_BUCKETS - 1]
    d = jnp.where(((dist >= 0) & (dist < MAX_DISTANCE))[..., None], d, 0.0)
    return jnp.moveaxis(d, -1, 0)


def _group_rows(delta, rows):
    return delta.reshape(B_KV_HEADS, B_REP * rows, delta.shape[-1])


def kernel(x_prompt, x_sample, cache_k, cache_v, cache_kidx, state_wkv, state_shift, state_pool, page_table,
           meta_tokens, rel_bias, norm_mix, norm_ffn, norm_final, w_in, w_out,
           a_mu, a_w0, a_w2, a_a0, a_a2, a_g2, a_kk, a_ka, a_rk, a_ln_w, a_ln_b,
           c_w, c_scale, moe_w_router, moe_b_router, moe_w1, moe_b1, moe_w2, moe_b2):
    assert x_prompt.shape[0] == 1
    depth = w_in.shape[0]
    seq = x_prompt.shape[1]
    nb, dec_seq = x_sample.shape[:2]
    assert dec_seq <= SROWS
    t_p = seq + N_META
    n_s = nb * dec_seq
    n_qblk = -(-t_p // Q_TILE)
    nkp = KEY_CHUNK * ((n_qblk + 4) // 4 + 1)
    assert MOE_GROUP % TOK_TILE == 0
    nt = _round_up(max(t_p + n_s, Q_TILE * n_qblk), MOE_GROUP)
    n_pages = page_table.shape[1]
    past = n_pages * PAGE_SIZE
    topk_p = min(TOPK_MAX, seq // 4)
    topk_s = min(TOPK_MAX, (past + dec_seq) // 4)
    n_phys = cache_k.shape[1]

    x = jnp.concatenate([meta_tokens.astype(F32), x_prompt[0], x_sample.reshape(n_s, D_MODEL),
                         jnp.zeros((nt - t_p - n_s, D_MODEL), F32)], axis=0)

    ones_blk = jnp.kron(jnp.eye(A_HEADS, dtype=F32), jnp.ones((HEAD_DIM, HEAD_DIM), F32))
    qi_ = jnp.arange(Q_TILE)[:, None]
    cj_ = jnp.arange(2 * Q_TILE)[None, :]
    side = TAIL_WIDTH - 2 * Q_TILE
    delta_p = jnp.pad(_bias_delta(rel_bias, Q_TILE + qi_ - cj_), ((0, 0), (0, 0), (side, side)))
    ts_ = jnp.arange(SROWS)[:, None]
    cs_ = jnp.arange(PAGE_SIZE)[None, :]
    delta_s = jnp.stack([_group_rows(_bias_delta(rel_bias, ts_ + PAGE_SIZE - cs_), SROWS),
                         _group_rows(_bias_delta(rel_bias, ts_ - cs_), SROWS)])

    outs = {n: [] for n in ("kp", "vp", "kip", "wkvp", "shp", "plp", "ks", "vs", "kis", "wkvs", "shs", "pls")}
    for l in range(depth):
        wl = w_in[l]
        b0 = A_PROJ
        w_cat = jnp.concatenate([
            wl[:, :A_PROJ], wl[:, b0:b0 + B_WIDTH + 2 * KV_WIDTH + IDX_HEADS * IDX_DIM],
            wl[:, b0 + B_PROJ - IDX_DIM - IDX_HEADS:b0 + B_PROJ],
            jnp.zeros((D_MODEL, LANES - IDX_DIM - IDX_HEADS), F32), wl[:, b0 + B_PROJ:]], axis=1).astype(BF16)
        pa, q, k, v, qi, kw, pc = _inproj(x, norm_mix[l][None], w_cat)

        sl_p = slice(0, t_p)
        sl_s = slice(t_p, t_p + n_s)
        outs["kp"].append(k[sl_p].reshape(1, t_p, B_KV_HEADS, HEAD_DIM))
        outs["vp"].append(v[sl_p].reshape(1, t_p, B_KV_HEADS, HEAD_DIM))
        outs["kip"].append(kw[sl_p, :IDX_DIM][None])
        outs["ks"].append(k[sl_s].reshape(nb, dec_seq, B_KV_HEADS, HEAD_DIM))
        outs["vs"].append(v[sl_s].reshape(nb, dec_seq, B_KV_HEADS, HEAD_DIM))
        outs["kis"].append(kw[sl_s, :IDX_DIM].reshape(nb, dec_seq, IDX_DIM))
        outs["shp"].append(pa[t_p - 1:t_p])
        pa_s = pa[sl_s].reshape(nb, dec_seq, A_PROJ)
        outs["shs"].append(pa_s[:, -1])
        pc_s = pc[sl_s].reshape(nb, dec_seq, C_WIDTH)
        outs["plp"].append(pc[t_p - POOL_BUF:t_p][None])
        pool_full = jnp.concatenate([state_pool[l], pc_s], axis=1)
        outs["pls"].append(pool_full[:, -POOL_BUF:])

        rw = (a_mu[l], a_w0[l], a_w2[l], a_a0[l], a_a2[l], a_g2[l], a_kk[l], a_ka[l], a_rk[l], a_ln_w[l], a_ln_b[l])
        oa_p, st_p = _rwkv_chunked(pa[None], jnp.zeros((1, 1, A_PROJ), F32),
                                   jnp.zeros((1, A_HEADS, HEAD_DIM, HEAD_DIM), F32),
                                   rw, ones_blk, t_p, RWKV_BLOCK, RWKV_CHUNK)
        pa_s8 = jnp.pad(pa_s, ((0, 0), (0, SROWS - dec_seq), (0, 0)))
        oa_s, st_s = _rwkv_chunked(pa_s8, state_shift[l][:, None, :], state_wkv[l].astype(F32),
                                   rw, ones_blk, dec_seq, SROWS, SROWS)
        outs["wkvp"].append(st_p.astype(state_wkv.dtype))
        outs["wkvs"].append(st_s.astype(state_wkv.dtype))

        n_real = min(nt, nkp - KEY_FRONT_PAD)
        front = lambda a: jnp.pad(a[:n_real].astype(BF16), ((KEY_FRONT_PAD, nkp - KEY_FRONT_PAD - n_real), (0, 0)))
        ob_p = _dsa_prompt(q, qi, kw, front(k), front(v), front(kw[:, :IDX_DIM]), delta_p, n_qblk, topk_p)
        rows8 = lambda a: jnp.pad(a[sl_s].reshape(nb, dec_seq, -1), ((0, 0), (0, SROWS - dec_seq), (0, 0)))
        page8 = lambda a: jnp.pad(a[sl_s].reshape(nb, dec_seq, -1), ((0, 0), (0, PAGE_SIZE - dec_seq), (0, 0)))
        kv_page = lambda a: page8(a).reshape(nb, PAGE_SIZE, B_KV_HEADS, HEAD_DIM).transpose(0, 2, 3, 1)
        ob_s = _dsa_sample(page_table, rows8(q), rows8(qi), rows8(kw),
                           cache_kidx.transpose(0, 1, 3, 2), cache_k.transpose(0, 1, 3, 4, 2),
                           cache_v.transpose(0, 1, 3, 4, 2), l, page8(kw[:, :IDX_DIM]).transpose(0, 2, 1),
                           kv_page(k), kv_page(v), delta_s, dec_seq, topk_s)

        w_bd = jax.scipy.linalg.block_diag(*[c_w[l][gi] for gi in range(len(POOL_WINDOWS))]).astype(BF16)
        oc_p = _pool(pc[None], w_bd, c_scale[l][None], TOK_TILE, 0)
        pool_in = jnp.concatenate([jnp.zeros((nb, 1, C_WIDTH), F32), pool_full,
                                   jnp.zeros((nb, 2 * POOL_HALO - 1 - POOL_BUF - dec_seq, C_WIDTH), F32)], axis=1)
        oc_s = _pool(pool_in, w_bd, c_scale[l][None], POOL_HALO, 1)[:, POOL_HALO:POOL_HALO + dec_seq]

        def merge(p_rows, s_rows):
            w = p_rows.shape[-1]
            return jnp.concatenate([p_rows[:t_p], s_rows.reshape(n_s, w), jnp.zeros((nt - t_p - n_s, w), F32)], axis=0)

        oa = merge(oa_p[0], oa_s[:, :dec_seq])
        ob = merge(ob_p, ob_s[:, :dec_seq])
        oc = merge(oc_p[0], oc_s)
        x, h2, gm = _outproj(x, oa, ob, oc, w_out[l].astype(BF16), norm_ffn[l][None], moe_w_router[l],
                             moe_b_router[l][None])
        x = _moe_routed(h2, gm, x, moe_w1, moe_b1, moe_w2, moe_b2, norm_final[None], l, l == depth - 1)

    y_prompt = x[N_META:t_p][None]
    y_sample = x[t_p:t_p + n_s].reshape(nb, dec_seq, D_MODEL)
    st = lambda n: jnp.stack(outs[n])
    return (y_prompt, y_sample, st("kp"), st("vp"), st("kip"), st("wkvp"), st("shp"), st("plp"),
            st("ks"), st("vs"), st("kis"), st("wkvs"), st("shs"), st("pls"))
```

```python
import functools
import math

import jax
import jax.numpy as jnp
import numpy as np
from jax import lax
from jax.experimental import pallas as pl
from jax.experimental.pallas import tpu as pltpu

F32 = jnp.float32
BF16 = jnp.bfloat16
I32 = jnp.int32
HIGHEST = lax.Precision.HIGHEST

D_MODEL = 1024
N_META = 16
HEAD_DIM = 64
A_WIDTH = 384
A_HEADS = 6
LORA_W, LORA_A, LORA_G = 64, 64, 128
A_PROJ = 3 * A_WIDTH + LORA_W + LORA_A + LORA_G
GN_EPS = 64e-5
B_WIDTH = 384
B_HEADS = 6
B_KV_HEADS = 2
B_REP = 3
KV_WIDTH = 128
IDX_HEADS = 4
IDX_DIM = 64
IDX_SCALE = (IDX_HEADS * IDX_DIM) ** -0.5
TOPK_MAX = 256
B_PROJ = B_WIDTH + 2 * KV_WIDTH + IDX_HEADS * IDX_DIM + IDX_DIM + IDX_HEADS
C_WIDTH = 256
C_GROUP_DIM = 64
POOL_WINDOWS = (2, 4, 8, 16)
POOL_BUF = 15
N_BUCKETS = 32
MAX_DISTANCE = 128
N_EXPERTS = 32
TOP_K = 4
D_FF = 1024
SWIGLU_LIMIT = 7.0
SWIGLU_ALPHA = 1.702
RMS_EPS = 1e-5
PAGE_SIZE = 128

LANES = 128
SUBLANES = 8
Q_TILE = 128
KEY_CHUNK = 512
KEY_FRONT_PAD = 128
TAIL_WIDTH = 640
FAR_WIDE = 4
TOP_M = 12
SAMPLE_PAGES = 16
TOK_TILE = 512
RWKV_BLOCK = 128
RWKV_CHUNK = 64
POOL_HALO = 16
VMEM_LIMIT = 56 * 1024 * 1024

INT_MIN = -(2 ** 31)
NEG_INF_KEY = np.int32(np.uint32(0xFF800000) ^ np.uint32(0x7FFFFFFF))
NEG_BIG = -1e30


def _round_up(x, m):
    return (x + m - 1) // m * m


def _cparams(sem):
    return pltpu.CompilerParams(dimension_semantics=sem, vmem_limit_bytes=VMEM_LIMIT)


def _rms(x, g):
    return x * lax.rsqrt(jnp.mean(x * x, axis=-1, keepdims=True) + RMS_EPS) * g


def _sigmoid(x):
    return 1.0 / (1.0 + jnp.exp(-x))


IN_COLS = (A_PROJ, B_WIDTH, KV_WIDTH, KV_WIDTH, IDX_HEADS * IDX_DIM, LANES, C_WIDTH)


def _inproj_kernel(x_ref, g_ref, w_ref, *out_refs):
    hb = _rms(x_ref[...], g_ref[...]).astype(BF16)
    off = 0
    for ref in out_refs:
        n = ref.shape[1]
        ref[...] = jnp.dot(hb, w_ref[:, off:off + n], preferred_element_type=F32)
        off += n


def _inproj(x, g, w):
    nt = x.shape[0]
    return pl.pallas_call(
        _inproj_kernel,
        grid=(nt // TOK_TILE,),
        in_specs=[pl.BlockSpec((TOK_TILE, D_MODEL), lambda i: (i, 0)),
                  pl.BlockSpec((1, D_MODEL), lambda i: (0, 0)),
                  pl.BlockSpec(w.shape, lambda i: (0, 0))],
        out_specs=[pl.BlockSpec((TOK_TILE, n), lambda i: (i, 0)) for n in IN_COLS],
        out_shape=[jax.ShapeDtypeStruct((nt, n), F32) for n in IN_COLS],
        compiler_params=_cparams(("parallel",)),
        name="inproj",
    )(x, g, w)


NN_DIMS = (((1,), (0,)), ((), ()))
NT_DIMS = (((1,), (1,)), ((), ()))
TN_DIMS = (((0,), (0,)), ((), ()))


def _dot3(a, b, dims=NN_DIMS):
    a_hi = a.astype(BF16)
    b_hi = b.astype(BF16)
    a_lo = (a - a_hi.astype(F32)).astype(BF16)
    b_lo = (b - b_hi.astype(F32)).astype(BF16)
    dg = functools.partial(lax.dot_general, dimension_numbers=dims, preferred_element_type=F32)
    return dg(a_hi, b_hi) + (dg(a_hi, b_lo) + dg(a_lo, b_hi))


def _rwkv_chunk_kernel(pa_ref, sh_ref, s0_ref, mu_ref, w0_ref, w2_ref, a0_ref, a2_ref, g2_ref, kk_ref, ka_ref,
                       rk_ref, lnw_ref, lnb_ref, ones_ref, tri_ref, o_ref, s_out_ref, carry_scr, s_scr,
                       *, seq_len, block, chunk):
    blk = pl.program_id(1)

    @pl.when(blk == 0)
    def _():
        carry_scr[...] = sh_ref[0]
        s_scr[...] = s0_ref[0]

    pa = pa_ref[0]
    row = lax.broadcasted_iota(I32, (block, 1), 0)
    prev = jnp.where(row == 0, carry_scr[...], pltpu.roll(pa, 1, axis=0))
    carry_scr[...] = pa[block - 1:block, :]
    xm = pa + (prev - pa) * mu_ref[...]
    r = xm[:, 0:A_WIDTH]
    k = xm[:, A_WIDTH:2 * A_WIDTH]
    v = xm[:, 2 * A_WIDTH:3 * A_WIDTH]
    lw = xm[:, 3 * A_WIDTH:3 * A_WIDTH + LORA_W]
    la = xm[:, 3 * A_WIDTH + LORA_W:3 * A_WIDTH + LORA_W + LORA_A]
    lg = xm[:, 3 * A_WIDTH + LORA_W + LORA_A:]

    def dot_hi(a, b):
        return jnp.dot(a, b, precision=HIGHEST, preferred_element_type=F32)

    z = -(w0_ref[...] + dot_hi(jnp.tanh(lw), w2_ref[...]))
    softplus = jnp.maximum(z, 0.0) + jnp.log(1.0 + jnp.exp(-jnp.abs(z)))
    log_w = -jnp.exp(-softplus - 0.5)
    a = _sigmoid(a0_ref[...] + dot_hi(la, a2_ref[...]))
    g = dot_hi(_sigmoid(lg), g2_ref[...])
    ones_blk = ones_ref[...]
    kk = k * kk_ref[...]
    kk = kk / jnp.maximum(jnp.sqrt(dot_hi(kk * kk, ones_blk)), 1e-12)
    k2 = k * (1.0 + (a - 1.0) * ka_ref[...])
    kka = kk * a

    valid = (blk * block + row) < seq_len
    log_w = jnp.where(valid, log_w, 0.0)
    kk_m, kka_m, k2_m, r_m, v_m = (jnp.where(valid, t, 0.0) for t in (kk, kka, k2, r, v))

    cum = dot_hi(tri_ref[0], log_w)
    cum_end = dot_hi(tri_ref[1], log_w)
    kap = kk_m * jnp.exp(cum - log_w)
    rt = r_m * jnp.exp(cum)
    inv_p = jnp.exp(-cum)
    kt = k2_m * inv_p
    bt = kka_m * inv_p
    to_end = jnp.exp(cum_end - cum)
    k_end = k2_m * to_end
    b_end = kka_m * to_end
    p_end = jnp.exp(cum_end)

    sub = min(16, chunk)
    ti = lax.broadcasted_iota(I32, (chunk, chunk), 0)
    tj = lax.broadcasted_iota(I32, (chunk, chunk), 1)
    heads = range(A_HEADS)
    o_chunks = []
    for c in range(block // chunk):
        rs = slice(c * chunk, (c + 1) * chunk)
        hsl = lambda t, h: t[rs, h * HEAD_DIM:(h + 1) * HEAD_DIM]
        lhs = [jnp.concatenate([hsl(kap, h), hsl(rt, h)], axis=0) for h in heads]
        rhs = [jnp.concatenate([hsl(kt, h), hsl(bt, h)], axis=0) for h in heads]
        vh = [hsl(v_m, h) for h in heads]
        gram = [_dot3(lhs[h], rhs[h], NT_DIMS) for h in heads]
        from_s = [_dot3(lhs[h], s_scr[h], NT_DIMS) for h in heads]
        a1 = [jnp.where(ti > tj, gram[h][:chunk, :chunk], 0.0) for h in heads]
        a2 = [jnp.where(ti > tj, gram[h][:chunk, chunk:], 0.0) for h in heads]
        a3 = [jnp.where(ti >= tj, gram[h][chunk:, :chunk], 0.0) for h in heads]
        a4 = [jnp.where(ti >= tj, gram[h][chunk:, chunk:], 0.0) for h in heads]
        u = [from_s[h][:chunk] + _dot3(a1[h], vh[h]) for h in heads]
        zs = [[] for _ in heads]
        for jb in range(chunk // sub):
            js = slice(jb * sub, (jb + 1) * sub)
            zj = []
            for h in heads:
                t = u[h][js]
                if jb > 0:
                    t = t - _dot3(a2[h][js, :jb * sub], jnp.concatenate(zs[h], axis=0))
                zj.append(t)
            for i in range(sub - 1):
                for h in heads:
                    col = jnp.broadcast_to(a2[h][js, jb * sub + i:jb * sub + i + 1], (sub, HEAD_DIM))
                    zj[h] = zj[h] - col * zj[h][i:i + 1, :]
            for h in heads:
                zs[h].append(zj[h])
        zh = [jnp.concatenate(zs[h], axis=0) for h in heads]
        vz = [jnp.concatenate([vh[h], zh[h]], axis=0) for h in heads]
        o_chunks.append(jnp.concatenate(
            [from_s[h][chunk:] + _dot3(jnp.concatenate([a3[h], -a4[h]], axis=1), vz[h]) for h in heads], axis=1))
        for h in heads:
            ends = jnp.concatenate([hsl(k_end, h), -hsl(b_end, h)], axis=0)
            s_scr[h] = s_scr[h] * hsl(p_end, h)[0:1, :] + _dot3(vz[h], ends, TN_DIMS)
    s_out_ref[0] = s_scr[...]

    o = jnp.concatenate(o_chunks, axis=0) if len(o_chunks) > 1 else o_chunks[0]
    inv = 1.0 / HEAD_DIM
    mean = dot_hi(o, ones_blk) * inv
    cen = o - mean
    var = dot_hi(cen * cen, ones_blk) * inv
    o = cen * lax.rsqrt(var + GN_EPS) * lnw_ref[...] + lnb_ref[...]
    bonus = dot_hi(r * k2 * rk_ref[...], ones_blk) * v
    o_ref[0] = (o + bonus) * g


def _rwkv_chunked(pa, shift_prev, s0, params, ones_blk, seq_len, block, chunk):
    n_seq, length, _ = pa.shape
    n_blk = -(-seq_len // block)
    idx = np.arange(block)
    same = (idx[:, None] // chunk) == (idx[None, :] // chunk)
    tri = jnp.asarray(np.stack([same & (idx[None, :] <= idx[:, None]), same]).astype(np.float32))
    row_spec = lambda n: pl.BlockSpec((1, n), lambda s, i: (0, 0))
    mat_spec = lambda a: pl.BlockSpec(a.shape, lambda s, i: (0,) * a.ndim)
    mu, w0, w2, a0, a2, g2, k_k, k_a, r_k, ln_w, ln_b = params
    kern = functools.partial(_rwkv_chunk_kernel, seq_len=seq_len, block=block, chunk=chunk)
    state_spec = pl.BlockSpec((1, A_HEADS, HEAD_DIM, HEAD_DIM), lambda s, i: (s, 0, 0, 0))
    return pl.pallas_call(
        kern,
        grid=(n_seq, n_blk),
        in_specs=[pl.BlockSpec((1, block, A_PROJ), lambda s, i: (s, i, 0)),
                  pl.BlockSpec((1, 1, A_PROJ), lambda s, i: (s, 0, 0)),
                  state_spec,
                  row_spec(A_PROJ), row_spec(A_WIDTH), mat_spec(w2), row_spec(A_WIDTH), mat_spec(a2),
                  mat_spec(g2), row_spec(A_WIDTH), row_spec(A_WIDTH), row_spec(A_WIDTH), row_spec(A_WIDTH),
                  row_spec(A_WIDTH), mat_spec(ones_blk), mat_spec(tri)],
        out_specs=[pl.BlockSpec((1, block, A_WIDTH), lambda s, i: (s, i, 0)), state_spec],
        out_shape=[jax.ShapeDtypeStruct((n_seq, length, A_WIDTH), F32),
                   jax.ShapeDtypeStruct((n_seq, A_HEADS, HEAD_DIM, HEAD_DIM), F32)],
        scratch_shapes=[pltpu.VMEM((1, A_PROJ), F32), pltpu.VMEM((A_HEADS, HEAD_DIM, HEAD_DIM), F32)],
        compiler_params=_cparams(("arbitrary", "arbitrary")),
        name="rwkv",
    )(pa, shift_prev, s0, mu[None], w0[None], w2, a0[None], a2, g2, k_k[None], k_a[None], r_k[None],
      ln_w[None], ln_b[None], ones_blk, tri)


def _sortable_key(s):
    bits = pltpu.bitcast(s + 0.0, I32)
    return bits ^ ((bits >> 31) & jnp.int32(0x7FFFFFFF))


def _index_scores(qi_b, wi, ki_b, transposed=False):
    s = None
    for h in range(IDX_HEADS):
        d = lax.dot_general(qi_b[:, h * IDX_DIM:(h + 1) * IDX_DIM], ki_b,
                            (((1,), (0 if transposed else 1,)), ((), ())), preferred_element_type=F32)
        t = jnp.maximum(d, 0.0) * wi[:, h:h + 1]
        s = t if s is None else s + t
    return s


def _topk_threshold(count_fn, count_tie_fn, rows, topk):
    kf = jnp.float32(topk)
    c0 = count_fn(jnp.zeros((rows, 1), I32), False)
    prefix0 = jnp.where(c0 >= kf, jnp.int32(0), jnp.int32(INT_MIN))

    def bit_step(i, prefix):
        cand = prefix | lax.shift_left(jnp.int32(1), jnp.int32(30) - i)
        return jnp.where(count_fn(cand, False) >= kf, cand, prefix)

    tau = lax.fori_loop(0, 31, bit_step, prefix0)
    c_gt = count_fn(tau, True)
    c_ge = count_fn(tau, False)
    need = kf - c_gt
    excess = jnp.max(c_ge - c_gt - need)

    def tie_search(_):
        def jbit(i, jl):
            cand = jl | lax.shift_left(jnp.int32(1), jnp.int32(14) - i)
            return jnp.where(count_tie_fn(tau, cand) < need, cand, jl)
        return lax.fori_loop(0, 15, jbit, jnp.zeros((rows, 1), I32))

    jlim = lax.cond(excess > 0.0, tie_search, lambda _: jnp.full((rows, 1), 2 ** 30, I32), 0)
    return tau, jlim


def _select(keys, aidx, tau, jlim):
    return ((keys > tau) | ((keys == tau) & (aidx <= jlim))) & (keys > jnp.int32(NEG_INF_KEY))


def _softmax_step(state, qg, kt, vt, sel_rows, delta):
    m, l, acc = state
    lg = lax.dot_general(qg, kt, (((1,), (1,)), ((), ())), preferred_element_type=F32)
    if delta is not None:
        lg = lg + delta
    lg = jnp.where(sel_rows, lg, NEG_BIG)
    m_new = jnp.maximum(m, jnp.max(lg, axis=1, keepdims=True))
    alpha = jnp.exp(m - m_new)
    p = jnp.where(sel_rows, jnp.exp(lg - m_new), 0.0)
    l = alpha * l + jnp.sum(p, axis=1, keepdims=True)
    acc = alpha * acc + jnp.dot(p.astype(BF16), vt, preferred_element_type=F32)
    return m_new, l, acc


def _dsa_prompt_kernel(q_ref, qi_ref, kw_ref, kp_ref, vp_ref, kip_ref, dl_ref, o_ref, key_scr, cand_scr, *, topk):
    j = pl.program_id(0)
    rows = Q_TILE
    n_chunks = (j + 5) // 4
    qi_b = qi_ref[...].astype(BF16)
    wi = kw_ref[:, IDX_DIM:IDX_DIM + IDX_HEADS] * IDX_SCALE
    qpos = j * Q_TILE + lax.broadcasted_iota(I32, (rows, 1), 0)
    lane_c = lax.broadcasted_iota(I32, (1, KEY_CHUNK), 1)

    def score_chunk(c, masked):
        a0 = pl.multiple_of(c * KEY_CHUNK, KEY_CHUNK)
        s = _index_scores(qi_b, wi, kip_ref[pl.ds(a0, KEY_CHUNK), :])
        if masked:
            kpos = a0 - KEY_FRONT_PAD + lane_c
            s = jnp.where((kpos <= qpos) & (kpos >= 0), s, -jnp.inf)
        key_scr[:, pl.ds(a0, KEY_CHUNK)] = _sortable_key(s)

    n_plain = jnp.clip((j * Q_TILE + KEY_FRONT_PAD - KEY_CHUNK) // KEY_CHUNK, 0, n_chunks - 1)
    score_chunk(0, True)
    lax.fori_loop(1, 1 + n_plain, lambda c, x: (score_chunk(c, False), x)[1], 0)
    lax.fori_loop(1 + n_plain, n_chunks, lambda c, x: (score_chunk(c, True), x)[1], 0)

    key_scr[:, pl.ds(pl.multiple_of(n_chunks * KEY_CHUNK, KEY_CHUNK), KEY_CHUNK)] = jnp.full(
        (rows, KEY_CHUNK), NEG_INF_KEY, I32)

    n_fold = KEY_CHUNK // LANES

    def chunk_at(ref, c):
        return ref[:, pl.ds(pl.multiple_of(c * KEY_CHUNK, KEY_CHUNK), KEY_CHUNK)]

    def count32(cand, strict):
        def body(c, acc):
            kk = chunk_at(key_scr, c)
            hit = (kk > cand) if strict else (kk >= cand)
            m = jnp.where(hit, 1.0, 0.0)
            for i in range(n_fold):
                acc = acc + m[:, i * LANES:(i + 1) * LANES]
            return acc
        acc = lax.fori_loop(0, n_chunks, body, jnp.zeros((rows, LANES), F32))
        return jnp.sum(acc, axis=1, keepdims=True)

    kf = jnp.float32(topk)

    def kth_largest(count_ge):
        def bit_step(i, prefix):
            cand = prefix | lax.shift_left(jnp.int32(1), jnp.int32(30) - i)
            return jnp.where(count_ge(cand) >= kf, cand, prefix)
        nonneg = count_ge(jnp.zeros((rows, 1), I32)) >= kf
        return lax.fori_loop(0, 31, bit_step, jnp.where(nonneg, jnp.int32(0), jnp.int32(INT_MIN)))

    pair = 2 * SUBLANES
    for rp in range(rows // pair):
        r0 = rp * pair

        def insert_chunk(c, lists, r0=r0):
            a0 = pl.multiple_of(c * KEY_CHUNK, KEY_CHUNK)
            blk = key_scr[r0:r0 + pair, pl.ds(a0, KEY_CHUNK)]
            blk = pltpu.bitcast(blk ^ ((blk >> 31) & jnp.int32(0x7FFFFFFF)), F32)
            lists = list(lists)
            for half in range(2):
                for t in range(n_fold):
                    x = blk[half * SUBLANES:(half + 1) * SUBLANES, t * LANES:(t + 1) * LANES]
                    for i in range(TOP_M):
                        a = lists[half * TOP_M + i]
                        lists[half * TOP_M + i] = jnp.maximum(a, x)
                        x = jnp.minimum(a, x)
            return tuple(lists)

        lists = lax.fori_loop(0, n_chunks, insert_chunk,
                              tuple(jnp.full((SUBLANES, LANES), -jnp.inf, F32) for _ in range(2 * TOP_M)))
        for half in range(2):
            for i in range(TOP_M):
                cand_scr[r0 + half * SUBLANES:r0 + (half + 1) * SUBLANES, i * LANES:(i + 1) * LANES] = _sortable_key(
                    lists[half * TOP_M + i])

    def count_cand(cand):
        acc = jnp.zeros((rows, LANES), F32)
        for i in range(TOP_M):
            acc = acc + jnp.where(cand_scr[:, i * LANES:(i + 1) * LANES] >= cand, 1.0, 0.0)
        return jnp.sum(acc, axis=1, keepdims=True)

    tau_c = kth_largest(count_cand)
    gt_c = count32(tau_c, True)
    ge_c = count32(tau_c, False)
    wrong = jnp.max(jnp.where((gt_c < kf) & (ge_c >= kf), 0.0, 1.0))

    def full_search(_):
        t = kth_largest(lambda cand: count32(cand, False))
        return t, count32(t, True), count32(t, False)

    tau, c_gt, c_ge = lax.cond(wrong > 0.0, full_search, lambda _: (tau_c, gt_c, ge_c), 0)
    need = kf - c_gt
    excess = jnp.max(c_ge - c_gt - need)

    def tie_pass(_):
        ri = lax.broadcasted_iota(I32, (LANES, 2 * LANES), 0)
        ci = lax.broadcasted_iota(I32, (LANES, 2 * LANES), 1)
        tri = jnp.where((ri <= ci) | (ci >= LANES), 1.0, 0.0).astype(BF16)

        def body(c, carry):
            seen, jacc = carry
            a0 = pl.multiple_of(c * KEY_CHUNK, KEY_CHUNK)
            kk = key_scr[:, pl.ds(a0, KEY_CHUNK)]
            ties = [kk[:, i * LANES:(i + 1) * LANES] == tau for i in range(n_fold)]
            pres = [jnp.dot(jnp.where(t, 1.0, 0.0).astype(BF16), tri, preferred_element_type=F32) for t in ties]
            for i in range(n_fold):
                rank = jnp.where(ties[i], seen + pres[i][:, :LANES], 1e9)
                aidx = a0 + i * LANES + lax.broadcasted_iota(I32, (1, LANES), 1)
                jacc = jnp.maximum(jacc, jnp.where(rank <= need, aidx, -1))
                seen = seen + pres[i][:, LANES:]
            return seen, jacc
        _, jacc = lax.fori_loop(0, n_chunks, body, (jnp.zeros((rows, LANES), F32), jnp.full((rows, LANES), -1, I32)))
        return jnp.max(jacc, axis=1, keepdims=True)

    jlim = lax.cond(excess > 0.0, tie_pass, lambda _: jnp.full((rows, 1), 2 ** 30, I32), 0)
    jlim = jnp.where(tau == jnp.int32(NEG_INF_KEY), jnp.int32(-1), jlim)
    tau_b = jnp.broadcast_to(tau, (rows, LANES))
    tau_m1_b = tau_b - 1

    q = q_ref[...] * (HEAD_DIM ** -0.5)
    qh = [q[:, h * HEAD_DIM:(h + 1) * HEAD_DIM].astype(BF16) for h in range(B_HEADS)]

    def init_state():
        return (jnp.full((rows, 1), NEG_BIG, F32), jnp.zeros((rows, 1), F32), jnp.zeros((rows, HEAD_DIM), F32))

    def attend(a0, width, states, delta):
        kk = key_scr[:, pl.ds(a0, width)]
        aidx = a0 + lax.broadcasted_iota(I32, (1, width), 1)
        thr = jnp.where(aidx <= jlim, jnp.concatenate([tau_m1_b] * (width // LANES), axis=1),
                        jnp.concatenate([tau_b] * (width // LANES), axis=1))
        madd = jnp.where(kk > thr, 0.0, NEG_BIG)
        kts = [kp_ref[pl.ds(a0, width), g * HEAD_DIM:(g + 1) * HEAD_DIM] for g in range(B_KV_HEADS)]
        vts = [vp_ref[pl.ds(a0, width), g * HEAD_DIM:(g + 1) * HEAD_DIM] for g in range(B_KV_HEADS)]
        lgs = [lax.dot_general(qh[h], kts[h // B_REP], (((1,), (1,)), ((), ())), preferred_element_type=F32)
               for h in range(B_HEADS)]
        mid = []
        for h in range(B_HEADS):
            m, l, _ = states[h]
            lg = lgs[h] + madd
            if delta is not None:
                lg = lg + delta[h]
            m_new = jnp.maximum(m, jnp.max(lg, axis=1, keepdims=True))
            alpha = jnp.exp(m - m_new)
            p = jnp.exp(lg - m_new)
            mid.append((m_new, alpha, alpha * l + jnp.sum(p, axis=1, keepdims=True), p.astype(BF16)))
        out = []
        for h in range(B_HEADS):
            m_new, alpha, l, pb = mid[h]
            acc = alpha * states[h][2] + jnp.dot(pb, vts[h // B_REP], preferred_element_type=F32)
            out.append((m_new, l, acc))
        return tuple(out)

    def far_wide(c, states):
        return attend(pl.multiple_of(c * FAR_WIDE * KEY_CHUNK, FAR_WIDE * KEY_CHUNK), FAR_WIDE * KEY_CHUNK, states, None)

    def far_chunk(c, states):
        return attend(pl.multiple_of(c * KEY_CHUNK, KEY_CHUNK), KEY_CHUNK, states, None)

    n_far = j // 4
    states = lax.fori_loop(0, n_far // FAR_WIDE, far_wide, tuple(init_state() for _ in range(B_HEADS)))
    states = lax.fori_loop((n_far // FAR_WIDE) * FAR_WIDE, n_far, far_chunk, states)
    d0 = pl.multiple_of((3 - j % 4) * Q_TILE, Q_TILE)
    delta = [dl_ref[h, :, pl.ds(d0, TAIL_WIDTH)] for h in range(B_HEADS)]
    states = attend(pl.multiple_of(n_far * KEY_CHUNK, KEY_CHUNK), TAIL_WIDTH, states, delta)

    o_ref[...] = jnp.concatenate([acc / l for _, l, acc in states], axis=1)


def _dsa_prompt(q, qi, kw, kp, vp, kip, delta, n_qblk, topk):
    nkp = kp.shape[0]
    full = lambda a: pl.BlockSpec(a.shape, lambda j: (0,) * a.ndim)
    return pl.pallas_call(
        functools.partial(_dsa_prompt_kernel, topk=topk),
        grid=(n_qblk,),
        in_specs=[pl.BlockSpec((Q_TILE, B_WIDTH), lambda j: (j, 0)),
                  pl.BlockSpec((Q_TILE, IDX_HEADS * IDX_DIM), lambda j: (j, 0)),
                  pl.BlockSpec((Q_TILE, LANES), lambda j: (j, 0)),
                  full(kp), full(vp), full(kip), full(delta)],
        out_specs=pl.BlockSpec((Q_TILE, B_WIDTH), lambda j: (j, 0)),
        out_shape=jax.ShapeDtypeStruct((n_qblk * Q_TILE, B_WIDTH), F32),
        scratch_shapes=[pltpu.VMEM((Q_TILE, nkp), I32), pltpu.VMEM((Q_TILE, TOP_M * LANES), I32)],
        compiler_params=_cparams(("arbitrary",)),
        name="dsa_prompt",
    )(q, qi, kw, kp, vp, kip, delta)


SROWS = 8


def _dsa_sample_score_kernel(pt_ref, qi_ref, kw_ref, *refs, n_steps, pages, dec_seq):
    cki_refs, kin_ref, key_ref = refs[:pages], refs[pages], refs[pages + 1]
    p = pl.program_id(1)
    qi_b = qi_ref[0].astype(BF16)
    wi = kw_ref[0][:, IDX_DIM:IDX_DIM + IDX_HEADS] * IDX_SCALE
    t = lax.broadcasted_iota(I32, (SROWS, 1), 0)
    c = lax.broadcasted_iota(I32, (1, PAGE_SIZE), 1)

    @pl.when(p < n_steps)
    def _():
        ki_all = jnp.concatenate([r[0, 0].astype(BF16) for r in cki_refs], axis=1)
        s = _index_scores(qi_b, wi, ki_all, transposed=True)
        key_ref[0] = _sortable_key(jnp.where(t < dec_seq, s, -jnp.inf))

    @pl.when(p == n_steps)
    def _():
        s = _index_scores(qi_b, wi, kin_ref[0].astype(BF16), transposed=True)
        key_ref[0, :, 0:PAGE_SIZE] = _sortable_key(jnp.where((t < dec_seq) & (c <= t), s, -jnp.inf))
        if pages > 1:
            key_ref[0, :, PAGE_SIZE:] = jnp.full((SROWS, (pages - 1) * PAGE_SIZE), NEG_INF_KEY, I32)


def _dsa_sample_attend_kernel(pt_ref, q_ref, key_ref, *refs, n_steps, pages, topk):
    ck_refs, cv_refs = refs[:pages], refs[pages:2 * pages]
    kn_ref, vn_ref, dl_ref, o_ref, tau_scr, jl_scr, m_scr, l_scr, acc_scr = refs[2 * pages:]
    p = pl.program_id(1)
    width = pages * PAGE_SIZE

    @pl.when(p == 0)
    def _():
        keys = key_ref[0]
        aidx = lax.broadcasted_iota(I32, (1, keys.shape[1]), 1)

        def count_fn(cand, strict):
            hit = (keys > cand) if strict else (keys >= cand)
            return jnp.sum(jnp.where(hit, 1.0, 0.0), axis=1, keepdims=True)

        def count_tie_fn(tau, jl):
            return jnp.sum(jnp.where((keys == tau) & (aidx < jl), 1.0, 0.0), axis=1, keepdims=True)

        tau, jlim = _topk_threshold(count_fn, count_tie_fn, SROWS, topk)
        tau_scr[...] = tau
        jl_scr[...] = jlim
        m_scr[...] = jnp.full(m_scr.shape, NEG_BIG, F32)
        l_scr[...] = jnp.zeros(l_scr.shape, F32)
        acc_scr[...] = jnp.zeros(acc_scr.shape, F32)

    a0 = pl.multiple_of(p * width, width)
    kk = key_ref[0, :, pl.ds(a0, width)]
    aidx = a0 + lax.broadcasted_iota(I32, (1, width), 1)
    sel = _select(kk, aidx, tau_scr[...], jl_scr[...])
    sel3 = jnp.concatenate([sel] * B_REP, axis=0)
    q = q_ref[0] * (HEAD_DIM ** -0.5)

    def cached(ref):
        return lambda g: ref[0, 0, g]

    def fresh(ref):
        return lambda g: ref[0, g]

    def run(k_tiles, v_tiles, delta):
        n = len(k_tiles)
        sel_n = sel3[:, :n * PAGE_SIZE]
        for g in range(B_KV_HEADS):
            qg = jnp.concatenate([q[:, (g * B_REP + r) * HEAD_DIM:(g * B_REP + r + 1) * HEAD_DIM]
                                  for r in range(B_REP)], axis=0).astype(BF16)
            k_all = jnp.concatenate([kt(g).astype(BF16) for kt in k_tiles], axis=1)
            lg = jnp.dot(qg, k_all, preferred_element_type=F32)
            if delta is not None:
                pad = [jnp.zeros((lg.shape[0], (n - 1) * PAGE_SIZE), F32)] if n > 1 else []
                lg = lg + jnp.concatenate(pad + [delta[g]], axis=1)
            lg = jnp.where(sel_n, lg, NEG_BIG)
            m = m_scr[g]
            m_new = jnp.maximum(m, jnp.max(lg, axis=1, keepdims=True))
            alpha = jnp.exp(m - m_new)
            pr = jnp.where(sel_n, jnp.exp(lg - m_new), 0.0).astype(BF16)
            v_all = jnp.concatenate([vt(g).astype(BF16) for vt in v_tiles], axis=1)
            acc = alpha * acc_scr[g] + lax.dot_general(pr, v_all, NT_DIMS, preferred_element_type=F32)
            m_scr[g] = m_new
            l_scr[g] = alpha * l_scr[g] + jnp.sum(pr.astype(F32), axis=1, keepdims=True)
            acc_scr[g] = acc

    ck_tiles = [cached(r) for r in ck_refs]
    cv_tiles = [cached(r) for r in cv_refs]

    @pl.when(p < n_steps - 1)
    def _():
        run(ck_tiles, cv_tiles, None)

    @pl.when(p == n_steps - 1)
    def _():
        run(ck_tiles, cv_tiles, (dl_ref[0, 0], dl_ref[0, 1]))

    @pl.when(p == n_steps)
    def _():
        run([fresh(kn_ref)], [fresh(vn_ref)], (dl_ref[1, 0], dl_ref[1, 1]))
        pieces = [None] * B_HEADS
        for g in range(B_KV_HEADS):
            og = acc_scr[g] / l_scr[g]
            for r in range(B_REP):
                pieces[g * B_REP + r] = og[r * SROWS:(r + 1) * SROWS, :]
        o_ref[0] = jnp.concatenate(pieces, axis=1)


def _dsa_sample(page_table, q_s, qi_s, kw_s, cki, ck, cv, layer, ki_new, k_new, v_new, delta_s, dec_seq, topk):
    nb = q_s.shape[0]
    n_pages = page_table.shape[1]
    pages = math.gcd(n_pages, SAMPLE_PAGES)
    n_steps = n_pages // pages
    width = pages * PAGE_SIZE
    n_keys = (n_steps + 1) * width

    def page(i):
        return lambda b, p, pt: (layer, pt[b, jnp.minimum(p, n_steps - 1) * pages + i], 0, 0)

    def page_kv(i):
        return lambda b, p, pt: (layer, pt[b, jnp.minimum(p, n_steps - 1) * pages + i], 0, 0, 0)

    per_b = lambda b, p, pt: (b, 0, 0)
    keys = pl.pallas_call(
        functools.partial(_dsa_sample_score_kernel, n_steps=n_steps, pages=pages, dec_seq=dec_seq),
        grid_spec=pltpu.PrefetchScalarGridSpec(
            num_scalar_prefetch=1, grid=(nb, n_steps + 1),
            in_specs=[pl.BlockSpec((1, SROWS, IDX_HEADS * IDX_DIM), per_b),
                      pl.BlockSpec((1, SROWS, LANES), per_b)]
            + [pl.BlockSpec((1, 1, IDX_DIM, PAGE_SIZE), page(i)) for i in range(pages)]
            + [pl.BlockSpec((1, IDX_DIM, PAGE_SIZE), per_b)],
            out_specs=pl.BlockSpec((1, SROWS, width), lambda b, p, pt: (b, 0, p))),
        out_shape=jax.ShapeDtypeStruct((nb, SROWS, n_keys), I32),
        compiler_params=_cparams(("arbitrary", "arbitrary")),
        name="dsa_sample_score",
    )(page_table, qi_s, kw_s, *([cki] * pages), ki_new)
    grows = B_REP * SROWS
    return pl.pallas_call(
        functools.partial(_dsa_sample_attend_kernel, n_steps=n_steps, pages=pages, topk=topk),
        grid_spec=pltpu.PrefetchScalarGridSpec(
            num_scalar_prefetch=1, grid=(nb, n_steps + 1),
            in_specs=[pl.BlockSpec((1, SROWS, B_WIDTH), per_b),
                      pl.BlockSpec((1, SROWS, n_keys), per_b)]
            + [pl.BlockSpec((1, 1, B_KV_HEADS, HEAD_DIM, PAGE_SIZE), page_kv(i)) for i in range(pages)] * 2
            + [pl.BlockSpec((1, B_KV_HEADS, HEAD_DIM, PAGE_SIZE), lambda b, p, pt: (b, 0, 0, 0)),
               pl.BlockSpec((1, B_KV_HEADS, HEAD_DIM, PAGE_SIZE), lambda b, p, pt: (b, 0, 0, 0)),
               pl.BlockSpec(delta_s.shape, lambda b, p, pt: (0, 0, 0, 0))],
            out_specs=pl.BlockSpec((1, SROWS, B_WIDTH), per_b),
            scratch_shapes=[pltpu.VMEM((SROWS, 1), I32), pltpu.VMEM((SROWS, 1), I32),
                            pltpu.VMEM((B_KV_HEADS, grows, 1), F32), pltpu.VMEM((B_KV_HEADS, grows, 1), F32),
                            pltpu.VMEM((B_KV_HEADS, grows, HEAD_DIM), F32)]),
        out_shape=jax.ShapeDtypeStruct((nb, SROWS, B_WIDTH), F32),
        compiler_params=_cparams(("arbitrary", "arbitrary")),
        name="dsa_sample_attend",
    )(page_table, q_s, keys, *([ck] * pages), *([cv] * pages), k_new, v_new, delta_s)


def _pool_kernel(c_ref, halo_ref, w_ref, scale_ref, o_ref, *, block, offset):
    i = pl.program_id(1)
    c = c_ref[0]
    halo = jnp.where(i == 0, 0.0, halo_ref[0])
    full = jnp.concatenate([halo, c], axis=0)
    sums = [full]
    for sh in (1, 2, 4, 8):
        sums.append(sums[-1] + pltpu.roll(sums[-1], sh, axis=0))
    lane = lax.broadcasted_iota(I32, (block, C_WIDTH), 1)
    pos = i * block + lax.broadcasted_iota(I32, (block, C_WIDTH), 0) - offset
    win = jnp.zeros((block, C_WIDTH), F32)
    cnt = jnp.ones((block, C_WIDTH), F32)
    for gi, w in enumerate(POOL_WINDOWS):
        in_g = (lane >= gi * C_GROUP_DIM) & (lane < (gi + 1) * C_GROUP_DIM)
        win = jnp.where(in_g, sums[gi + 1][POOL_HALO:, :], win)
        cnt = jnp.where(in_g, jnp.clip(pos + 1, 1, w).astype(F32), cnt)
    d = win / cnt - c
    o_ref[0] = jnp.dot(d.astype(BF16), w_ref[...], preferred_element_type=F32) * scale_ref[...]


def _pool(c, w_bd, scale, block, offset):
    n_seq, length, _ = c.shape
    ratio = block // POOL_HALO
    return pl.pallas_call(
        functools.partial(_pool_kernel, block=block, offset=offset),
        grid=(n_seq, length // block),
        in_specs=[pl.BlockSpec((1, block, C_WIDTH), lambda s, i: (s, i, 0)),
                  pl.BlockSpec((1, POOL_HALO, C_WIDTH), lambda s, i: (s, jnp.maximum(i * ratio - 1, 0), 0)),
                  pl.BlockSpec((C_WIDTH, C_WIDTH), lambda s, i: (0, 0)),
                  pl.BlockSpec((1, C_WIDTH), lambda s, i: (0, 0))],
        out_specs=pl.BlockSpec((1, block, C_WIDTH), lambda s, i: (s, i, 0)),
        out_shape=jax.ShapeDtypeStruct(c.shape, F32),
        compiler_params=_cparams(("parallel", "parallel")),
        name="pool",
    )(c, c, w_bd, scale)


def _outproj_kernel(x_ref, oa_ref, ob_ref, oc_ref, wo_ref, g_ref, wr_ref, br_ref, xo_ref, h_ref, gm_ref):
    mix = jnp.dot(oa_ref[...].astype(BF16), wo_ref[0:A_WIDTH, :], preferred_element_type=F32)
    mix += jnp.dot(ob_ref[...].astype(BF16), wo_ref[A_WIDTH:A_WIDTH + B_WIDTH, :], preferred_element_type=F32)
    mix += jnp.dot(oc_ref[...].astype(BF16), wo_ref[A_WIDTH + B_WIDTH:, :], preferred_element_type=F32)
    x = x_ref[...] + mix
    xo_ref[...] = x
    h = _rms(x, g_ref[...])
    h_ref[...] = h.astype(BF16)
    logits = jnp.dot(h, wr_ref[...], precision=HIGHEST, preferred_element_type=F32) + br_ref[...]
    lane = lax.broadcasted_iota(I32, logits.shape, 1)
    work = logits
    vals, hots = [], []
    for _ in range(TOP_K):
        m = jnp.max(work, axis=1, keepdims=True)
        idx = jnp.min(jnp.where(work == m, lane, N_EXPERTS), axis=1, keepdims=True)
        hot = lane == idx
        vals.append(m)
        hots.append(hot)
        work = jnp.where(hot, -jnp.inf, work)
    es = [jnp.exp(v - vals[0]) for v in vals]
    den = es[0] + es[1] + es[2] + es[3]
    gm = jnp.zeros(logits.shape, F32)
    for e, hot in zip(es, hots):
        gm = jnp.where(hot, e / den, gm)
    gm_ref[...] = gm


def _outproj(x, oa, ob, oc, wo, g, wr, br):
    nt = x.shape[0]
    tile = lambda n: pl.BlockSpec((TOK_TILE, n), lambda i: (i, 0))
    full = lambda a: pl.BlockSpec(a.shape, lambda i: (0, 0))
    return pl.pallas_call(
        _outproj_kernel,
        grid=(nt // TOK_TILE,),
        in_specs=[tile(D_MODEL), tile(A_WIDTH), tile(B_WIDTH), tile(C_WIDTH), full(wo), full(g), full(wr), full(br)],
        out_specs=[tile(D_MODEL), tile(D_MODEL), tile(N_EXPERTS)],
        out_shape=[jax.ShapeDtypeStruct((nt, D_MODEL), F32), jax.ShapeDtypeStruct((nt, D_MODEL), BF16),
                   jax.ShapeDtypeStruct((nt, N_EXPERTS), F32)],
        compiler_params=_cparams(("parallel",)),
        name="outproj",
    )(x, oa, ob, oc, wo, g, wr, br)


MOE_GROUP = 1536
MOE_WINDOW = 240


def _moe_rank_kernel(gm_ref, su_ref, eye_ref, rank_ref, gate_ref, cnt_ref, carry_scr, *, tiles_per_group):
    i = pl.program_id(0)

    @pl.when(i % tiles_per_group == 0)
    def _():
        carry_scr[...] = jnp.zeros_like(carry_scr)

    gm = gm_ref[...]
    sel = jnp.where(gm > 0.0, 1.0, 0.0).astype(BF16)
    before = lax.dot_general(sel, su_ref[...], TN_DIMS, preferred_element_type=F32)
    sel_t = lax.dot_general(sel, eye_ref[...], TN_DIMS, preferred_element_type=F32)
    gate_ref[...] = _dot3(gm, eye_ref[...].astype(F32), TN_DIMS)
    carry = carry_scr[...]
    rank_ref[...] = jnp.where(sel_t > 0.5, carry[:, 0:1] + before, -1.0)
    carry = carry + jnp.sum(sel_t, axis=1, keepdims=True)
    carry_scr[...] = carry
    cnt_ref[0] = carry


def _moe_rank(gm):
    nt = gm.shape[0]
    tile = TOK_TILE
    tpg = MOE_GROUP // tile
    idx = np.arange(tile)
    su = jnp.asarray((idx[:, None] < idx[None, :]).astype(np.float32), BF16)
    eye = jnp.asarray(np.eye(tile, dtype=np.float32), BF16)
    return pl.pallas_call(
        functools.partial(_moe_rank_kernel, tiles_per_group=tpg),
        grid=(nt // tile,),
        in_specs=[pl.BlockSpec((tile, N_EXPERTS), lambda i: (i, 0)),
                  pl.BlockSpec((tile, tile), lambda i: (0, 0)), pl.BlockSpec((tile, tile), lambda i: (0, 0))],
        out_specs=[pl.BlockSpec((N_EXPERTS, tile), lambda i: (0, i)), pl.BlockSpec((N_EXPERTS, tile), lambda i: (0, i)),
                   pl.BlockSpec((1, N_EXPERTS, LANES), lambda i: (i // tpg, 0, 0))],
        out_shape=[jax.ShapeDtypeStruct((N_EXPERTS, nt), F32), jax.ShapeDtypeStruct((N_EXPERTS, nt), F32),
                   jax.ShapeDtypeStruct((nt // MOE_GROUP, N_EXPERTS, LANES), F32)],
        scratch_shapes=[pltpu.VMEM((N_EXPERTS, LANES), F32)],
        compiler_params=_cparams(("arbitrary",)),
        name="moe_rank",
    )(gm, su, eye)


def _one_hot_rows(rank_row, first_row):
    want = (first_row + lax.broadcasted_iota(I32, (MOE_WINDOW, 1), 0)).astype(F32)
    return jnp.where(rank_row == want, 1.0, 0.0)


def _moe_expert_kernel(ie_ref, ij_ref, io_ref, is_ref, iv_ref, h_ref, rank_ref, gate_ref, w1_ref, b1_ref, w2_ref,
                       b2_ref, y_ref, w1b_scr, w2b_scr):
    i = pl.program_id(0)

    @pl.when((i == 0) | (ie_ref[i] != ie_ref[jnp.maximum(i - 1, 0)]))
    def _():
        w1b_scr[...] = w1_ref[0, 0].astype(BF16)
        w2b_scr[...] = w2_ref[0, 0].astype(BF16)

    @pl.when(iv_ref[i] > 0)
    def _():
        hot = _one_hot_rows(rank_ref[0], is_ref[i] * MOE_WINDOW)
        rows = jnp.dot(hot.astype(BF16), h_ref[...], preferred_element_type=F32).astype(BF16)
        gate = jnp.sum(hot * gate_ref[0], axis=1, keepdims=True)
        u = jnp.dot(rows, w1b_scr[...], preferred_element_type=F32) + b1_ref[0, 0]
        glu = jnp.minimum(u[:, :D_FF], SWIGLU_LIMIT)
        lin = jnp.clip(u[:, D_FF:], -SWIGLU_LIMIT, SWIGLU_LIMIT)
        act = (glu * _sigmoid(SWIGLU_ALPHA * glu) * (lin + 1.0)).astype(BF16)
        out = jnp.dot(act, w2b_scr[...], preferred_element_type=F32) + b2_ref[0, 0]
        y_ref[...] = (out * gate).astype(BF16)


def _moe_combine_kernel(ie_ref, ij_ref, io_ref, is_ref, iv_ref, y_ref, rank_ref, x_ref, gf_ref, o_ref, acc_scr,
                        *, n_items, final):
    i = pl.program_id(0)
    valid = iv_ref[i] > 0
    nxt = jnp.minimum(i + 1, n_items - 1)

    @pl.when((i == 0) | (ij_ref[i] != ij_ref[jnp.maximum(i - 1, 0)]))
    def _():
        acc_scr[...] = jnp.zeros_like(acc_scr)

    @pl.when(valid)
    def _():
        hot = _one_hot_rows(rank_ref[0], is_ref[i] * MOE_WINDOW)
        acc_scr[...] += lax.dot_general(hot.astype(BF16), y_ref[...], TN_DIMS, preferred_element_type=F32)

    @pl.when(valid & ((i == n_items - 1) | (ij_ref[nxt] != ij_ref[i]) | (iv_ref[nxt] == 0)))
    def _():
        x = x_ref[...] + acc_scr[...]
        if final:
            x = _rms(x, gf_ref[...])
        o_ref[...] = x


def _moe_items(cnt, order_by_group):
    n_g, n_e = cnt.shape
    n_items = n_g * n_e + -(-(n_g * MOE_GROUP * TOP_K) // MOE_WINDOW)
    max_win = -(-MOE_GROUP // MOE_WINDOW)
    c = cnt.T
    n_win = (c + MOE_WINDOW - 1) // MOE_WINDOW
    seg_off = (jnp.cumsum(n_win.reshape(-1)) - n_win.reshape(-1)).reshape(n_e, n_g)
    e_id, g_id, s_id = jnp.meshgrid(jnp.arange(n_e), jnp.arange(n_g), jnp.arange(max_win), indexing="ij")
    ok = s_id < n_win[:, :, None]
    off = seg_off[:, :, None] + s_id
    if order_by_group:
        e_id, g_id, s_id, ok, off = (jnp.swapaxes(t, 0, 1) for t in (e_id, g_id, s_id, ok, off))
    flat = [t.reshape(-1).astype(I32) for t in (e_id, g_id, off, s_id, ok)]
    pick = jnp.argsort(1 - flat[4], stable=True)[:n_items]
    arrs = [t[pick] for t in flat]
    last = jnp.maximum(jnp.sum(arrs[4]) - 1, 0)
    e_a, g_a, o_a, s_a, v_a = arrs
    keep = v_a > 0
    return tuple(jnp.where(keep, t, t[last]) for t in (e_a, g_a, o_a, s_a)) + (v_a,), n_items


def _moe_routed(h, gm, x, w1, b1, w2, b2, gf, layer, final):
    nt = x.shape[0]
    n_g = nt // MOE_GROUP
    rank_t, gate_t, cnt3 = _moe_rank(gm)
    cnt = cnt3[:, :, 0].astype(I32)
    rank3 = rank_t[:, None, :]
    gate3 = gate_t[:, None, :]
    items_e, n_items = _moe_items(cnt, False)
    win_spec = pl.BlockSpec((MOE_WINDOW, D_MODEL), lambda i, ie, ij, io, is_, iv: (io[i], 0))
    per_eg = pl.BlockSpec((1, 1, MOE_GROUP), lambda i, ie, ij, io, is_, iv: (ie[i], 0, ij[i]))
    grp = lambda n: pl.BlockSpec((MOE_GROUP, n), lambda i, ie, ij, io, is_, iv: (ij[i], 0))
    wspec = lambda a, b: pl.BlockSpec((1, 1, a, b), lambda i, ie, ij, io, is_, iv: (layer, ie[i], 0, 0))
    y = pl.pallas_call(
        _moe_expert_kernel,
        grid_spec=pltpu.PrefetchScalarGridSpec(
            num_scalar_prefetch=5, grid=(n_items,),
            in_specs=[grp(D_MODEL), per_eg, per_eg, wspec(D_MODEL, 2 * D_FF), wspec(1, 2 * D_FF),
                      wspec(D_FF, D_MODEL), wspec(1, D_MODEL)],
            out_specs=win_spec,
            scratch_shapes=[pltpu.VMEM((D_MODEL, 2 * D_FF), BF16), pltpu.VMEM((D_FF, D_MODEL), BF16)]),
        out_shape=jax.ShapeDtypeStruct((n_items * MOE_WINDOW, D_MODEL), BF16),
        compiler_params=_cparams(("arbitrary",)),
        name="moe_expert",
    )(*items_e, h, rank3, gate3, w1, b1[:, :, None, :], w2, b2[:, :, None, :])
    items_g, _ = _moe_items(cnt, True)
    return pl.pallas_call(
        functools.partial(_moe_combine_kernel, n_items=n_items, final=final),
        grid_spec=pltpu.PrefetchScalarGridSpec(
            num_scalar_prefetch=5, grid=(n_items,),
            in_specs=[win_spec, per_eg, grp(D_MODEL),
                      pl.BlockSpec((1, D_MODEL), lambda i, ie, ij, io, is_, iv: (0, 0))],
            out_specs=grp(D_MODEL),
            scratch_shapes=[pltpu.VMEM((MOE_GROUP, D_MODEL), F32)]),
        out_shape=jax.ShapeDtypeStruct((nt, D_MODEL), F32),
        compiler_params=_cparams(("arbitrary",)),
        name="moe_combine",
    )(*items_g, y, rank3, x, gf)


def _rel_bucket(dist):
    n = jnp.maximum(dist, 0)
    max_exact = N_BUCKETS // 2
    log_ratio = jnp.log(jnp.maximum(n, 1).astype(jnp.float32) / max_exact) / math.log(MAX_DISTANCE / max_exact)
    large = jnp.minimum(max_exact + (log_ratio * (N_BUCKETS - max_exact)).astype(jnp.int32), N_BUCKETS - 1)
    return jnp.where(n < max_exact, n, large)


def _bias_delta(rel_bias, dist):
    rb = rel_bias.astype(F32)
    d = jnp.dot(jax.nn.one_hot(_rel_bucket(dist), N_BUCKETS, dtype=F32), rb, precision=HIGHEST) - rb[N_BUCKETS - 1]
    d = jnp.where(((dist >= 0) & (dist < MAX_DISTANCE))[..., None], d, 0.0)
    return jnp.moveaxis(d, -1, 0)


def _group_rows(delta, rows):
    return delta.reshape(B_KV_HEADS, B_REP * rows, delta.shape[-1])


def kernel(x_prompt, x_sample, cache_k, cache_v, cache_kidx, state_wkv, state_shift, state_pool, page_table,
           meta_tokens, rel_bias, norm_mix, norm_ffn, norm_final, w_in, w_out,
           a_mu, a_w0, a_w2, a_a0, a_a2, a_g2, a_kk, a_ka, a_rk, a_ln_w, a_ln_b,
           c_w, c_scale, moe_w_router, moe_b_router, moe_w1, moe_b1, moe_w2, moe_b2):
    assert x_prompt.shape[0] == 1
    depth = w_in.shape[0]
    seq = x_prompt.shape[1]
    nb, dec_seq = x_sample.shape[:2]
    assert dec_seq <= SROWS
    t_p = seq + N_META
    n_s = nb * dec_seq
    n_qblk = -(-t_p // Q_TILE)
    nkp = KEY_CHUNK * ((n_qblk + 4) // 4 + 1)
    assert MOE_GROUP % TOK_TILE == 0
    nt = _round_up(max(t_p + n_s, Q_TILE * n_qblk), MOE_GROUP)
    n_pages = page_table.shape[1]
    past = n_pages * PAGE_SIZE
    topk_p = min(TOPK_MAX, seq // 4)
    topk_s = min(TOPK_MAX, (past + dec_seq) // 4)
    n_phys = cache_k.shape[1]

    x = jnp.concatenate([meta_tokens.astype(F32), x_prompt[0], x_sample.reshape(n_s, D_MODEL),
                         jnp.zeros((nt - t_p - n_s, D_MODEL), F32)], axis=0)

    ones_blk = jnp.kron(jnp.eye(A_HEADS, dtype=F32), jnp.ones((HEAD_DIM, HEAD_DIM), F32))
    qi_ = jnp.arange(Q_TILE)[:, None]
    cj_ = jnp.arange(2 * Q_TILE)[None, :]
    side = TAIL_WIDTH - 2 * Q_TILE
    delta_p = jnp.pad(_bias_delta(rel_bias, Q_TILE + qi_ - cj_), ((0, 0), (0, 0), (side, side)))
    ts_ = jnp.arange(SROWS)[:, None]
    cs_ = jnp.arange(PAGE_SIZE)[None, :]
    delta_s = jnp.stack([_group_rows(_bias_delta(rel_bias, ts_ + PAGE_SIZE - cs_), SROWS),
                         _group_rows(_bias_delta(rel_bias, ts_ - cs_), SROWS)])

    outs = {n: [] for n in ("kp", "vp", "kip", "wkvp", "shp", "plp", "ks", "vs", "kis", "wkvs", "shs", "pls")}
    for l in range(depth):
        wl = w_in[l]
        b0 = A_PROJ
        w_cat = jnp.concatenate([
            wl[:, :A_PROJ], wl[:, b0:b0 + B_WIDTH + 2 * KV_WIDTH + IDX_HEADS * IDX_DIM],
            wl[:, b0 + B_PROJ - IDX_DIM - IDX_HEADS:b0 + B_PROJ],
            jnp.zeros((D_MODEL, LANES - IDX_DIM - IDX_HEADS), F32), wl[:, b0 + B_PROJ:]], axis=1).astype(BF16)
        pa, q, k, v, qi, kw, pc = _inproj(x, norm_mix[l][None], w_cat)

        sl_p = slice(0, t_p)
        sl_s = slice(t_p, t_p + n_s)
        outs["kp"].append(k[sl_p].reshape(1, t_p, B_KV_HEADS, HEAD_DIM))
        outs["vp"].append(v[sl_p].reshape(1, t_p, B_KV_HEADS, HEAD_DIM))
        outs["kip"].append(kw[sl_p, :IDX_DIM][None])
        outs["ks"].append(k[sl_s].reshape(nb, dec_seq, B_KV_HEADS, HEAD_DIM))
        outs["vs"].append(v[sl_s].reshape(nb, dec_seq, B_KV_HEADS, HEAD_DIM))
        outs["kis"].append(kw[sl_s, :IDX_DIM].reshape(nb, dec_seq, IDX_DIM))
        outs["shp"].append(pa[t_p - 1:t_p])
        pa_s = pa[sl_s].reshape(nb, dec_seq, A_PROJ)
        outs["shs"].append(pa_s[:, -1])
        pc_s = pc[sl_s].reshape(nb, dec_seq, C_WIDTH)
        outs["plp"].append(pc[t_p - POOL_BUF:t_p][None])
        pool_full = jnp.concatenate([state_pool[l], pc_s], axis=1)
        outs["pls"].append(pool_full[:, -POOL_BUF:])

        rw = (a_mu[l], a_w0[l], a_w2[l], a_a0[l], a_a2[l], a_g2[l], a_kk[l], a_ka[l], a_rk[l], a_ln_w[l], a_ln_b[l])
        oa_p, st_p = _rwkv_chunked(pa[None], jnp.zeros((1, 1, A_PROJ), F32),
                                   jnp.zeros((1, A_HEADS, HEAD_DIM, HEAD_DIM), F32),
                                   rw, ones_blk, t_p, RWKV_BLOCK, RWKV_CHUNK)
        pa_s8 = jnp.pad(pa_s, ((0, 0), (0, SROWS - dec_seq), (0, 0)))
        oa_s, st_s = _rwkv_chunked(pa_s8, state_shift[l][:, None, :], state_wkv[l].astype(F32),
                                   rw, ones_blk, dec_seq, SROWS, SROWS)
        outs["wkvp"].append(st_p.astype(state_wkv.dtype))
        outs["wkvs"].append(st_s.astype(state_wkv.dtype))

        n_real = min(nt, nkp - KEY_FRONT_PAD)
        front = lambda a: jnp.pad(a[:n_real].astype(BF16), ((KEY_FRONT_PAD, nkp - KEY_FRONT_PAD - n_real), (0, 0)))
        ob_p = _dsa_prompt(q, qi, kw, front(k), front(v), front(kw[:, :IDX_DIM]), delta_p, n_qblk, topk_p)
        rows8 = lambda a: jnp.pad(a[sl_s].reshape(nb, dec_seq, -1), ((0, 0), (0, SROWS - dec_seq), (0, 0)))
        page8 = lambda a: jnp.pad(a[sl_s].reshape(nb, dec_seq, -1), ((0, 0), (0, PAGE_SIZE - dec_seq), (0, 0)))
        kv_page = lambda a: page8(a).reshape(nb, PAGE_SIZE, B_KV_HEADS, HEAD_DIM).transpose(0, 2, 3, 1)
        ob_s = _dsa_sample(page_table, rows8(q), rows8(qi), rows8(kw),
                           cache_kidx.transpose(0, 1, 3, 2), cache_k.transpose(0, 1, 3, 4, 2),
                           cache_v.transpose(0, 1, 3, 4, 2), l, page8(kw[:, :IDX_DIM]).transpose(0, 2, 1),
                           kv_page(k), kv_page(v), delta_s, dec_seq, topk_s)

        w_bd = jax.scipy.linalg.block_diag(*[c_w[l][gi] for gi in range(len(POOL_WINDOWS))]).astype(BF16)
        oc_p = _pool(pc[None], w_bd, c_scale[l][None], TOK_TILE, 0)
        pool_in = jnp.concatenate([jnp.zeros((nb, 1, C_WIDTH), F32), pool_full,
                                   jnp.zeros((nb, 2 * POOL_HALO - 1 - POOL_BUF - dec_seq, C_WIDTH), F32)], axis=1)
        oc_s = _pool(pool_in, w_bd, c_scale[l][None], POOL_HALO, 1)[:, POOL_HALO:POOL_HALO + dec_seq]

        def merge(p_rows, s_rows):
            w = p_rows.shape[-1]
            return jnp.concatenate([p_rows[:t_p], s_rows.reshape(n_s, w), jnp.zeros((nt - t_p - n_s, w), F32)], axis=0)

        oa = merge(oa_p[0], oa_s[:, :dec_seq])
        ob = merge(ob_p, ob_s[:, :dec_seq])
        oc = merge(oc_p[0], oc_s)
        x, h2, gm = _outproj(x, oa, ob, oc, w_out[l].astype(BF16), norm_ffn[l][None], moe_w_router[l],
                             moe_b_router[l][None])
        x = _moe_routed(h2, gm, x, moe_w1, moe_b1, moe_w2, moe_b2, norm_final[None], l, l == depth - 1)

    y_prompt = x[N_META:t_p][None]
    y_sample = x[t_p:t_p + n_s].reshape(nb, dec_seq, D_MODEL)
    st = lambda n: jnp.stack(outs[n])
    return (y_prompt, y_sample, st("kp"), st("vp"), st("kip"), st("wkvp"), st("shp"), st("plp"),
            st("ks"), st("vs"), st("kis"), st("wkvs"), st("shs"), st("pls"))
```

```python
import functools
import math

import jax
import jax.numpy as jnp
import numpy as np
from jax import lax
from jax.experimental import pallas as pl
from jax.experimental.pallas import tpu as pltpu

F32 = jnp.float32
BF16 = jnp.bfloat16
I32 = jnp.int32
HIGHEST = lax.Precision.HIGHEST

D_MODEL = 1024
N_META = 16
HEAD_DIM = 64
A_WIDTH = 384
A_HEADS = 6
LORA_W, LORA_A, LORA_G = 64, 64, 128
A_PROJ = 3 * A_WIDTH + LORA_W + LORA_A + LORA_G
GN_EPS = 64e-5
B_WIDTH = 384
B_HEADS = 6
B_KV_HEADS = 2
B_REP = 3
KV_WIDTH = 128
IDX_HEADS = 4
IDX_DIM = 64
IDX_SCALE = (IDX_HEADS * IDX_DIM) ** -0.5
TOPK_MAX = 256
B_PROJ = B_WIDTH + 2 * KV_WIDTH + IDX_HEADS * IDX_DIM + IDX_DIM + IDX_HEADS
C_WIDTH = 256
C_GROUP_DIM = 64
POOL_WINDOWS = (2, 4, 8, 16)
POOL_BUF = 15
N_BUCKETS = 32
MAX_DISTANCE = 128
N_EXPERTS = 32
TOP_K = 4
D_FF = 1024
SWIGLU_LIMIT = 7.0
SWIGLU_ALPHA = 1.702
RMS_EPS = 1e-5
PAGE_SIZE = 128

LANES = 128
SUBLANES = 8
Q_TILE = 128
KEY_CHUNK = 512
KEY_FRONT_PAD = 128
TAIL_WIDTH = 640
SCORE_CHUNKS = 4
TIE_CHUNKS = 2
FAR_WIDE = 4
TOP_M = 12
SAMPLE_PAGES = 16
TOK_TILE = 512
RWKV_BLOCK = 256
RWKV_CHUNK = 128
POOL_HALO = 16
VMEM_LIMIT = 56 * 1024 * 1024

INT_MIN = -(2 ** 31)
NEG_INF_KEY = np.int32(np.uint32(0xFF800000) ^ np.uint32(0x7FFFFFFF))
NEG_BIG = -1e30


def _round_up(x, m):
    return (x + m - 1) // m * m


def _cparams(sem):
    return pltpu.CompilerParams(dimension_semantics=sem, vmem_limit_bytes=VMEM_LIMIT)


def _rms(x, g):
    return x * lax.rsqrt(jnp.mean(x * x, axis=-1, keepdims=True) + RMS_EPS) * g


def _sigmoid(x):
    return 1.0 / (1.0 + jnp.exp(-x))


IN_COLS = (A_PROJ, B_WIDTH, KV_WIDTH, KV_WIDTH, IDX_HEADS * IDX_DIM, LANES, C_WIDTH)


def _inproj_kernel(x_ref, g_ref, w_ref, *out_refs):
    hb = _rms(x_ref[...], g_ref[...]).astype(BF16)
    off = 0
    for ref in out_refs:
        n = ref.shape[1]
        ref[...] = jnp.dot(hb, w_ref[:, off:off + n], preferred_element_type=F32)
        off += n


def _inproj(x, g, w):
    nt = x.shape[0]
    return pl.pallas_call(
        _inproj_kernel,
        grid=(nt // TOK_TILE,),
        in_specs=[pl.BlockSpec((TOK_TILE, D_MODEL), lambda i: (i, 0)),
                  pl.BlockSpec((1, D_MODEL), lambda i: (0, 0)),
                  pl.BlockSpec(w.shape, lambda i: (0, 0))],
        out_specs=[pl.BlockSpec((TOK_TILE, n), lambda i: (i, 0)) for n in IN_COLS],
        out_shape=[jax.ShapeDtypeStruct((nt, n), F32) for n in IN_COLS],
        compiler_params=_cparams(("parallel",)),
        name="inproj",
    )(x, g, w)


NN_DIMS = (((1,), (0,)), ((), ()))
NT_DIMS = (((1,), (1,)), ((), ()))
TN_DIMS = (((0,), (0,)), ((), ()))


def _dot3(a, b, dims=NN_DIMS):
    a_hi = a.astype(BF16)
    b_hi = b.astype(BF16)
    a_lo = (a - a_hi.astype(F32)).astype(BF16)
    b_lo = (b - b_hi.astype(F32)).astype(BF16)
    dg = functools.partial(lax.dot_general, dimension_numbers=dims, preferred_element_type=F32)
    return dg(a_hi, b_hi) + (dg(a_hi, b_lo) + dg(a_lo, b_hi))


def _rwkv_chunk_kernel(pa_ref, sh_ref, s0_ref, mu_ref, w0_ref, w2_ref, a0_ref, a2_ref, g2_ref, kk_ref, ka_ref,
                       rk_ref, lnw_ref, lnb_ref, ones_ref, tri_ref, o_ref, s_out_ref, carry_scr, s_scr,
                       *, seq_len, block, chunk):
    blk = pl.program_id(1)

    @pl.when(blk == 0)
    def _():
        carry_scr[...] = sh_ref[0]
        s_scr[...] = s0_ref[0]

    pa = pa_ref[0]
    row = lax.broadcasted_iota(I32, (block, 1), 0)
    prev = jnp.where(row == 0, carry_scr[...], pltpu.roll(pa, 1, axis=0))
    carry_scr[...] = pa[block - 1:block, :]
    xm = pa + (prev - pa) * mu_ref[...]
    r = xm[:, 0:A_WIDTH]
    k = xm[:, A_WIDTH:2 * A_WIDTH]
    v = xm[:, 2 * A_WIDTH:3 * A_WIDTH]
    lw = xm[:, 3 * A_WIDTH:3 * A_WIDTH + LORA_W]
    la = xm[:, 3 * A_WIDTH + LORA_W:3 * A_WIDTH + LORA_W + LORA_A]
    lg = xm[:, 3 * A_WIDTH + LORA_W + LORA_A:]

    def dot_hi(a, b):
        return jnp.dot(a, b, precision=HIGHEST, preferred_element_type=F32)

    z = -(w0_ref[...] + dot_hi(jnp.tanh(lw), w2_ref[...]))
    softplus = jnp.maximum(z, 0.0) + jnp.log(1.0 + jnp.exp(-jnp.abs(z)))
    log_w = -jnp.exp(-softplus - 0.5)
    a = _sigmoid(a0_ref[...] + dot_hi(la, a2_ref[...]))
    g = dot_hi(_sigmoid(lg), g2_ref[...])
    ones_blk = ones_ref[...]
    kk = k * kk_ref[...]
    kk = kk / jnp.maximum(jnp.sqrt(dot_hi(kk * kk, ones_blk)), 1e-12)
    k2 = k * (1.0 + (a - 1.0) * ka_ref[...])
    kka = kk * a

    valid = (blk * block + row) < seq_len
    log_w = jnp.where(valid, log_w, 0.0)
    kk_m, kka_m, k2_m, r_m, v_m = (jnp.where(valid, t, 0.0) for t in (kk, kka, k2, r, v))

    cum = dot_hi(tri_ref[0], log_w)
    cum_end = dot_hi(tri_ref[1], log_w)
    kap = kk_m * jnp.exp(cum - log_w)
    rt = r_m * jnp.exp(cum)
    inv_p = jnp.exp(-cum)
    kt = k2_m * inv_p
    bt = kka_m * inv_p
    to_end = jnp.exp(cum_end - cum)
    k_end = k2_m * to_end
    b_end = kka_m * to_end
    p_end = jnp.exp(cum_end)

    sub = min(16, chunk)
    ti = lax.broadcasted_iota(I32, (chunk, chunk), 0)
    tj = lax.broadcasted_iota(I32, (chunk, chunk), 1)
    heads = range(A_HEADS)
    o_chunks = []
    for c in range(block // chunk):
        rs = slice(c * chunk, (c + 1) * chunk)
        hsl = lambda t, h: t[rs, h * HEAD_DIM:(h + 1) * HEAD_DIM]
        lhs = [jnp.concatenate([hsl(kap, h), hsl(rt, h)], axis=0) for h in heads]
        rhs = [jnp.concatenate([hsl(kt, h), hsl(bt, h)], axis=0) for h in heads]
        vh = [hsl(v_m, h) for h in heads]
        gram = [_dot3(lhs[h], rhs[h], NT_DIMS) for h in heads]
        from_s = [_dot3(lhs[h], s_scr[h], NT_DIMS) for h in heads]
        a1 = [jnp.where(ti > tj, gram[h][:chunk, :chunk], 0.0) for h in heads]
        a2 = [jnp.where(ti > tj, gram[h][:chunk, chunk:], 0.0) for h in heads]
        a3 = [jnp.where(ti >= tj, gram[h][chunk:, :chunk], 0.0) for h in heads]
        a4 = [jnp.where(ti >= tj, gram[h][chunk:, chunk:], 0.0) for h in heads]
        u = [from_s[h][:chunk] + _dot3(a1[h], vh[h]) for h in heads]
        zs = [[] for _ in heads]
        for jb in range(chunk // sub):
            js = slice(jb * sub, (jb + 1) * sub)
            zj = []
            for h in heads:
                t = u[h][js]
                if jb > 0:
                    t = t - _dot3(a2[h][js, :jb * sub], jnp.concatenate(zs[h], axis=0))
                zj.append(t)
            for i in range(sub - 1):
                for h in heads:
                    col = jnp.broadcast_to(a2[h][js, jb * sub + i:jb * sub + i + 1], (sub, HEAD_DIM))
                    zj[h] = zj[h] - col * zj[h][i:i + 1, :]
            for h in heads:
                zs[h].append(zj[h])
        zh = [jnp.concatenate(zs[h], axis=0) for h in heads]
        vz = [jnp.concatenate([vh[h], zh[h]], axis=0) for h in heads]
        o_chunks.append(jnp.concatenate(
            [from_s[h][chunk:] + _dot3(jnp.concatenate([a3[h], -a4[h]], axis=1), vz[h]) for h in heads], axis=1))
        for h in heads:
            ends = jnp.concatenate([hsl(k_end, h), -hsl(b_end, h)], axis=0)
            s_scr[h] = s_scr[h] * hsl(p_end, h)[0:1, :] + _dot3(vz[h], ends, TN_DIMS)
    s_out_ref[0] = s_scr[...]

    o = jnp.concatenate(o_chunks, axis=0) if len(o_chunks) > 1 else o_chunks[0]
    inv = 1.0 / HEAD_DIM
    mean = dot_hi(o, ones_blk) * inv
    cen = o - mean
    var = dot_hi(cen * cen, ones_blk) * inv
    o = cen * lax.rsqrt(var + GN_EPS) * lnw_ref[...] + lnb_ref[...]
    bonus = dot_hi(r * k2 * rk_ref[...], ones_blk) * v
    o_ref[0] = (o + bonus) * g


def _rwkv_chunked(pa, shift_prev, s0, params, ones_blk, seq_len, block, chunk):
    n_seq, length, _ = pa.shape
    n_blk = -(-seq_len // block)
    idx = np.arange(block)
    same = (idx[:, None] // chunk) == (idx[None, :] // chunk)
    tri = jnp.asarray(np.stack([same & (idx[None, :] <= idx[:, None]), same]).astype(np.float32))
    row_spec = lambda n: pl.BlockSpec((1, n), lambda s, i: (0, 0))
    mat_spec = lambda a: pl.BlockSpec(a.shape, lambda s, i: (0,) * a.ndim)
    mu, w0, w2, a0, a2, g2, k_k, k_a, r_k, ln_w, ln_b = params
    kern = functools.partial(_rwkv_chunk_kernel, seq_len=seq_len, block=block, chunk=chunk)
    state_spec = pl.BlockSpec((1, A_HEADS, HEAD_DIM, HEAD_DIM), lambda s, i: (s, 0, 0, 0))
    return pl.pallas_call(
        kern,
        grid=(n_seq, n_blk),
        in_specs=[pl.BlockSpec((1, block, A_PROJ), lambda s, i: (s, i, 0)),
                  pl.BlockSpec((1, 1, A_PROJ), lambda s, i: (s, 0, 0)),
                  state_spec,
                  row_spec(A_PROJ), row_spec(A_WIDTH), mat_spec(w2), row_spec(A_WIDTH), mat_spec(a2),
                  mat_spec(g2), row_spec(A_WIDTH), row_spec(A_WIDTH), row_spec(A_WIDTH), row_spec(A_WIDTH),
                  row_spec(A_WIDTH), mat_spec(ones_blk), mat_spec(tri)],
        out_specs=[pl.BlockSpec((1, block, A_WIDTH), lambda s, i: (s, i, 0)), state_spec],
        out_shape=[jax.ShapeDtypeStruct((n_seq, length, A_WIDTH), F32),
                   jax.ShapeDtypeStruct((n_seq, A_HEADS, HEAD_DIM, HEAD_DIM), F32)],
        scratch_shapes=[pltpu.VMEM((1, A_PROJ), F32), pltpu.VMEM((A_HEADS, HEAD_DIM, HEAD_DIM), F32)],
        compiler_params=_cparams(("arbitrary", "arbitrary")),
        name="rwkv",
    )(pa, shift_prev, s0, mu[None], w0[None], w2, a0[None], a2, g2, k_k[None], k_a[None], r_k[None],
      ln_w[None], ln_b[None], ones_blk, tri)


def _sortable_key(s):
    bits = pltpu.bitcast(s + 0.0, I32)
    return bits ^ ((bits >> 31) & jnp.int32(0x7FFFFFFF))


def _index_scores(qi_b, wi, ki_b, transposed=False):
    s = None
    for h in range(IDX_HEADS):
        d = lax.dot_general(qi_b[:, h * IDX_DIM:(h + 1) * IDX_DIM], ki_b,
                            (((1,), (0 if transposed else 1,)), ((), ())), preferred_element_type=F32)
        t = jnp.maximum(d, 0.0) * wi[:, h:h + 1]
        s = t if s is None else s + t
    return s


def _topk_threshold(count_fn, count_tie_fn, rows, topk):
    kf = jnp.float32(topk)
    c0 = count_fn(jnp.zeros((rows, 1), I32), False)
    prefix0 = jnp.where(c0 >= kf, jnp.int32(0), jnp.int32(INT_MIN))

    def bit_step(i, prefix):
        cand = prefix | lax.shift_left(jnp.int32(1), jnp.int32(30) - i)
        return jnp.where(count_fn(cand, False) >= kf, cand, prefix)

    tau = lax.fori_loop(0, 31, bit_step, prefix0)
    c_gt = count_fn(tau, True)
    c_ge = count_fn(tau, False)
    need = kf - c_gt
    excess = jnp.max(c_ge - c_gt - need)

    def tie_search(_):
        def jbit(i, jl):
            cand = jl | lax.shift_left(jnp.int32(1), jnp.int32(14) - i)
            return jnp.where(count_tie_fn(tau, cand) < need, cand, jl)
        return lax.fori_loop(0, 15, jbit, jnp.zeros((rows, 1), I32))

    jlim = lax.cond(excess > 0.0, tie_search, lambda _: jnp.full((rows, 1), 2 ** 30, I32), 0)
    return tau, jlim


def _select(keys, aidx, tau, jlim):
    return ((keys > tau) | ((keys == tau) & (aidx <= jlim))) & (keys > jnp.int32(NEG_INF_KEY))


def _softmax_step(state, qg, kt, vt, sel_rows, delta):
    m, l, acc = state
    lg = lax.dot_general(qg, kt, (((1,), (1,)), ((), ())), preferred_element_type=F32)
    if delta is not None:
        lg = lg + delta
    lg = jnp.where(sel_rows, lg, NEG_BIG)
    m_new = jnp.maximum(m, jnp.max(lg, axis=1, keepdims=True))
    alpha = jnp.exp(m - m_new)
    p = jnp.where(sel_rows, jnp.exp(lg - m_new), 0.0)
    l = alpha * l + jnp.sum(p, axis=1, keepdims=True)
    acc = alpha * acc + jnp.dot(p.astype(BF16), vt, preferred_element_type=F32)
    return m_new, l, acc


def _dsa_prompt_kernel(q_ref, qi_ref, kw_ref, kp_ref, vp_ref, kip_ref, dl_ref, o_ref, key_scr, cand_scr, *, topk):
    j = pl.program_id(0)
    rows = Q_TILE
    n_chunks = (j + 5) // 4
    qi_b = qi_ref[...].astype(BF16)
    wi = kw_ref[:, IDX_DIM:IDX_DIM + IDX_HEADS] * IDX_SCALE
    qpos = j * Q_TILE + lax.broadcasted_iota(I32, (rows, 1), 0)
    lane_c = lax.broadcasted_iota(I32, (1, KEY_CHUNK), 1)

    def score_chunk(c, masked):
        a0 = pl.multiple_of(c * KEY_CHUNK, KEY_CHUNK)
        s = _index_scores(qi_b, wi, kip_ref[pl.ds(a0, KEY_CHUNK), :])
        if masked:
            kpos = a0 - KEY_FRONT_PAD + lane_c
            s = jnp.where((kpos <= qpos) & (kpos >= 0), s, -jnp.inf)
        key_scr[:, pl.ds(a0, KEY_CHUNK)] = _sortable_key(s)

    n_plain = jnp.clip((j * Q_TILE + KEY_FRONT_PAD - KEY_CHUNK) // KEY_CHUNK, 0, n_chunks - 1)
    score_chunk(0, True)

    def plain_group(p, x):
        for i in range(SCORE_CHUNKS):
            score_chunk(1 + SCORE_CHUNKS * p + i, False)
        return x

    lax.fori_loop(0, n_plain // SCORE_CHUNKS, plain_group, 0)
    lax.fori_loop(1 + SCORE_CHUNKS * (n_plain // SCORE_CHUNKS), n_chunks, lambda c, x: (score_chunk(c, True), x)[1], 0)

    key_scr[:, pl.ds(pl.multiple_of(n_chunks * KEY_CHUNK, KEY_CHUNK), KEY_CHUNK)] = jnp.full(
        (rows, KEY_CHUNK), NEG_INF_KEY, I32)

    n_fold = KEY_CHUNK // LANES

    def chunk_at(ref, c):
        return ref[:, pl.ds(pl.multiple_of(c * KEY_CHUNK, KEY_CHUNK), KEY_CHUNK)]

    def count32(cand, strict):
        def body(c, acc):
            kk = chunk_at(key_scr, c)
            hit = (kk > cand) if strict else (kk >= cand)
            m = jnp.where(hit, 1.0, 0.0)
            for i in range(n_fold):
                acc = acc + m[:, i * LANES:(i + 1) * LANES]
            return acc
        acc = lax.fori_loop(0, n_chunks, body, jnp.zeros((rows, LANES), F32))
        return jnp.sum(acc, axis=1, keepdims=True)

    kf = jnp.float32(topk)

    def kth_largest(count_ge):
        def bit_step(i, prefix):
            cand = prefix | lax.shift_left(jnp.int32(1), jnp.int32(30) - i)
            return jnp.where(count_ge(cand) >= kf, cand, prefix)
        nonneg = count_ge(jnp.zeros((rows, 1), I32)) >= kf
        return lax.fori_loop(0, 31, bit_step, jnp.where(nonneg, jnp.int32(0), jnp.int32(INT_MIN)))

    pair = 2 * SUBLANES
    for rp in range(rows // pair):
        r0 = rp * pair

        def insert_chunk(c, lists, r0=r0):
            a0 = pl.multiple_of(c * KEY_CHUNK, KEY_CHUNK)
            blk = key_scr[r0:r0 + pair, pl.ds(a0, KEY_CHUNK)]
            blk = pltpu.bitcast(blk ^ ((blk >> 31) & jnp.int32(0x7FFFFFFF)), F32)
            lists = list(lists)
            for half in range(2):
                for t in range(n_fold):
                    x = blk[half * SUBLANES:(half + 1) * SUBLANES, t * LANES:(t + 1) * LANES]
                    for i in range(TOP_M):
                        a = lists[half * TOP_M + i]
                        lists[half * TOP_M + i] = jnp.maximum(a, x)
                        x = jnp.minimum(a, x)
            return tuple(lists)

        lists = lax.fori_loop(0, n_chunks, insert_chunk,
                              tuple(jnp.full((SUBLANES, LANES), -jnp.inf, F32) for _ in range(2 * TOP_M)))
        for half in range(2):
            for i in range(TOP_M):
                cand_scr[r0 + half * SUBLANES:r0 + (half + 1) * SUBLANES, i * LANES:(i + 1) * LANES] = _sortable_key(
                    lists[half * TOP_M + i])

    def count_cand(cand):
        acc = jnp.zeros((rows, LANES), F32)
        for i in range(TOP_M):
            acc = acc + jnp.where(cand_scr[:, i * LANES:(i + 1) * LANES] >= cand, 1.0, 0.0)
        return jnp.sum(acc, axis=1, keepdims=True)

    tau_c = kth_largest(count_cand)
    gt_c = count32(tau_c, True)
    ge_c = count32(tau_c, False)
    wrong = jnp.max(jnp.where((gt_c < kf) & (ge_c >= kf), 0.0, 1.0))

    def full_search(_):
        t = kth_largest(lambda cand: count32(cand, False))
        return t, count32(t, True), count32(t, False)

    tau, c_gt, c_ge = lax.cond(wrong > 0.0, full_search, lambda _: (tau_c, gt_c, ge_c), 0)
    need = kf - c_gt
    excess = jnp.max(c_ge - c_gt - need)

    def tie_pass(_):
        ri = lax.broadcasted_iota(I32, (LANES, 2 * LANES), 0)
        ci = lax.broadcasted_iota(I32, (LANES, 2 * LANES), 1)
        tri = jnp.where((ri <= ci) | (ci >= LANES), 1.0, 0.0).astype(BF16)

        span = TIE_CHUNKS * KEY_CHUNK
        n_tile = span // LANES

        def body(c, carry):
            seen, jacc = carry
            a0 = pl.multiple_of(c * span, span)
            kk = key_scr[:, pl.ds(a0, span)]
            ties = [kk[:, i * LANES:(i + 1) * LANES] == tau for i in range(n_tile)]
            pres = [jnp.dot(jnp.where(t, 1.0, 0.0).astype(BF16), tri, preferred_element_type=F32) for t in ties]
            for i in range(n_tile):
                rank = jnp.where(ties[i], seen + pres[i][:, :LANES], 1e9)
                aidx = a0 + i * LANES + lax.broadcasted_iota(I32, (1, LANES), 1)
                jacc = jnp.maximum(jacc, jnp.where(rank <= need, aidx, -1))
                seen = seen + pres[i][:, LANES:]
            return seen, jacc
        _, jacc = lax.fori_loop(0, (n_chunks + TIE_CHUNKS - 1) // TIE_CHUNKS, body,
                                (jnp.zeros((rows, LANES), F32), jnp.full((rows, LANES), -1, I32)))
        return jnp.max(jacc, axis=1, keepdims=True)

    jlim = lax.cond(excess > 0.0, tie_pass, lambda _: jnp.full((rows, 1), 2 ** 30, I32), 0)
    jlim = jnp.where(tau == jnp.int32(NEG_INF_KEY), jnp.int32(-1), jlim)
    tau_b = jnp.broadcast_to(tau, (rows, LANES))
    tau_m1_b = tau_b - 1

    q = q_ref[...] * (HEAD_DIM ** -0.5)
    qh = [q[:, h * HEAD_DIM:(h + 1) * HEAD_DIM].astype(BF16) for h in range(B_HEADS)]

    def init_state():
        return (jnp.full((rows, 1), NEG_BIG, F32), jnp.zeros((rows, 1), F32), jnp.zeros((rows, HEAD_DIM), F32))

    def attend(a0, width, states, delta):
        kk = key_scr[:, pl.ds(a0, width)]
        aidx = a0 + lax.broadcasted_iota(I32, (1, width), 1)
        thr = jnp.where(aidx <= jlim, jnp.concatenate([tau_m1_b] * (width // LANES), axis=1),
                        jnp.concatenate([tau_b] * (width // LANES), axis=1))
        madd = jnp.where(kk > thr, 0.0, NEG_BIG)
        kts = [kp_ref[pl.ds(a0, width), g * HEAD_DIM:(g + 1) * HEAD_DIM] for g in range(B_KV_HEADS)]
        vts = [vp_ref[pl.ds(a0, width), g * HEAD_DIM:(g + 1) * HEAD_DIM] for g in range(B_KV_HEADS)]
        lgs = [lax.dot_general(qh[h], kts[h // B_REP], (((1,), (1,)), ((), ())), preferred_element_type=F32)
               for h in range(B_HEADS)]
        mid = []
        for h in range(B_HEADS):
            m, l, _ = states[h]
            lg = lgs[h] + madd
            if delta is not None:
                lg = lg + delta[h]
            m_new = jnp.maximum(m, jnp.max(lg, axis=1, keepdims=True))
            alpha = jnp.exp(m - m_new)
            p = jnp.exp(lg - m_new)
            mid.append((m_new, alpha, alpha * l + jnp.sum(p, axis=1, keepdims=True), p.astype(BF16)))
        out = []
        for h in range(B_HEADS):
            m_new, alpha, l, pb = mid[h]
            acc = alpha * states[h][2] + jnp.dot(pb, vts[h // B_REP], preferred_element_type=F32)
            out.append((m_new, l, acc))
        return tuple(out)

    def far_wide(c, states):
        return attend(pl.multiple_of(c * FAR_WIDE * KEY_CHUNK, FAR_WIDE * KEY_CHUNK), FAR_WIDE * KEY_CHUNK, states, None)

    def far_chunk(c, states):
        return attend(pl.multiple_of(c * KEY_CHUNK, KEY_CHUNK), KEY_CHUNK, states, None)

    n_far = j // 4
    states = lax.fori_loop(0, n_far // FAR_WIDE, far_wide, tuple(init_state() for _ in range(B_HEADS)))
    states = lax.fori_loop((n_far // FAR_WIDE) * FAR_WIDE, n_far, far_chunk, states)
    d0 = pl.multiple_of((3 - j % 4) * Q_TILE, Q_TILE)
    delta = [dl_ref[h, :, pl.ds(d0, TAIL_WIDTH)] for h in range(B_HEADS)]
    states = attend(pl.multiple_of(n_far * KEY_CHUNK, KEY_CHUNK), TAIL_WIDTH, states, delta)

    o_ref[...] = jnp.concatenate([acc / l for _, l, acc in states], axis=1)


def _dsa_prompt(q, qi, kw, kp, vp, kip, delta, n_qblk, topk):
    nkp = kp.shape[0]
    full = lambda a: pl.BlockSpec(a.shape, lambda j: (0,) * a.ndim)
    return pl.pallas_call(
        functools.partial(_dsa_prompt_kernel, topk=topk),
        grid=(n_qblk,),
        in_specs=[pl.BlockSpec((Q_TILE, B_WIDTH), lambda j: (j, 0)),
                  pl.BlockSpec((Q_TILE, IDX_HEADS * IDX_DIM), lambda j: (j, 0)),
                  pl.BlockSpec((Q_TILE, LANES), lambda j: (j, 0)),
                  full(kp), full(vp), full(kip), full(delta)],
        out_specs=pl.BlockSpec((Q_TILE, B_WIDTH), lambda j: (j, 0)),
        out_shape=jax.ShapeDtypeStruct((n_qblk * Q_TILE, B_WIDTH), F32),
        scratch_shapes=[pltpu.VMEM((Q_TILE, nkp), I32), pltpu.VMEM((Q_TILE, TOP_M * LANES), I32)],
        compiler_params=_cparams(("arbitrary",)),
        name="dsa_prompt",
    )(q, qi, kw, kp, vp, kip, delta)


SROWS = 8


def _dsa_sample_score_kernel(pt_ref, qi_ref, kw_ref, *refs, n_steps, pages, dec_seq):
    cki_refs, kin_ref, key_ref = refs[:pages], refs[pages], refs[pages + 1]
    p = pl.program_id(1)
    qi_b = qi_ref[0].astype(BF16)
    wi = kw_ref[0][:, IDX_DIM:IDX_DIM + IDX_HEADS] * IDX_SCALE
    t = lax.broadcasted_iota(I32, (SROWS, 1), 0)
    c = lax.broadcasted_iota(I32, (1, PAGE_SIZE), 1)

    @pl.when(p < n_steps)
    def _():
        ki_all = jnp.concatenate([r[0, 0].astype(BF16) for r in cki_refs], axis=1)
        s = _index_scores(qi_b, wi, ki_all, transposed=True)
        key_ref[0] = _sortable_key(jnp.where(t < dec_seq, s, -jnp.inf))

    @pl.when(p == n_steps)
    def _():
        s = _index_scores(qi_b, wi, kin_ref[0].astype(BF16), transposed=True)
        key_ref[0, :, 0:PAGE_SIZE] = _sortable_key(jnp.where((t < dec_seq) & (c <= t), s, -jnp.inf))
        if pages > 1:
            key_ref[0, :, PAGE_SIZE:] = jnp.full((SROWS, (pages - 1) * PAGE_SIZE), NEG_INF_KEY, I32)


def _dsa_sample_attend_kernel(pt_ref, q_ref, key_ref, *refs, n_steps, pages, topk):
    ck_refs, cv_refs = refs[:pages], refs[pages:2 * pages]
    kn_ref, vn_ref, dl_ref, o_ref, tau_scr, jl_scr, m_scr, l_scr, acc_scr = refs[2 * pages:]
    p = pl.program_id(1)
    width = pages * PAGE_SIZE

    @pl.when(p == 0)
    def _():
        keys = key_ref[0]
        aidx = lax.broadcasted_iota(I32, (1, keys.shape[1]), 1)

        def count_fn(cand, strict):
            hit = (keys > cand) if strict else (keys >= cand)
            return jnp.sum(jnp.where(hit, 1.0, 0.0), axis=1, keepdims=True)

        def count_tie_fn(tau, jl):
            return jnp.sum(jnp.where((keys == tau) & (aidx < jl), 1.0, 0.0), axis=1, keepdims=True)

        tau, jlim = _topk_threshold(count_fn, count_tie_fn, SROWS, topk)
        tau_scr[...] = tau
        jl_scr[...] = jlim
        m_scr[...] = jnp.full(m_scr.shape, NEG_BIG, F32)
        l_scr[...] = jnp.zeros(l_scr.shape, F32)
        acc_scr[...] = jnp.zeros(acc_scr.shape, F32)

    a0 = pl.multiple_of(p * width, width)
    kk = key_ref[0, :, pl.ds(a0, width)]
    aidx = a0 + lax.broadcasted_iota(I32, (1, width), 1)
    sel = _select(kk, aidx, tau_scr[...], jl_scr[...])
    sel3 = jnp.concatenate([sel] * B_REP, axis=0)
    q = q_ref[0] * (HEAD_DIM ** -0.5)

    def cached(ref):
        return lambda g: ref[0, 0, g]

    def fresh(ref):
        return lambda g: ref[0, g]

    def run(k_tiles, v_tiles, delta):
        n = len(k_tiles)
        sel_n = sel3[:, :n * PAGE_SIZE]
        for g in range(B_KV_HEADS):
            qg = jnp.concatenate([q[:, (g * B_REP + r) * HEAD_DIM:(g * B_REP + r + 1) * HEAD_DIM]
                                  for r in range(B_REP)], axis=0).astype(BF16)
            k_all = jnp.concatenate([kt(g).astype(BF16) for kt in k_tiles], axis=1)
            lg = jnp.dot(qg, k_all, preferred_element_type=F32)
            if delta is not None:
                pad = [jnp.zeros((lg.shape[0], (n - 1) * PAGE_SIZE), F32)] if n > 1 else []
                lg = lg + jnp.concatenate(pad + [delta[g]], axis=1)
            lg = jnp.where(sel_n, lg, NEG_BIG)
            m = m_scr[g]
            m_new = jnp.maximum(m, jnp.max(lg, axis=1, keepdims=True))
            alpha = jnp.exp(m - m_new)
            pr = jnp.where(sel_n, jnp.exp(lg - m_new), 0.0).astype(BF16)
            v_all = jnp.concatenate([vt(g).astype(BF16) for vt in v_tiles], axis=1)
            acc = alpha * acc_scr[g] + lax.dot_general(pr, v_all, NT_DIMS, preferred_element_type=F32)
            m_scr[g] = m_new
            l_scr[g] = alpha * l_scr[g] + jnp.sum(pr.astype(F32), axis=1, keepdims=True)
            acc_scr[g] = acc

    ck_tiles = [cached(r) for r in ck_refs]
    cv_tiles = [cached(r) for r in cv_refs]

    @pl.when(p < n_steps - 1)
    def _():
        run(ck_tiles, cv_tiles, None)

    @pl.when(p == n_steps - 1)
    def _():
        run(ck_tiles, cv_tiles, (dl_ref[0, 0], dl_ref[0, 1]))

    @pl.when(p == n_steps)
    def _():
        run([fresh(kn_ref)], [fresh(vn_ref)], (dl_ref[1, 0], dl_ref[1, 1]))
        pieces = [None] * B_HEADS
        for g in range(B_KV_HEADS):
            og = acc_scr[g] / l_scr[g]
            for r in range(B_REP):
                pieces[g * B_REP + r] = og[r * SROWS:(r + 1) * SROWS, :]
        o_ref[0] = jnp.concatenate(pieces, axis=1)


def _dsa_sample(page_table, q_s, qi_s, kw_s, cki, ck, cv, layer, ki_new, k_new, v_new, delta_s, dec_seq, topk):
    nb = q_s.shape[0]
    n_pages = page_table.shape[1]
    pages = math.gcd(n_pages, SAMPLE_PAGES)
    n_steps = n_pages // pages
    width = pages * PAGE_SIZE
    n_keys = (n_steps + 1) * width

    def page(i):
        return lambda b, p, pt: (layer, pt[b, jnp.minimum(p, n_steps - 1) * pages + i], 0, 0)

    def page_kv(i):
        return lambda b, p, pt: (layer, pt[b, jnp.minimum(p, n_steps - 1) * pages + i], 0, 0, 0)

    per_b = lambda b, p, pt: (b, 0, 0)
    keys = pl.pallas_call(
        functools.partial(_dsa_sample_score_kernel, n_steps=n_steps, pages=pages, dec_seq=dec_seq),
        grid_spec=pltpu.PrefetchScalarGridSpec(
            num_scalar_prefetch=1, grid=(nb, n_steps + 1),
            in_specs=[pl.BlockSpec((1, SROWS, IDX_HEADS * IDX_DIM), per_b),
                      pl.BlockSpec((1, SROWS, LANES), per_b)]
            + [pl.BlockSpec((1, 1, IDX_DIM, PAGE_SIZE), page(i)) for i in range(pages)]
            + [pl.BlockSpec((1, IDX_DIM, PAGE_SIZE), per_b)],
            out_specs=pl.BlockSpec((1, SROWS, width), lambda b, p, pt: (b, 0, p))),
        out_shape=jax.ShapeDtypeStruct((nb, SROWS, n_keys), I32),
        compiler_params=_cparams(("arbitrary", "arbitrary")),
        name="dsa_sample_score",
    )(page_table, qi_s, kw_s, *([cki] * pages), ki_new)
    grows = B_REP * SROWS
    return pl.pallas_call(
        functools.partial(_dsa_sample_attend_kernel, n_steps=n_steps, pages=pages, topk=topk),
        grid_spec=pltpu.PrefetchScalarGridSpec(
            num_scalar_prefetch=1, grid=(nb, n_steps + 1),
            in_specs=[pl.BlockSpec((1, SROWS, B_WIDTH), per_b),
                      pl.BlockSpec((1, SROWS, n_keys), per_b)]
            + [pl.BlockSpec((1, 1, B_KV_HEADS, HEAD_DIM, PAGE_SIZE), page_kv(i)) for i in range(pages)] * 2
            + [pl.BlockSpec((1, B_KV_HEADS, HEAD_DIM, PAGE_SIZE), lambda b, p, pt: (b, 0, 0, 0)),
               pl.BlockSpec((1, B_KV_HEADS, HEAD_DIM, PAGE_SIZE), lambda b, p, pt: (b, 0, 0, 0)),
               pl.BlockSpec(delta_s.shape, lambda b, p, pt: (0, 0, 0, 0))],
            out_specs=pl.BlockSpec((1, SROWS, B_WIDTH), per_b),
            scratch_shapes=[pltpu.VMEM((SROWS, 1), I32), pltpu.VMEM((SROWS, 1), I32),
                            pltpu.VMEM((B_KV_HEADS, grows, 1), F32), pltpu.VMEM((B_KV_HEADS, grows, 1), F32),
                            pltpu.VMEM((B_KV_HEADS, grows, HEAD_DIM), F32)]),
        out_shape=jax.ShapeDtypeStruct((nb, SROWS, B_WIDTH), F32),
        compiler_params=_cparams(("arbitrary", "arbitrary")),
        name="dsa_sample_attend",
    )(page_table, q_s, keys, *([ck] * pages), *([cv] * pages), k_new, v_new, delta_s)


def _pool_kernel(c_ref, halo_ref, w_ref, scale_ref, o_ref, *, block, offset):
    i = pl.program_id(1)
    c = c_ref[0]
    halo = jnp.where(i == 0, 0.0, halo_ref[0])
    full = jnp.concatenate([halo, c], axis=0)
    sums = [full]
    for sh in (1, 2, 4, 8):
        sums.append(sums[-1] + pltpu.roll(sums[-1], sh, axis=0))
    lane = lax.broadcasted_iota(I32, (block, C_WIDTH), 1)
    pos = i * block + lax.broadcasted_iota(I32, (block, C_WIDTH), 0) - offset
    win = jnp.zeros((block, C_WIDTH), F32)
    cnt = jnp.ones((block, C_WIDTH), F32)
    for gi, w in enumerate(POOL_WINDOWS):
        in_g = (lane >= gi * C_GROUP_DIM) & (lane < (gi + 1) * C_GROUP_DIM)
        win = jnp.where(in_g, sums[gi + 1][POOL_HALO:, :], win)
        cnt = jnp.where(in_g, jnp.clip(pos + 1, 1, w).astype(F32), cnt)
    d = win / cnt - c
    o_ref[0] = jnp.dot(d.astype(BF16), w_ref[...], preferred_element_type=F32) * scale_ref[...]


def _pool(c, w_bd, scale, block, offset):
    n_seq, length, _ = c.shape
    ratio = block // POOL_HALO
    return pl.pallas_call(
        functools.partial(_pool_kernel, block=block, offset=offset),
        grid=(n_seq, length // block),
        in_specs=[pl.BlockSpec((1, block, C_WIDTH), lambda s, i: (s, i, 0)),
                  pl.BlockSpec((1, POOL_HALO, C_WIDTH), lambda s, i: (s, jnp.maximum(i * ratio - 1, 0), 0)),
                  pl.BlockSpec((C_WIDTH, C_WIDTH), lambda s, i: (0, 0)),
                  pl.BlockSpec((1, C_WIDTH), lambda s, i: (0, 0))],
        out_specs=pl.BlockSpec((1, block, C_WIDTH), lambda s, i: (s, i, 0)),
        out_shape=jax.ShapeDtypeStruct(c.shape, F32),
        compiler_params=_cparams(("parallel", "parallel")),
        name="pool",
    )(c, c, w_bd, scale)


def _outproj_kernel(x_ref, oa_ref, ob_ref, oc_ref, wo_ref, g_ref, wr_ref, br_ref, xo_ref, h_ref, gm_ref):
    mix = jnp.dot(oa_ref[...].astype(BF16), wo_ref[0:A_WIDTH, :], preferred_element_type=F32)
    mix += jnp.dot(ob_ref[...].astype(BF16), wo_ref[A_WIDTH:A_WIDTH + B_WIDTH, :], preferred_element_type=F32)
    mix += jnp.dot(oc_ref[...].astype(BF16), wo_ref[A_WIDTH + B_WIDTH:, :], preferred_element_type=F32)
    x = x_ref[...] + mix
    xo_ref[...] = x
    h = _rms(x, g_ref[...])
    h_ref[...] = h.astype(BF16)
    logits = jnp.dot(h, wr_ref[...], precision=HIGHEST, preferred_element_type=F32) + br_ref[...]
    lane = lax.broadcasted_iota(I32, logits.shape, 1)
    work = logits
    vals, hots = [], []
    for _ in range(TOP_K):
        m = jnp.max(work, axis=1, keepdims=True)
        idx = jnp.min(jnp.where(work == m, lane, N_EXPERTS), axis=1, keepdims=True)
        hot = lane == idx
        vals.append(m)
        hots.append(hot)
        work = jnp.where(hot, -jnp.inf, work)
    es = [jnp.exp(v - vals[0]) for v in vals]
    den = es[0] + es[1] + es[2] + es[3]
    gm = jnp.zeros(logits.shape, F32)
    for e, hot in zip(es, hots):
        gm = jnp.where(hot, e / den, gm)
    gm_ref[...] = gm


def _outproj(x, oa, ob, oc, wo, g, wr, br):
    nt = x.shape[0]
    tile = lambda n: pl.BlockSpec((TOK_TILE, n), lambda i: (i, 0))
    full = lambda a: pl.BlockSpec(a.shape, lambda i: (0, 0))
    return pl.pallas_call(
        _outproj_kernel,
        grid=(nt // TOK_TILE,),
        in_specs=[tile(D_MODEL), tile(A_WIDTH), tile(B_WIDTH), tile(C_WIDTH), full(wo), full(g), full(wr), full(br)],
        out_specs=[tile(D_MODEL), tile(D_MODEL), tile(N_EXPERTS)],
        out_shape=[jax.ShapeDtypeStruct((nt, D_MODEL), F32), jax.ShapeDtypeStruct((nt, D_MODEL), BF16),
                   jax.ShapeDtypeStruct((nt, N_EXPERTS), F32)],
        compiler_params=_cparams(("parallel",)),
        name="outproj",
    )(x, oa, ob, oc, wo, g, wr, br)


MOE_GROUP = 1536
MOE_WINDOW = 240


def _moe_rank_kernel(gm_ref, su_ref, eye_ref, rank_ref, gate_ref, cnt_ref, carry_scr, *, tiles_per_group):
    i = pl.program_id(0)

    @pl.when(i % tiles_per_group == 0)
    def _():
        carry_scr[...] = jnp.zeros_like(carry_scr)

    gm = gm_ref[...]
    sel = jnp.where(gm > 0.0, 1.0, 0.0).astype(BF16)
    before = lax.dot_general(sel, su_ref[...], TN_DIMS, preferred_element_type=F32)
    sel_t = lax.dot_general(sel, eye_ref[...], TN_DIMS, preferred_element_type=F32)
    gate_ref[...] = _dot3(gm, eye_ref[...].astype(F32), TN_DIMS)
    carry = carry_scr[...]
    rank_ref[...] = jnp.where(sel_t > 0.5, carry[:, 0:1] + before, -1.0)
    carry = carry + jnp.sum(sel_t, axis=1, keepdims=True)
    carry_scr[...] = carry
    cnt_ref[0] = carry


def _moe_rank(gm):
    nt = gm.shape[0]
    tile = TOK_TILE
    tpg = MOE_GROUP // tile
    idx = np.arange(tile)
    su = jnp.asarray((idx[:, None] < idx[None, :]).astype(np.float32), BF16)
    eye = jnp.asarray(np.eye(tile, dtype=np.float32), BF16)
    return pl.pallas_call(
        functools.partial(_moe_rank_kernel, tiles_per_group=tpg),
        grid=(nt // tile,),
        in_specs=[pl.BlockSpec((tile, N_EXPERTS), lambda i: (i, 0)),
                  pl.BlockSpec((tile, tile), lambda i: (0, 0)), pl.BlockSpec((tile, tile), lambda i: (0, 0))],
        out_specs=[pl.BlockSpec((N_EXPERTS, tile), lambda i: (0, i)), pl.BlockSpec((N_EXPERTS, tile), lambda i: (0, i)),
                   pl.BlockSpec((1, N_EXPERTS, LANES), lambda i: (i // tpg, 0, 0))],
        out_shape=[jax.ShapeDtypeStruct((N_EXPERTS, nt), F32), jax.ShapeDtypeStruct((N_EXPERTS, nt), F32),
                   jax.ShapeDtypeStruct((nt // MOE_GROUP, N_EXPERTS, LANES), F32)],
        scratch_shapes=[pltpu.VMEM((N_EXPERTS, LANES), F32)],
        compiler_params=_cparams(("arbitrary",)),
        name="moe_rank",
    )(gm, su, eye)


def _one_hot_rows(rank_row, first_row):
    want = (first_row + lax.broadcasted_iota(I32, (MOE_WINDOW, 1), 0)).astype(F32)
    return jnp.where(rank_row == want, 1.0, 0.0)


def _moe_expert_kernel(ie_ref, ij_ref, io_ref, is_ref, iv_ref, h_ref, rank_ref, gate_ref, w1_ref, b1_ref, w2_ref,
                       b2_ref, y_ref, w1b_scr, w2b_scr):
    i = pl.program_id(0)

    @pl.when((i == 0) | (ie_ref[i] != ie_ref[jnp.maximum(i - 1, 0)]))
    def _():
        w1b_scr[...] = w1_ref[0, 0].astype(BF16)
        w2b_scr[...] = w2_ref[0, 0].astype(BF16)

    @pl.when(iv_ref[i] > 0)
    def _():
        hot = _one_hot_rows(rank_ref[0], is_ref[i] * MOE_WINDOW)
        rows = jnp.dot(hot.astype(BF16), h_ref[...], preferred_element_type=F32).astype(BF16)
        gate = jnp.sum(hot * gate_ref[0], axis=1, keepdims=True)
        u = jnp.dot(rows, w1b_scr[...], preferred_element_type=F32) + b1_ref[0, 0]
        glu = jnp.minimum(u[:, :D_FF], SWIGLU_LIMIT)
        lin = jnp.clip(u[:, D_FF:], -SWIGLU_LIMIT, SWIGLU_LIMIT)
        act = (glu * _sigmoid(SWIGLU_ALPHA * glu) * (lin + 1.0)).astype(BF16)
        out = jnp.dot(act, w2b_scr[...], preferred_element_type=F32) + b2_ref[0, 0]
        y_ref[...] = (out * gate).astype(BF16)


def _moe_combine_kernel(ie_ref, ij_ref, io_ref, is_ref, iv_ref, y_ref, rank_ref, x_ref, gf_ref, o_ref, acc_scr,
                        *, n_items, final):
    i = pl.program_id(0)
    valid = iv_ref[i] > 0
    nxt = jnp.minimum(i + 1, n_items - 1)

    @pl.when((i == 0) | (ij_ref[i] != ij_ref[jnp.maximum(i - 1, 0)]))
    def _():
        acc_scr[...] = jnp.zeros_like(acc_scr)

    @pl.when(valid)
    def _():
        hot = _one_hot_rows(rank_ref[0], is_ref[i] * MOE_WINDOW)
        acc_scr[...] += lax.dot_general(hot.astype(BF16), y_ref[...], TN_DIMS, preferred_element_type=F32)

    @pl.when(valid & ((i == n_items - 1) | (ij_ref[nxt] != ij_ref[i]) | (iv_ref[nxt] == 0)))
    def _():
        x = x_ref[...] + acc_scr[...]
        if final:
            x = _rms(x, gf_ref[...])
        o_ref[...] = x


def _moe_items(cnt, order_by_group):
    n_g, n_e = cnt.shape
    n_items = n_g * n_e + -(-(n_g * MOE_GROUP * TOP_K) // MOE_WINDOW)
    max_win = -(-MOE_GROUP // MOE_WINDOW)
    c = cnt.T
    n_win = (c + MOE_WINDOW - 1) // MOE_WINDOW
    seg_off = (jnp.cumsum(n_win.reshape(-1)) - n_win.reshape(-1)).reshape(n_e, n_g)
    e_id, g_id, s_id = jnp.meshgrid(jnp.arange(n_e), jnp.arange(n_g), jnp.arange(max_win), indexing="ij")
    ok = s_id < n_win[:, :, None]
    off = seg_off[:, :, None] + s_id
    if order_by_group:
        e_id, g_id, s_id, ok, off = (jnp.swapaxes(t, 0, 1) for t in (e_id, g_id, s_id, ok, off))
    flat = [t.reshape(-1).astype(I32) for t in (e_id, g_id, off, s_id, ok)]
    pick = jnp.argsort(1 - flat[4], stable=True)[:n_items]
    arrs = [t[pick] for t in flat]
    last = jnp.maximum(jnp.sum(arrs[4]) - 1, 0)
    e_a, g_a, o_a, s_a, v_a = arrs
    keep = v_a > 0
    return tuple(jnp.where(keep, t, t[last]) for t in (e_a, g_a, o_a, s_a)) + (v_a,), n_items


def _moe_routed(h, gm, x, w1, b1, w2, b2, gf, layer, final):
    nt = x.shape[0]
    n_g = nt // MOE_GROUP
    rank_t, gate_t, cnt3 = _moe_rank(gm)
    cnt = cnt3[:, :, 0].astype(I32)
    rank3 = rank_t[:, None, :]
    gate3 = gate_t[:, None, :]
    items_e, n_items = _moe_items(cnt, False)
    win_spec = pl.BlockSpec((MOE_WINDOW, D_MODEL), lambda i, ie, ij, io, is_, iv: (io[i], 0))
    per_eg = pl.BlockSpec((1, 1, MOE_GROUP), lambda i, ie, ij, io, is_, iv: (ie[i], 0, ij[i]))
    grp = lambda n: pl.BlockSpec((MOE_GROUP, n), lambda i, ie, ij, io, is_, iv: (ij[i], 0))
    wspec = lambda a, b: pl.BlockSpec((1, 1, a, b), lambda i, ie, ij, io, is_, iv: (layer, ie[i], 0, 0))
    y = pl.pallas_call(
        _moe_expert_kernel,
        grid_spec=pltpu.PrefetchScalarGridSpec(
            num_scalar_prefetch=5, grid=(n_items,),
            in_specs=[grp(D_MODEL), per_eg, per_eg, wspec(D_MODEL, 2 * D_FF), wspec(1, 2 * D_FF),
                      wspec(D_FF, D_MODEL), wspec(1, D_MODEL)],
            out_specs=win_spec,
            scratch_shapes=[pltpu.VMEM((D_MODEL, 2 * D_FF), BF16), pltpu.VMEM((D_FF, D_MODEL), BF16)]),
        out_shape=jax.ShapeDtypeStruct((n_items * MOE_WINDOW, D_MODEL), BF16),
        compiler_params=_cparams(("arbitrary",)),
        name="moe_expert",
    )(*items_e, h, rank3, gate3, w1, b1[:, :, None, :], w2, b2[:, :, None, :])
    items_g, _ = _moe_items(cnt, True)
    return pl.pallas_call(
        functools.partial(_moe_combine_kernel, n_items=n_items, final=final),
        grid_spec=pltpu.PrefetchScalarGridSpec(
            num_scalar_prefetch=5, grid=(n_items,),
            in_specs=[win_spec, per_eg, grp(D_MODEL),
                      pl.BlockSpec((1, D_MODEL), lambda i, ie, ij, io, is_, iv: (0, 0))],
            out_specs=grp(D_MODEL),
            scratch_shapes=[pltpu.VMEM((MOE_GROUP, D_MODEL), F32)]),
        out_shape=jax.ShapeDtypeStruct((nt, D_MODEL), F32),
        compiler_params=_cparams(("arbitrary",)),
        name="moe_combine",
    )(*items_g, y, rank3, x, gf)


def _rel_bucket(dist):
    n = jnp.maximum(dist, 0)
    max_exact = N_BUCKETS // 2
    log_ratio = jnp.log(jnp.maximum(n, 1).astype(jnp.float32) / max_exact) / math.log(MAX_DISTANCE / max_exact)
    large = jnp.minimum(max_exact + (log_ratio * (N_BUCKETS - max_exact)).astype(jnp.int32), N_BUCKETS - 1)
    return jnp.where(n < max_exact, n, large)


def _bias_delta(rel_bias, dist):
    rb = rel_bias.astype(F32)
    d = jnp.dot(jax.nn.one_hot(_rel_bucket(dist), N_BUCKETS, dtype=F32), rb, precision=HIGHEST) - rb[N_BUCKETS - 1]
    d = jnp.where(((dist >= 0) & (dist < MAX_DISTANCE))[..., None], d, 0.0)
    return jnp.moveaxis(d, -1, 0)


def _group_rows(delta, rows):
    return delta.reshape(B_KV_HEADS, B_REP * rows, delta.shape[-1])


def kernel(x_prompt, x_sample, cache_k, cache_v, cache_kidx, state_wkv, state_shift, state_pool, page_table,
           meta_tokens, rel_bias, norm_mix, norm_ffn, norm_final, w_in, w_out,
           a_mu, a_w0, a_w2, a_a0, a_a2, a_g2, a_kk, a_ka, a_rk, a_ln_w, a_ln_b,
           c_w, c_scale, moe_w_router, moe_b_router, moe_w1, moe_b1, moe_w2, moe_b2):
    assert x_prompt.shape[0] == 1
    depth = w_in.shape[0]
    seq = x_prompt.shape[1]
    nb, dec_seq = x_sample.shape[:2]
    assert dec_seq <= SROWS
    t_p = seq + N_META
    n_s = nb * dec_seq
    n_qblk = -(-t_p // Q_TILE)
    nkp = KEY_CHUNK * ((n_qblk + 4) // 4 + 1)
    assert MOE_GROUP % TOK_TILE == 0
    nt = _round_up(max(t_p + n_s, Q_TILE * n_qblk), MOE_GROUP)
    n_pages = page_table.shape[1]
    past = n_pages * PAGE_SIZE
    topk_p = min(TOPK_MAX, seq // 4)
    topk_s = min(TOPK_MAX, (past + dec_seq) // 4)
    n_phys = cache_k.shape[1]

    x = jnp.concatenate([meta_tokens.astype(F32), x_prompt[0], x_sample.reshape(n_s, D_MODEL),
                         jnp.zeros((nt - t_p - n_s, D_MODEL), F32)], axis=0)

    ones_blk = jnp.kron(jnp.eye(A_HEADS, dtype=F32), jnp.ones((HEAD_DIM, HEAD_DIM), F32))
    qi_ = jnp.arange(Q_TILE)[:, None]
    cj_ = jnp.arange(2 * Q_TILE)[None, :]
    side = TAIL_WIDTH - 2 * Q_TILE
    delta_p = jnp.pad(_bias_delta(rel_bias, Q_TILE + qi_ - cj_), ((0, 0), (0, 0), (side, side)))
    ts_ = jnp.arange(SROWS)[:, None]
    cs_ = jnp.arange(PAGE_SIZE)[None, :]
    delta_s = jnp.stack([_group_rows(_bias_delta(rel_bias, ts_ + PAGE_SIZE - cs_), SROWS),
                         _group_rows(_bias_delta(rel_bias, ts_ - cs_), SROWS)])

    outs = {n: [] for n in ("kp", "vp", "kip", "wkvp", "shp", "plp", "ks", "vs", "kis", "wkvs", "shs", "pls")}
    for l in range(depth):
        wl = w_in[l]
        b0 = A_PROJ
        w_cat = jnp.concatenate([
            wl[:, :A_PROJ], wl[:, b0:b0 + B_WIDTH + 2 * KV_WIDTH + IDX_HEADS * IDX_DIM],
            wl[:, b0 + B_PROJ - IDX_DIM - IDX_HEADS:b0 + B_PROJ],
            jnp.zeros((D_MODEL, LANES - IDX_DIM - IDX_HEADS), F32), wl[:, b0 + B_PROJ:]], axis=1).astype(BF16)
        pa, q, k, v, qi, kw, pc = _inproj(x, norm_mix[l][None], w_cat)

        sl_p = slice(0, t_p)
        sl_s = slice(t_p, t_p + n_s)
        outs["kp"].append(k[sl_p].reshape(1, t_p, B_KV_HEADS, HEAD_DIM))
        outs["vp"].append(v[sl_p].reshape(1, t_p, B_KV_HEADS, HEAD_DIM))
        outs["kip"].append(kw[sl_p, :IDX_DIM][None])
        outs["ks"].append(k[sl_s].reshape(nb, dec_seq, B_KV_HEADS, HEAD_DIM))
        outs["vs"].append(v[sl_s].reshape(nb, dec_seq, B_KV_HEADS, HEAD_DIM))
        outs["kis"].append(kw[sl_s, :IDX_DIM].reshape(nb, dec_seq, IDX_DIM))
        outs["shp"].append(pa[t_p - 1:t_p])
        pa_s = pa[sl_s].reshape(nb, dec_seq, A_PROJ)
        outs["shs"].append(pa_s[:, -1])
        pc_s = pc[sl_s].reshape(nb, dec_seq, C_WIDTH)
        outs["plp"].append(pc[t_p - POOL_BUF:t_p][None])
        pool_full = jnp.concatenate([state_pool[l], pc_s], axis=1)
        outs["pls"].append(pool_full[:, -POOL_BUF:])

        rw = (a_mu[l], a_w0[l], a_w2[l], a_a0[l], a_a2[l], a_g2[l], a_kk[l], a_ka[l], a_rk[l], a_ln_w[l], a_ln_b[l])
        oa_p, st_p = _rwkv_chunked(pa[None], jnp.zeros((1, 1, A_PROJ), F32),
                                   jnp.zeros((1, A_HEADS, HEAD_DIM, HEAD_DIM), F32),
                                   rw, ones_blk, t_p, RWKV_BLOCK, RWKV_CHUNK)
        pa_s8 = jnp.pad(pa_s, ((0, 0), (0, SROWS - dec_seq), (0, 0)))
        oa_s, st_s = _rwkv_chunked(pa_s8, state_shift[l][:, None, :], state_wkv[l].astype(F32),
                                   rw, ones_blk, dec_seq, SROWS, SROWS)
        outs["wkvp"].append(st_p.astype(state_wkv.dtype))
        outs["wkvs"].append(st_s.astype(state_wkv.dtype))

        n_real = min(nt, nkp - KEY_FRONT_PAD)
        front = lambda a: jnp.pad(a[:n_real].astype(BF16), ((KEY_FRONT_PAD, nkp - KEY_FRONT_PAD - n_real), (0, 0)))
        ob_p = _dsa_prompt(q, qi, kw, front(k), front(v), front(kw[:, :IDX_DIM]), delta_p, n_qblk, topk_p)
        rows8 = lambda a: jnp.pad(a[sl_s].reshape(nb, dec_seq, -1), ((0, 0), (0, SROWS - dec_seq), (0, 0)))
        page8 = lambda a: jnp.pad(a[sl_s].reshape(nb, dec_seq, -1), ((0, 0), (0, PAGE_SIZE - dec_seq), (0, 0)))
        kv_page = lambda a: page8(a).reshape(nb, PAGE_SIZE, B_KV_HEADS, HEAD_DIM).transpose(0, 2, 3, 1)
        ob_s = _dsa_sample(page_table, rows8(q), rows8(qi), rows8(kw),
                           cache_kidx.transpose(0, 1, 3, 2), cache_k.transpose(0, 1, 3, 4, 2),
                           cache_v.transpose(0, 1, 3, 4, 2), l, page8(kw[:, :IDX_DIM]).transpose(0, 2, 1),
                           kv_page(k), kv_page(v), delta_s, dec_seq, topk_s)

        w_bd = jax.scipy.linalg.block_diag(*[c_w[l][gi] for gi in range(len(POOL_WINDOWS))]).astype(BF16)
        oc_p = _pool(pc[None], w_bd, c_scale[l][None], TOK_TILE, 0)
        pool_in = jnp.concatenate([jnp.zeros((nb, 1, C_WIDTH), F32), pool_full,
                                   jnp.zeros((nb, 2 * POOL_HALO - 1 - POOL_BUF - dec_seq, C_WIDTH), F32)], axis=1)
        oc_s = _pool(pool_in, w_bd, c_scale[l][None], POOL_HALO, 1)[:, POOL_HALO:POOL_HALO + dec_seq]

        def merge(p_rows, s_rows):
            w = p_rows.shape[-1]
            return jnp.concatenate([p_rows[:t_p], s_rows.reshape(n_s, w), jnp.zeros((nt - t_p - n_s, w), F32)], axis=0)

        oa = merge(oa_p[0], oa_s[:, :dec_seq])
        ob = merge(ob_p, ob_s[:, :dec_seq])
        oc = merge(oc_p[0], oc_s)
        x, h2, gm = _outproj(x, oa, ob, oc, w_out[l].astype(BF16), norm_ffn[l][None], moe_w_router[l],
                             moe_b_router[l][None])
        x = _moe_routed(h2, gm, x, moe_w1, moe_b1, moe_w2, moe_b2, norm_final[None], l, l == depth - 1)

    y_prompt = x[N_META:t_p][None]
    y_sample = x[t_p:t_p + n_s].reshape(nb, dec_seq, D_MODEL)
    st = lambda n: jnp.stack(outs[n])
    return (y_prompt, y_sample, st("kp"), st("vp"), st("kip"), st("wkvp"), st("shp"), st("plp"),
            st("ks"), st("vs"), st("kis"), st("wkvs"), st("shs"), st("pls"))
```

```python
import functools
import math

import jax
import jax.numpy as jnp
import numpy as np
from jax import lax
from jax.experimental import pallas as pl
from jax.experimental.pallas import tpu as pltpu

F32 = jnp.float32
BF16 = jnp.bfloat16
I32 = jnp.int32
HIGHEST = lax.Precision.HIGHEST

D_MODEL = 1024
N_META = 16
HEAD_DIM = 64
A_WIDTH = 384
A_HEADS = 6
LORA_W, LORA_A, LORA_G = 64, 64, 128
A_PROJ = 3 * A_WIDTH + LORA_W + LORA_A + LORA_G
GN_EPS = 64e-5
B_WIDTH = 384
B_HEADS = 6
B_KV_HEADS = 2
B_REP = 3
KV_WIDTH = 128
IDX_HEADS = 4
IDX_DIM = 64
IDX_SCALE = (IDX_HEADS * IDX_DIM) ** -0.5
TOPK_MAX = 256
B_PROJ = B_WIDTH + 2 * KV_WIDTH + IDX_HEADS * IDX_DIM + IDX_DIM + IDX_HEADS
C_WIDTH = 256
C_GROUP_DIM = 64
POOL_WINDOWS = (2, 4, 8, 16)
POOL_BUF = 15
N_BUCKETS = 32
MAX_DISTANCE = 128
N_EXPERTS = 32
TOP_K = 4
D_FF = 1024
SWIGLU_LIMIT = 7.0
SWIGLU_ALPHA = 1.702
RMS_EPS = 1e-5
PAGE_SIZE = 128

LANES = 128
SUBLANES = 8
Q_TILE = 128
KEY_CHUNK = 512
KEY_FRONT_PAD = 128
TILES_PER_CHUNK = KEY_CHUNK // Q_TILE
TAIL_WIDTH = KEY_CHUNK + Q_TILE
SCORE_CHUNKS = 4
TIE_CHUNKS = 2
FAR_WIDE = 4
TOP_M = 12
SAMPLE_PAGES = 32
TOK_TILE = 512
RWKV_BLOCK = 256
RWKV_CHUNK = 128
POOL_HALO = 16
VMEM_LIMIT = 56 * 1024 * 1024

INT_MIN = -(2 ** 31)
NEG_INF_KEY = np.int32(np.uint32(0xFF800000) ^ np.uint32(0x7FFFFFFF))
NEG_BIG = -1e30


def _round_up(x, m):
    return (x + m - 1) // m * m


def _cparams(sem):
    return pltpu.CompilerParams(dimension_semantics=sem, vmem_limit_bytes=VMEM_LIMIT)


def _rms(x, g):
    return x * lax.rsqrt(jnp.mean(x * x, axis=-1, keepdims=True) + RMS_EPS) * g


def _sigmoid(x):
    return 1.0 / (1.0 + jnp.exp(-x))


IN_COLS = (A_PROJ, B_WIDTH, KV_WIDTH, KV_WIDTH, IDX_HEADS * IDX_DIM, LANES, C_WIDTH)


def _inproj_kernel(x_ref, g_ref, w_ref, *out_refs):
    hb = _rms(x_ref[...], g_ref[...]).astype(BF16)
    off = 0
    for ref in out_refs:
        n = ref.shape[1]
        ref[...] = jnp.dot(hb, w_ref[:, off:off + n], preferred_element_type=F32)
        off += n


def _inproj(x, g, w):
    nt = x.shape[0]
    return pl.pallas_call(
        _inproj_kernel,
        grid=(nt // TOK_TILE,),
        in_specs=[pl.BlockSpec((TOK_TILE, D_MODEL), lambda i: (i, 0)),
                  pl.BlockSpec((1, D_MODEL), lambda i: (0, 0)),
                  pl.BlockSpec(w.shape, lambda i: (0, 0))],
        out_specs=[pl.BlockSpec((TOK_TILE, n), lambda i: (i, 0)) for n in IN_COLS],
        out_shape=[jax.ShapeDtypeStruct((nt, n), F32) for n in IN_COLS],
        compiler_params=_cparams(("parallel",)),
        name="inproj",
    )(x, g, w)


NN_DIMS = (((1,), (0,)), ((), ()))
NT_DIMS = (((1,), (1,)), ((), ()))
TN_DIMS = (((0,), (0,)), ((), ()))


def _dot3(a, b, dims=NN_DIMS):
    a_hi = a.astype(BF16)
    b_hi = b.astype(BF16)
    a_lo = (a - a_hi.astype(F32)).astype(BF16)
    b_lo = (b - b_hi.astype(F32)).astype(BF16)
    dg = functools.partial(lax.dot_general, dimension_numbers=dims, preferred_element_type=F32)
    return dg(a_hi, b_hi) + (dg(a_hi, b_lo) + dg(a_lo, b_hi))


def _rwkv_chunk_kernel(pa_ref, sh_ref, s0_ref, mu_ref, w0_ref, w2_ref, a0_ref, a2_ref, g2_ref, kk_ref, ka_ref,
                       rk_ref, lnw_ref, lnb_ref, ones_ref, tri_ref, o_ref, s_out_ref, carry_scr, s_scr,
                       *, seq_len, block, chunk):
    blk = pl.program_id(1)

    @pl.when(blk == 0)
    def _():
        carry_scr[...] = sh_ref[0]
        s_scr[...] = s0_ref[0]

    pa = pa_ref[0]
    row = lax.broadcasted_iota(I32, (block, 1), 0)
    prev = jnp.where(row == 0, carry_scr[...], pltpu.roll(pa, 1, axis=0))
    carry_scr[...] = pa[block - 1:block, :]
    xm = pa + (prev - pa) * mu_ref[...]
    r = xm[:, 0:A_WIDTH]
    k = xm[:, A_WIDTH:2 * A_WIDTH]
    v = xm[:, 2 * A_WIDTH:3 * A_WIDTH]
    lw = xm[:, 3 * A_WIDTH:3 * A_WIDTH + LORA_W]
    la = xm[:, 3 * A_WIDTH + LORA_W:3 * A_WIDTH + LORA_W + LORA_A]
    lg = xm[:, 3 * A_WIDTH + LORA_W + LORA_A:]

    def dot_hi(a, b):
        return jnp.dot(a, b, precision=HIGHEST, preferred_element_type=F32)

    z = -(w0_ref[...] + dot_hi(jnp.tanh(lw), w2_ref[...]))
    softplus = jnp.maximum(z, 0.0) + jnp.log(1.0 + jnp.exp(-jnp.abs(z)))
    log_w = -jnp.exp(-softplus - 0.5)
    a = _sigmoid(a0_ref[...] + dot_hi(la, a2_ref[...]))
    g = dot_hi(_sigmoid(lg), g2_ref[...])
    ones_blk = ones_ref[...]
    kk = k * kk_ref[...]
    kk = kk / jnp.maximum(jnp.sqrt(dot_hi(kk * kk, ones_blk)), 1e-12)
    k2 = k * (1.0 + (a - 1.0) * ka_ref[...])
    kka = kk * a

    valid = (blk * block + row) < seq_len
    log_w = jnp.where(valid, log_w, 0.0)
    kk_m, kka_m, k2_m, r_m, v_m = (jnp.where(valid, t, 0.0) for t in (kk, kka, k2, r, v))

    cum = dot_hi(tri_ref[0], log_w)
    cum_end = dot_hi(tri_ref[1], log_w)
    kap = kk_m * jnp.exp(cum - log_w)
    rt = r_m * jnp.exp(cum)
    inv_p = jnp.exp(-cum)
    kt = k2_m * inv_p
    bt = kka_m * inv_p
    to_end = jnp.exp(cum_end - cum)
    k_end = k2_m * to_end
    b_end = kka_m * to_end
    p_end = jnp.exp(cum_end)

    sub = min(16, chunk)
    ti = lax.broadcasted_iota(I32, (chunk, chunk), 0)
    tj = lax.broadcasted_iota(I32, (chunk, chunk), 1)
    heads = range(A_HEADS)
    o_chunks = []
    for c in range(block // chunk):
        rs = slice(c * chunk, (c + 1) * chunk)
        hsl = lambda t, h: t[rs, h * HEAD_DIM:(h + 1) * HEAD_DIM]
        lhs = [jnp.concatenate([hsl(kap, h), hsl(rt, h)], axis=0) for h in heads]
        rhs = [jnp.concatenate([hsl(kt, h), hsl(bt, h)], axis=0) for h in heads]
        vh = [hsl(v_m, h) for h in heads]
        gram = [_dot3(lhs[h], rhs[h], NT_DIMS) for h in heads]
        from_s = [_dot3(lhs[h], s_scr[h], NT_DIMS) for h in heads]
        a1 = [jnp.where(ti > tj, gram[h][:chunk, :chunk], 0.0) for h in heads]
        a2 = [jnp.where(ti > tj, gram[h][:chunk, chunk:], 0.0) for h in heads]
        a3 = [jnp.where(ti >= tj, gram[h][chunk:, :chunk], 0.0) for h in heads]
        a4 = [jnp.where(ti >= tj, gram[h][chunk:, chunk:], 0.0) for h in heads]
        u = [from_s[h][:chunk] + _dot3(a1[h], vh[h]) for h in heads]
        zs = [[] for _ in heads]
        for jb in range(chunk // sub):
            js = slice(jb * sub, (jb + 1) * sub)
            zj = []
            for h in heads:
                t = u[h][js]
                if jb > 0:
                    t = t - _dot3(a2[h][js, :jb * sub], jnp.concatenate(zs[h], axis=0))
                zj.append(t)
            for i in range(sub - 1):
                for h in heads:
                    col = jnp.broadcast_to(a2[h][js, jb * sub + i:jb * sub + i + 1], (sub, HEAD_DIM))
                    zj[h] = zj[h] - col * zj[h][i:i + 1, :]
            for h in heads:
                zs[h].append(zj[h])
        zh = [jnp.concatenate(zs[h], axis=0) for h in heads]
        vz = [jnp.concatenate([vh[h], zh[h]], axis=0) for h in heads]
        o_chunks.append(jnp.concatenate(
            [from_s[h][chunk:] + _dot3(jnp.concatenate([a3[h], -a4[h]], axis=1), vz[h]) for h in heads], axis=1))
        for h in heads:
            ends = jnp.concatenate([hsl(k_end, h), -hsl(b_end, h)], axis=0)
            s_scr[h] = s_scr[h] * hsl(p_end, h)[0:1, :] + _dot3(vz[h], ends, TN_DIMS)
    s_out_ref[0] = s_scr[...]

    o = jnp.concatenate(o_chunks, axis=0) if len(o_chunks) > 1 else o_chunks[0]
    inv = 1.0 / HEAD_DIM
    mean = dot_hi(o, ones_blk) * inv
    cen = o - mean
    var = dot_hi(cen * cen, ones_blk) * inv
    o = cen * lax.rsqrt(var + GN_EPS) * lnw_ref[...] + lnb_ref[...]
    bonus = dot_hi(r * k2 * rk_ref[...], ones_blk) * v
    o_ref[0] = (o + bonus) * g


def _rwkv_chunked(pa, shift_prev, s0, params, ones_blk, seq_len, block, chunk):
    n_seq, length, _ = pa.shape
    n_blk = -(-seq_len // block)
    idx = np.arange(block)
    same = (idx[:, None] // chunk) == (idx[None, :] // chunk)
    tri = jnp.asarray(np.stack([same & (idx[None, :] <= idx[:, None]), same]).astype(np.float32))
    row_spec = lambda n: pl.BlockSpec((1, n), lambda s, i: (0, 0))
    mat_spec = lambda a: pl.BlockSpec(a.shape, lambda s, i: (0,) * a.ndim)
    mu, w0, w2, a0, a2, g2, k_k, k_a, r_k, ln_w, ln_b = params
    kern = functools.partial(_rwkv_chunk_kernel, seq_len=seq_len, block=block, chunk=chunk)
    state_spec = pl.BlockSpec((1, A_HEADS, HEAD_DIM, HEAD_DIM), lambda s, i: (s, 0, 0, 0))
    return pl.pallas_call(
        kern,
        grid=(n_seq, n_blk),
        in_specs=[pl.BlockSpec((1, block, A_PROJ), lambda s, i: (s, i, 0)),
                  pl.BlockSpec((1, 1, A_PROJ), lambda s, i: (s, 0, 0)),
                  state_spec,
                  row_spec(A_PROJ), row_spec(A_WIDTH), mat_spec(w2), row_spec(A_WIDTH), mat_spec(a2),
                  mat_spec(g2), row_spec(A_WIDTH), row_spec(A_WIDTH), row_spec(A_WIDTH), row_spec(A_WIDTH),
                  row_spec(A_WIDTH), mat_spec(ones_blk), mat_spec(tri)],
        out_specs=[pl.BlockSpec((1, block, A_WIDTH), lambda s, i: (s, i, 0)), state_spec],
        out_shape=[jax.ShapeDtypeStruct((n_seq, length, A_WIDTH), F32),
                   jax.ShapeDtypeStruct((n_seq, A_HEADS, HEAD_DIM, HEAD_DIM), F32)],
        scratch_shapes=[pltpu.VMEM((1, A_PROJ), F32), pltpu.VMEM((A_HEADS, HEAD_DIM, HEAD_DIM), F32)],
        compiler_params=_cparams(("arbitrary", "arbitrary")),
        name="rwkv",
    )(pa, shift_prev, s0, mu[None], w0[None], w2, a0[None], a2, g2, k_k[None], k_a[None], r_k[None],
      ln_w[None], ln_b[None], ones_blk, tri)


def _sortable_key(s):
    bits = pltpu.bitcast(s + 0.0, I32)
    return bits ^ ((bits >> 31) & jnp.int32(0x7FFFFFFF))


def _index_scores(qi_b, wi, ki_b, transposed=False):
    s = None
    for h in range(IDX_HEADS):
        d = lax.dot_general(qi_b[:, h * IDX_DIM:(h + 1) * IDX_DIM], ki_b,
                            (((1,), (0 if transposed else 1,)), ((), ())), preferred_element_type=F32)
        t = jnp.maximum(d, 0.0) * wi[:, h:h + 1]
        s = t if s is None else s + t
    return s


def _topk_threshold(count_fn, count_tie_fn, rows, topk):
    kf = jnp.float32(topk)
    c0 = count_fn(jnp.zeros((rows, 1), I32), False)
    prefix0 = jnp.where(c0 >= kf, jnp.int32(0), jnp.int32(INT_MIN))

    def bit_step(i, prefix):
        cand = prefix | lax.shift_left(jnp.int32(1), jnp.int32(30) - i)
        return jnp.where(count_fn(cand, False) >= kf, cand, prefix)

    tau = lax.fori_loop(0, 31, bit_step, prefix0)
    c_gt = count_fn(tau, True)
    c_ge = count_fn(tau, False)
    need = kf - c_gt
    excess = jnp.max(c_ge - c_gt - need)

    def tie_search(_):
        def jbit(i, jl):
            cand = jl | lax.shift_left(jnp.int32(1), jnp.int32(14) - i)
            return jnp.where(count_tie_fn(tau, cand) < need, cand, jl)
        return lax.fori_loop(0, 15, jbit, jnp.zeros((rows, 1), I32))

    jlim = lax.cond(excess > 0.0, tie_search, lambda _: jnp.full((rows, 1), 2 ** 30, I32), 0)
    return tau, jlim


def _select(keys, aidx, tau, jlim):
    return ((keys > tau) | ((keys == tau) & (aidx <= jlim))) & (keys > jnp.int32(NEG_INF_KEY))


def _dsa_prompt_kernel(q_ref, qi_ref, kw_ref, kp_ref, vp_ref, kip_ref, dl_ref, o_ref, key_scr, cand_scr, *, topk):
    j = pl.program_id(0)
    rows = Q_TILE
    n_chunks = (j + 2 + TILES_PER_CHUNK - 1) // TILES_PER_CHUNK
    qi_b = qi_ref[...].astype(BF16)
    wi = kw_ref[:, IDX_DIM:IDX_DIM + IDX_HEADS] * IDX_SCALE
    qpos = j * Q_TILE + lax.broadcasted_iota(I32, (rows, 1), 0)
    lane_c = lax.broadcasted_iota(I32, (1, KEY_CHUNK), 1)

    def score_chunk(c, masked):
        a0 = pl.multiple_of(c * KEY_CHUNK, KEY_CHUNK)
        s = _index_scores(qi_b, wi, kip_ref[pl.ds(a0, KEY_CHUNK), :])
        if masked:
            kpos = a0 - KEY_FRONT_PAD + lane_c
            s = jnp.where((kpos <= qpos) & (kpos >= 0), s, -jnp.inf)
        key_scr[:, pl.ds(a0, KEY_CHUNK)] = _sortable_key(s)

    n_plain = jnp.clip((j * Q_TILE + KEY_FRONT_PAD - KEY_CHUNK) // KEY_CHUNK, 0, n_chunks - 1)
    score_chunk(0, True)

    def plain_group(p, x):
        for i in range(SCORE_CHUNKS):
            score_chunk(1 + SCORE_CHUNKS * p + i, False)
        return x

    lax.fori_loop(0, n_plain // SCORE_CHUNKS, plain_group, 0)
    lax.fori_loop(1 + SCORE_CHUNKS * (n_plain // SCORE_CHUNKS), n_chunks, lambda c, x: (score_chunk(c, True), x)[1], 0)

    key_scr[:, pl.ds(pl.multiple_of(n_chunks * KEY_CHUNK, KEY_CHUNK), KEY_CHUNK)] = jnp.full(
        (rows, KEY_CHUNK), NEG_INF_KEY, I32)

    n_fold = KEY_CHUNK // LANES

    def chunk_at(ref, c):
        return ref[:, pl.ds(pl.multiple_of(c * KEY_CHUNK, KEY_CHUNK), KEY_CHUNK)]

    def count32(cand, strict):
        def body(c, acc):
            kk = chunk_at(key_scr, c)
            hit = (kk > cand) if strict else (kk >= cand)
            m = jnp.where(hit, 1.0, 0.0)
            for i in range(n_fold):
                acc = acc + m[:, i * LANES:(i + 1) * LANES]
            return acc
        acc = lax.fori_loop(0, n_chunks, body, jnp.zeros((rows, LANES), F32))
        return jnp.sum(acc, axis=1, keepdims=True)

    kf = jnp.float32(topk)

    def kth_largest(count_ge):
        def bit_step(i, prefix):
            cand = prefix | lax.shift_left(jnp.int32(1), jnp.int32(30) - i)
            return jnp.where(count_ge(cand) >= kf, cand, prefix)
        nonneg = count_ge(jnp.zeros((rows, 1), I32)) >= kf
        return lax.fori_loop(0, 31, bit_step, jnp.where(nonneg, jnp.int32(0), jnp.int32(INT_MIN)))

    pair = 2 * SUBLANES
    for rp in range(rows // pair):
        r0 = rp * pair

        def insert_chunk(c, lists, r0=r0):
            a0 = pl.multiple_of(c * KEY_CHUNK, KEY_CHUNK)
            blk = key_scr[r0:r0 + pair, pl.ds(a0, KEY_CHUNK)]
            blk = pltpu.bitcast(blk ^ ((blk >> 31) & jnp.int32(0x7FFFFFFF)), F32)
            lists = list(lists)
            for half in range(2):
                for t in range(n_fold):
                    x = blk[half * SUBLANES:(half + 1) * SUBLANES, t * LANES:(t + 1) * LANES]
                    for i in range(TOP_M):
                        a = lists[half * TOP_M + i]
                        lists[half * TOP_M + i] = jnp.maximum(a, x)
                        x = jnp.minimum(a, x)
            return tuple(lists)

        lists = lax.fori_loop(0, n_chunks, insert_chunk,
                              tuple(jnp.full((SUBLANES, LANES), -jnp.inf, F32) for _ in range(2 * TOP_M)))
        for half in range(2):
            for i in range(TOP_M):
                cand_scr[r0 + half * SUBLANES:r0 + (half + 1) * SUBLANES, i * LANES:(i + 1) * LANES] = _sortable_key(
                    lists[half * TOP_M + i])

    def count_cand(cand):
        acc = jnp.zeros((rows, LANES), F32)
        for i in range(TOP_M):
            acc = acc + jnp.where(cand_scr[:, i * LANES:(i + 1) * LANES] >= cand, 1.0, 0.0)
        return jnp.sum(acc, axis=1, keepdims=True)

    tau_c = kth_largest(count_cand)
    gt_c = count32(tau_c, True)
    ge_c = count32(tau_c, False)
    wrong = jnp.max(jnp.where((gt_c < kf) & (ge_c >= kf), 0.0, 1.0))

    def full_search(_):
        t = kth_largest(lambda cand: count32(cand, False))
        return t, count32(t, True), count32(t, False)

    tau, c_gt, c_ge = lax.cond(wrong > 0.0, full_search, lambda _: (tau_c, gt_c, ge_c), 0)
    need = kf - c_gt
    excess = jnp.max(c_ge - c_gt - need)

    def tie_pass(_):
        ri = lax.broadcasted_iota(I32, (LANES, 2 * LANES), 0)
        ci = lax.broadcasted_iota(I32, (LANES, 2 * LANES), 1)
        tri = jnp.where((ri <= ci) | (ci >= LANES), 1.0, 0.0).astype(BF16)

        span = TIE_CHUNKS * KEY_CHUNK
        n_tile = span // LANES

        def body(c, carry):
            seen, jacc = carry
            a0 = pl.multiple_of(c * span, span)
            kk = key_scr[:, pl.ds(a0, span)]
            ties = [kk[:, i * LANES:(i + 1) * LANES] == tau for i in range(n_tile)]
            pres = [jnp.dot(jnp.where(t, 1.0, 0.0).astype(BF16), tri, preferred_element_type=F32) for t in ties]
            for i in range(n_tile):
                rank = jnp.where(ties[i], seen + pres[i][:, :LANES], 1e9)
                aidx = a0 + i * LANES + lax.broadcasted_iota(I32, (1, LANES), 1)
                jacc = jnp.maximum(jacc, jnp.where(rank <= need, aidx, -1))
                seen = seen + pres[i][:, LANES:]
            return seen, jacc
        _, jacc = lax.fori_loop(0, (n_chunks + TIE_CHUNKS - 1) // TIE_CHUNKS, body,
                                (jnp.zeros((rows, LANES), F32), jnp.full((rows, LANES), -1, I32)))
        return jnp.max(jacc, axis=1, keepdims=True)

    jlim = lax.cond(excess > 0.0, tie_pass, lambda _: jnp.full((rows, 1), 2 ** 30, I32), 0)
    jlim = jnp.where(tau == jnp.int32(NEG_INF_KEY), jnp.int32(-1), jlim)
    tau_b = jnp.broadcast_to(tau, (rows, LANES))
    tau_m1_b = tau_b - 1

    q = q_ref[...] * (HEAD_DIM ** -0.5)
    qh = [q[:, h * HEAD_DIM:(h + 1) * HEAD_DIM].astype(BF16) for h in range(B_HEADS)]

    def init_state():
        return (jnp.full((rows, 1), NEG_BIG, F32), jnp.zeros((rows, 1), F32), jnp.zeros((rows, HEAD_DIM), F32))

    def attend(a0, width, states, delta):
        kk = key_scr[:, pl.ds(a0, width)]
        aidx = a0 + lax.broadcasted_iota(I32, (1, width), 1)
        thr = jnp.where(aidx <= jlim, jnp.concatenate([tau_m1_b] * (width // LANES), axis=1),
                        jnp.concatenate([tau_b] * (width // LANES), axis=1))
        madd = jnp.where(kk > thr, 0.0, NEG_BIG)
        kts = [kp_ref[pl.ds(a0, width), g * HEAD_DIM:(g + 1) * HEAD_DIM] for g in range(B_KV_HEADS)]
        vts = [vp_ref[pl.ds(a0, width), g * HEAD_DIM:(g + 1) * HEAD_DIM] for g in range(B_KV_HEADS)]
        lgs = [lax.dot_general(qh[h], kts[h // B_REP], (((1,), (1,)), ((), ())), preferred_element_type=F32)
               for h in range(B_HEADS)]
        mid = []
        for h in range(B_HEADS):
            m, l, _ = states[h]
            lg = lgs[h] + madd
            if delta is not None:
                lg = lg + delta[h]
            m_new = jnp.maximum(m, jnp.max(lg, axis=1, keepdims=True))
            alpha = jnp.exp(m - m_new)
            p = jnp.exp(lg - m_new)
            mid.append((m_new, alpha, alpha * l + jnp.sum(p, axis=1, keepdims=True), p.astype(BF16)))
        out = []
        for h in range(B_HEADS):
            m_new, alpha, l, pb = mid[h]
            acc = alpha * states[h][2] + jnp.dot(pb, vts[h // B_REP], preferred_element_type=F32)
            out.append((m_new, l, acc))
        return tuple(out)

    def far_wide(c, states):
        return attend(pl.multiple_of(c * FAR_WIDE * KEY_CHUNK, FAR_WIDE * KEY_CHUNK), FAR_WIDE * KEY_CHUNK, states, None)

    def far_chunk(c, states):
        return attend(pl.multiple_of(c * KEY_CHUNK, KEY_CHUNK), KEY_CHUNK, states, None)

    n_far = j // TILES_PER_CHUNK
    states = lax.fori_loop(0, n_far // FAR_WIDE, far_wide, tuple(init_state() for _ in range(B_HEADS)))
    states = lax.fori_loop((n_far // FAR_WIDE) * FAR_WIDE, n_far, far_chunk, states)
    d0 = pl.multiple_of((TILES_PER_CHUNK - 1 - j % TILES_PER_CHUNK) * Q_TILE, Q_TILE)
    delta = [dl_ref[h, :, pl.ds(d0, TAIL_WIDTH)] for h in range(B_HEADS)]
    states = attend(pl.multiple_of(n_far * KEY_CHUNK, KEY_CHUNK), TAIL_WIDTH, states, delta)

    o_ref[...] = jnp.concatenate([acc / l for _, l, acc in states], axis=1)


def _dsa_prompt(q, qi, kw, kp, vp, kip, delta, n_qblk, topk):
    nkp = kp.shape[0]
    full = lambda a: pl.BlockSpec(a.shape, lambda j: (0,) * a.ndim)
    return pl.pallas_call(
        functools.partial(_dsa_prompt_kernel, topk=topk),
        grid=(n_qblk,),
        in_specs=[pl.BlockSpec((Q_TILE, B_WIDTH), lambda j: (j, 0)),
                  pl.BlockSpec((Q_TILE, IDX_HEADS * IDX_DIM), lambda j: (j, 0)),
                  pl.BlockSpec((Q_TILE, LANES), lambda j: (j, 0)),
                  full(kp), full(vp), full(kip), full(delta)],
        out_specs=pl.BlockSpec((Q_TILE, B_WIDTH), lambda j: (j, 0)),
        out_shape=jax.ShapeDtypeStruct((n_qblk * Q_TILE, B_WIDTH), F32),
        scratch_shapes=[pltpu.VMEM((Q_TILE, nkp), I32), pltpu.VMEM((Q_TILE, TOP_M * LANES), I32)],
        compiler_params=_cparams(("arbitrary",)),
        name="dsa_prompt",
    )(q, qi, kw, kp, vp, kip, delta)


SROWS = 8


def _dsa_sample_score_kernel(pt_ref, qi_ref, kw_ref, *refs, n_steps, pages, dec_seq):
    cki_refs, kin_ref, key_ref = refs[:pages], refs[pages], refs[pages + 1]
    p = pl.program_id(1)
    qi_b = qi_ref[0].astype(BF16)
    wi = kw_ref[0][:, IDX_DIM:IDX_DIM + IDX_HEADS] * IDX_SCALE
    t = lax.broadcasted_iota(I32, (SROWS, 1), 0)
    c = lax.broadcasted_iota(I32, (1, PAGE_SIZE), 1)

    @pl.when(p < n_steps)
    def _():
        ki_all = jnp.concatenate([r[0, 0].astype(BF16) for r in cki_refs], axis=1)
        s = _index_scores(qi_b, wi, ki_all, transposed=True)
        key_ref[0] = _sortable_key(jnp.where(t < dec_seq, s, -jnp.inf))

    @pl.when(p == n_steps)
    def _():
        s = _index_scores(qi_b, wi, kin_ref[0].astype(BF16), transposed=True)
        key_ref[0, :, 0:PAGE_SIZE] = _sortable_key(jnp.where((t < dec_seq) & (c <= t), s, -jnp.inf))
        if pages > 1:
            key_ref[0, :, PAGE_SIZE:] = jnp.full((SROWS, (pages - 1) * PAGE_SIZE), NEG_INF_KEY, I32)


def _dsa_sample_attend_kernel(pt_ref, q_ref, key_ref, *refs, n_steps, pages, topk):
    ck_refs, cv_refs = refs[:pages], refs[pages:2 * pages]
    kn_ref, vn_ref, dl_ref, o_ref, tau_scr, jl_scr, m_scr, l_scr, acc_scr = refs[2 * pages:]
    p = pl.program_id(1)
    width = pages * PAGE_SIZE

    @pl.when(p == 0)
    def _():
        keys = key_ref[0]
        aidx = lax.broadcasted_iota(I32, (1, keys.shape[1]), 1)

        def count_fn(cand, strict):
            hit = (keys > cand) if strict else (keys >= cand)
            return jnp.sum(jnp.where(hit, 1.0, 0.0), axis=1, keepdims=True)

        def count_tie_fn(tau, jl):
            return jnp.sum(jnp.where((keys == tau) & (aidx < jl), 1.0, 0.0), axis=1, keepdims=True)

        tau, jlim = _topk_threshold(count_fn, count_tie_fn, SROWS, topk)
        tau_scr[...] = tau
        jl_scr[...] = jlim
        m_scr[...] = jnp.full(m_scr.shape, NEG_BIG, F32)
        l_scr[...] = jnp.zeros(l_scr.shape, F32)
        acc_scr[...] = jnp.zeros(acc_scr.shape, F32)

    a0 = pl.multiple_of(p * width, width)
    kk = key_ref[0, :, pl.ds(a0, width)]
    aidx = a0 + lax.broadcasted_iota(I32, (1, width), 1)
    sel = _select(kk, aidx, tau_scr[...], jl_scr[...])
    sel3 = jnp.concatenate([sel] * B_REP, axis=0)
    q = q_ref[0] * (HEAD_DIM ** -0.5)

    def cached(ref):
        return lambda g: ref[0, 0, g]

    def fresh(ref):
        return lambda g: ref[0, g]

    def run(k_tiles, v_tiles, delta):
        n = len(k_tiles)
        sel_n = sel3[:, :n * PAGE_SIZE]
        for g in range(B_KV_HEADS):
            qg = jnp.concatenate([q[:, (g * B_REP + r) * HEAD_DIM:(g * B_REP + r + 1) * HEAD_DIM]
                                  for r in range(B_REP)], axis=0).astype(BF16)
            k_all = jnp.concatenate([kt(g).astype(BF16) for kt in k_tiles], axis=1)
            lg = jnp.dot(qg, k_all, preferred_element_type=F32)
            if delta is not None:
                pad = [jnp.zeros((lg.shape[0], (n - 1) * PAGE_SIZE), F32)] if n > 1 else []
                lg = lg + jnp.concatenate(pad + [delta[g]], axis=1)
            lg = jnp.where(sel_n, lg, NEG_BIG)
            m = m_scr[g]
            m_new = jnp.maximum(m, jnp.max(lg, axis=1, keepdims=True))
            alpha = jnp.exp(m - m_new)
            pr = jnp.where(sel_n, jnp.exp(lg - m_new), 0.0).astype(BF16)
            v_all = jnp.concatenate([vt(g).astype(BF16) for vt in v_tiles], axis=1)
            acc = alpha * acc_scr[g] + lax.dot_general(pr, v_all, NT_DIMS, preferred_element_type=F32)
            m_scr[g] = m_new
            l_scr[g] = alpha * l_scr[g] + jnp.sum(pr.astype(F32), axis=1, keepdims=True)
            acc_scr[g] = acc

    ck_tiles = [cached(r) for r in ck_refs]
    cv_tiles = [cached(r) for r in cv_refs]

    @pl.when(p < n_steps - 1)
    def _():
        run(ck_tiles, cv_tiles, None)

    @pl.when(p == n_steps - 1)
    def _():
        run(ck_tiles, cv_tiles, (dl_ref[0, 0], dl_ref[0, 1]))

    @pl.when(p == n_steps)
    def _():
        run([fresh(kn_ref)], [fresh(vn_ref)], (dl_ref[1, 0], dl_ref[1, 1]))
        pieces = [None] * B_HEADS
        for g in range(B_KV_HEADS):
            og = acc_scr[g] / l_scr[g]
            for r in range(B_REP):
                pieces[g * B_REP + r] = og[r * SROWS:(r + 1) * SROWS, :]
        o_ref[0] = jnp.concatenate(pieces, axis=1)


def _dsa_sample(page_table, q_s, qi_s, kw_s, cki, ck, cv, layer, ki_new, k_new, v_new, delta_s, dec_seq, topk):
    nb = q_s.shape[0]
    n_pages = page_table.shape[1]
    pages = math.gcd(n_pages, SAMPLE_PAGES)
    n_steps = n_pages // pages
    width = pages * PAGE_SIZE
    n_keys = (n_steps + 1) * width

    def page(i):
        return lambda b, p, pt: (layer, pt[b, jnp.minimum(p, n_steps - 1) * pages + i], 0, 0)

    def page_kv(i):
        return lambda b, p, pt: (layer, pt[b, jnp.minimum(p, n_steps - 1) * pages + i], 0, 0, 0)

    per_b = lambda b, p, pt: (b, 0, 0)
    keys = pl.pallas_call(
        functools.partial(_dsa_sample_score_kernel, n_steps=n_steps, pages=pages, dec_seq=dec_seq),
        grid_spec=pltpu.PrefetchScalarGridSpec(
            num_scalar_prefetch=1, grid=(nb, n_steps + 1),
            in_specs=[pl.BlockSpec((1, SROWS, IDX_HEADS * IDX_DIM), per_b),
                      pl.BlockSpec((1, SROWS, LANES), per_b)]
            + [pl.BlockSpec((1, 1, IDX_DIM, PAGE_SIZE), page(i)) for i in range(pages)]
            + [pl.BlockSpec((1, IDX_DIM, PAGE_SIZE), per_b)],
            out_specs=pl.BlockSpec((1, SROWS, width), lambda b, p, pt: (b, 0, p))),
        out_shape=jax.ShapeDtypeStruct((nb, SROWS, n_keys), I32),
        compiler_params=_cparams(("arbitrary", "arbitrary")),
        name="dsa_sample_score",
    )(page_table, qi_s, kw_s, *([cki] * pages), ki_new)
    grows = B_REP * SROWS
    return pl.pallas_call(
        functools.partial(_dsa_sample_attend_kernel, n_steps=n_steps, pages=pages, topk=topk),
        grid_spec=pltpu.PrefetchScalarGridSpec(
            num_scalar_prefetch=1, grid=(nb, n_steps + 1),
            in_specs=[pl.BlockSpec((1, SROWS, B_WIDTH), per_b),
                      pl.BlockSpec((1, SROWS, n_keys), per_b)]
            + [pl.BlockSpec((1, 1, B_KV_HEADS, HEAD_DIM, PAGE_SIZE), page_kv(i)) for i in range(pages)] * 2
            + [pl.BlockSpec((1, B_KV_HEADS, HEAD_DIM, PAGE_SIZE), lambda b, p, pt: (b, 0, 0, 0)),
               pl.BlockSpec((1, B_KV_HEADS, HEAD_DIM, PAGE_SIZE), lambda b, p, pt: (b, 0, 0, 0)),
               pl.BlockSpec(delta_s.shape, lambda b, p, pt: (0, 0, 0, 0))],
            out_specs=pl.BlockSpec((1, SROWS, B_WIDTH), per_b),
            scratch_shapes=[pltpu.VMEM((SROWS, 1), I32), pltpu.VMEM((SROWS, 1), I32),
                            pltpu.VMEM((B_KV_HEADS, grows, 1), F32), pltpu.VMEM((B_KV_HEADS, grows, 1), F32),
                            pltpu.VMEM((B_KV_HEADS, grows, HEAD_DIM), F32)]),
        out_shape=jax.ShapeDtypeStruct((nb, SROWS, B_WIDTH), F32),
        compiler_params=_cparams(("arbitrary", "arbitrary")),
        name="dsa_sample_attend",
    )(page_table, q_s, keys, *([ck] * pages), *([cv] * pages), k_new, v_new, delta_s)


def _pool_kernel(c_ref, halo_ref, w_ref, scale_ref, o_ref, *, block, offset):
    i = pl.program_id(1)
    c = c_ref[0]
    halo = jnp.where(i == 0, 0.0, halo_ref[0])
    full = jnp.concatenate([halo, c], axis=0)
    sums = [full]
    for sh in (1, 2, 4, 8):
        sums.append(sums[-1] + pltpu.roll(sums[-1], sh, axis=0))
    lane = lax.broadcasted_iota(I32, (block, C_WIDTH), 1)
    pos = i * block + lax.broadcasted_iota(I32, (block, C_WIDTH), 0) - offset
    win = jnp.zeros((block, C_WIDTH), F32)
    cnt = jnp.ones((block, C_WIDTH), F32)
    for gi, w in enumerate(POOL_WINDOWS):
        in_g = (lane >= gi * C_GROUP_DIM) & (lane < (gi + 1) * C_GROUP_DIM)
        win = jnp.where(in_g, sums[gi + 1][POOL_HALO:, :], win)
        cnt = jnp.where(in_g, jnp.clip(pos + 1, 1, w).astype(F32), cnt)
    d = win / cnt - c
    o_ref[0] = jnp.dot(d.astype(BF16), w_ref[...], preferred_element_type=F32) * scale_ref[...]


def _pool(c, w_bd, scale, block, offset):
    n_seq, length, _ = c.shape
    ratio = block // POOL_HALO
    return pl.pallas_call(
        functools.partial(_pool_kernel, block=block, offset=offset),
        grid=(n_seq, length // block),
        in_specs=[pl.BlockSpec((1, block, C_WIDTH), lambda s, i: (s, i, 0)),
                  pl.BlockSpec((1, POOL_HALO, C_WIDTH), lambda s, i: (s, jnp.maximum(i * ratio - 1, 0), 0)),
                  pl.BlockSpec((C_WIDTH, C_WIDTH), lambda s, i: (0, 0)),
                  pl.BlockSpec((1, C_WIDTH), lambda s, i: (0, 0))],
        out_specs=pl.BlockSpec((1, block, C_WIDTH), lambda s, i: (s, i, 0)),
        out_shape=jax.ShapeDtypeStruct(c.shape, F32),
        compiler_params=_cparams(("parallel", "parallel")),
        name="pool",
    )(c, c, w_bd, scale)


def _outproj_kernel(x_ref, oa_ref, ob_ref, oc_ref, wo_ref, g_ref, wr_ref, br_ref, xo_ref, h_ref, gm_ref):
    mix = jnp.dot(oa_ref[...].astype(BF16), wo_ref[0:A_WIDTH, :], preferred_element_type=F32)
    mix += jnp.dot(ob_ref[...].astype(BF16), wo_ref[A_WIDTH:A_WIDTH + B_WIDTH, :], preferred_element_type=F32)
    mix += jnp.dot(oc_ref[...].astype(BF16), wo_ref[A_WIDTH + B_WIDTH:, :], preferred_element_type=F32)
    x = x_ref[...] + mix
    xo_ref[...] = x
    h = _rms(x, g_ref[...])
    h_ref[...] = h.astype(BF16)
    logits = jnp.dot(h, wr_ref[...], precision=HIGHEST, preferred_element_type=F32) + br_ref[...]
    lane = lax.broadcasted_iota(I32, logits.shape, 1)
    work = logits
    vals, hots = [], []
    for _ in range(TOP_K):
        m = jnp.max(work, axis=1, keepdims=True)
        idx = jnp.min(jnp.where(work == m, lane, N_EXPERTS), axis=1, keepdims=True)
        hot = lane == idx
        vals.append(m)
        hots.append(hot)
        work = jnp.where(hot, -jnp.inf, work)
    es = [jnp.exp(v - vals[0]) for v in vals]
    den = es[0] + es[1] + es[2] + es[3]
    gm = jnp.zeros(logits.shape, F32)
    for e, hot in zip(es, hots):
        gm = jnp.where(hot, e / den, gm)
    gm_ref[...] = gm


def _outproj(x, oa, ob, oc, wo, g, wr, br):
    nt = x.shape[0]
    tile = lambda n: pl.BlockSpec((TOK_TILE, n), lambda i: (i, 0))
    full = lambda a: pl.BlockSpec(a.shape, lambda i: (0, 0))
    return pl.pallas_call(
        _outproj_kernel,
        grid=(nt // TOK_TILE,),
        in_specs=[tile(D_MODEL), tile(A_WIDTH), tile(B_WIDTH), tile(C_WIDTH), full(wo), full(g), full(wr), full(br)],
        out_specs=[tile(D_MODEL), tile(D_MODEL), tile(N_EXPERTS)],
        out_shape=[jax.ShapeDtypeStruct((nt, D_MODEL), F32), jax.ShapeDtypeStruct((nt, D_MODEL), BF16),
                   jax.ShapeDtypeStruct((nt, N_EXPERTS), F32)],
        compiler_params=_cparams(("parallel",)),
        name="outproj",
    )(x, oa, ob, oc, wo, g, wr, br)


MOE_GROUP = 1536
MOE_WINDOW = 240


def _moe_rank_kernel(gm_ref, su_ref, eye_ref, rank_ref, gate_ref, cnt_ref, carry_scr, *, tiles_per_group):
    i = pl.program_id(0)

    @pl.when(i % tiles_per_group == 0)
    def _():
        carry_scr[...] = jnp.zeros_like(carry_scr)

    gm = gm_ref[...]
    sel = jnp.where(gm > 0.0, 1.0, 0.0).astype(BF16)
    before = lax.dot_general(sel, su_ref[...], TN_DIMS, preferred_element_type=F32)
    sel_t = lax.dot_general(sel, eye_ref[...], TN_DIMS, preferred_element_type=F32)
    gate_ref[...] = _dot3(gm, eye_ref[...].astype(F32), TN_DIMS)
    carry = carry_scr[...]
    rank_ref[...] = jnp.where(sel_t > 0.5, carry[:, 0:1] + before, -1.0)
    carry = carry + jnp.sum(sel_t, axis=1, keepdims=True)
    carry_scr[...] = carry
    cnt_ref[0] = carry


def _moe_rank(gm):
    nt = gm.shape[0]
    tile = TOK_TILE
    tpg = MOE_GROUP // tile
    idx = np.arange(tile)
    su = jnp.asarray((idx[:, None] < idx[None, :]).astype(np.float32), BF16)
    eye = jnp.asarray(np.eye(tile, dtype=np.float32), BF16)
    return pl.pallas_call(
        functools.partial(_moe_rank_kernel, tiles_per_group=tpg),
        grid=(nt // tile,),
        in_specs=[pl.BlockSpec((tile, N_EXPERTS), lambda i: (i, 0)),
                  pl.BlockSpec((tile, tile), lambda i: (0, 0)), pl.BlockSpec((tile, tile), lambda i: (0, 0))],
        out_specs=[pl.BlockSpec((N_EXPERTS, tile), lambda i: (0, i)), pl.BlockSpec((N_EXPERTS, tile), lambda i: (0, i)),
                   pl.BlockSpec((1, N_EXPERTS, LANES), lambda i: (i // tpg, 0, 0))],
        out_shape=[jax.ShapeDtypeStruct((N_EXPERTS, nt), F32), jax.ShapeDtypeStruct((N_EXPERTS, nt), F32),
                   jax.ShapeDtypeStruct((nt // MOE_GROUP, N_EXPERTS, LANES), F32)],
        scratch_shapes=[pltpu.VMEM((N_EXPERTS, LANES), F32)],
        compiler_params=_cparams(("arbitrary",)),
        name="moe_rank",
    )(gm, su, eye)


def _one_hot_rows(rank_row, first_row):
    want = (first_row + lax.broadcasted_iota(I32, (MOE_WINDOW, 1), 0)).astype(F32)
    return jnp.where(rank_row == want, 1.0, 0.0)


def _moe_expert_kernel(ie_ref, ij_ref, io_ref, is_ref, iv_ref, h_ref, rank_ref, gate_ref, w1_ref, b1_ref, w2_ref,
                       b2_ref, y_ref, w1b_scr, w2b_scr):
    i = pl.program_id(0)

    @pl.when((i == 0) | (ie_ref[i] != ie_ref[jnp.maximum(i - 1, 0)]))
    def _():
        w1b_scr[...] = w1_ref[0, 0].astype(BF16)
        w2b_scr[...] = w2_ref[0, 0].astype(BF16)

    @pl.when(iv_ref[i] > 0)
    def _():
        hot = _one_hot_rows(rank_ref[0], is_ref[i] * MOE_WINDOW)
        rows = jnp.dot(hot.astype(BF16), h_ref[...], preferred_element_type=F32).astype(BF16)
        gate = jnp.sum(hot * gate_ref[0], axis=1, keepdims=True)
        u = jnp.dot(rows, w1b_scr[...], preferred_element_type=F32) + b1_ref[0, 0]
        glu = jnp.minimum(u[:, :D_FF], SWIGLU_LIMIT)
        lin = jnp.clip(u[:, D_FF:], -SWIGLU_LIMIT, SWIGLU_LIMIT)
        act = (glu * _sigmoid(SWIGLU_ALPHA * glu) * (lin + 1.0)).astype(BF16)
        out = jnp.dot(act, w2b_scr[...], preferred_element_type=F32) + b2_ref[0, 0]
        y_ref[...] = (out * gate).astype(BF16)


def _moe_combine_kernel(ie_ref, ij_ref, io_ref, is_ref, iv_ref, y_ref, rank_ref, x_ref, gf_ref, o_ref, acc_scr,
                        *, n_items, final):
    i = pl.program_id(0)
    valid = iv_ref[i] > 0
    nxt = jnp.minimum(i + 1, n_items - 1)

    @pl.when((i == 0) | (ij_ref[i] != ij_ref[jnp.maximum(i - 1, 0)]))
    def _():
        acc_scr[...] = jnp.zeros_like(acc_scr)

    @pl.when(valid)
    def _():
        hot = _one_hot_rows(rank_ref[0], is_ref[i] * MOE_WINDOW)
        acc_scr[...] += lax.dot_general(hot.astype(BF16), y_ref[...], TN_DIMS, preferred_element_type=F32)

    @pl.when(valid & ((i == n_items - 1) | (ij_ref[nxt] != ij_ref[i]) | (iv_ref[nxt] == 0)))
    def _():
        x = x_ref[...] + acc_scr[...]
        if final:
            x = _rms(x, gf_ref[...])
        o_ref[...] = x


def _moe_items(cnt, order_by_group):
    n_g, n_e = cnt.shape
    n_items = n_g * n_e + -(-(n_g * MOE_GROUP * TOP_K) // MOE_WINDOW)
    max_win = -(-MOE_GROUP // MOE_WINDOW)
    c = cnt.T
    n_win = (c + MOE_WINDOW - 1) // MOE_WINDOW
    seg_off = (jnp.cumsum(n_win.reshape(-1)) - n_win.reshape(-1)).reshape(n_e, n_g)
    e_id, g_id, s_id = jnp.meshgrid(jnp.arange(n_e), jnp.arange(n_g), jnp.arange(max_win), indexing="ij")
    ok = s_id < n_win[:, :, None]
    off = seg_off[:, :, None] + s_id
    if order_by_group:
        e_id, g_id, s_id, ok, off = (jnp.swapaxes(t, 0, 1) for t in (e_id, g_id, s_id, ok, off))
    flat = [t.reshape(-1).astype(I32) for t in (e_id, g_id, off, s_id, ok)]
    pick = jnp.argsort(1 - flat[4], stable=True)[:n_items]
    arrs = [t[pick] for t in flat]
    last = jnp.maximum(jnp.sum(arrs[4]) - 1, 0)
    e_a, g_a, o_a, s_a, v_a = arrs
    keep = v_a > 0
    return tuple(jnp.where(keep, t, t[last]) for t in (e_a, g_a, o_a, s_a)) + (v_a,), n_items


def _moe_routed(h, gm, x, w1, b1, w2, b2, gf, layer, final):
    nt = x.shape[0]
    n_g = nt // MOE_GROUP
    rank_t, gate_t, cnt3 = _moe_rank(gm)
    cnt = cnt3[:, :, 0].astype(I32)
    rank3 = rank_t[:, None, :]
    gate3 = gate_t[:, None, :]
    items_e, n_items = _moe_items(cnt, False)
    win_spec = pl.BlockSpec((MOE_WINDOW, D_MODEL), lambda i, ie, ij, io, is_, iv: (io[i], 0))
    per_eg = pl.BlockSpec((1, 1, MOE_GROUP), lambda i, ie, ij, io, is_, iv: (ie[i], 0, ij[i]))
    grp = lambda n: pl.BlockSpec((MOE_GROUP, n), lambda i, ie, ij, io, is_, iv: (ij[i], 0))
    wspec = lambda a, b: pl.BlockSpec((1, 1, a, b), lambda i, ie, ij, io, is_, iv: (layer, ie[i], 0, 0))
    y = pl.pallas_call(
        _moe_expert_kernel,
        grid_spec=pltpu.PrefetchScalarGridSpec(
            num_scalar_prefetch=5, grid=(n_items,),
            in_specs=[grp(D_MODEL), per_eg, per_eg, wspec(D_MODEL, 2 * D_FF), wspec(1, 2 * D_FF),
                      wspec(D_FF, D_MODEL), wspec(1, D_MODEL)],
            out_specs=win_spec,
            scratch_shapes=[pltpu.VMEM((D_MODEL, 2 * D_FF), BF16), pltpu.VMEM((D_FF, D_MODEL), BF16)]),
        out_shape=jax.ShapeDtypeStruct((n_items * MOE_WINDOW, D_MODEL), BF16),
        compiler_params=_cparams(("arbitrary",)),
        name="moe_expert",
    )(*items_e, h, rank3, gate3, w1, b1[:, :, None, :], w2, b2[:, :, None, :])
    items_g, _ = _moe_items(cnt, True)
    return pl.pallas_call(
        functools.partial(_moe_combine_kernel, n_items=n_items, final=final),
        grid_spec=pltpu.PrefetchScalarGridSpec(
            num_scalar_prefetch=5, grid=(n_items,),
            in_specs=[win_spec, per_eg, grp(D_MODEL),
                      pl.BlockSpec((1, D_MODEL), lambda i, ie, ij, io, is_, iv: (0, 0))],
            out_specs=grp(D_MODEL),
            scratch_shapes=[pltpu.VMEM((MOE_GROUP, D_MODEL), F32)]),
        out_shape=jax.ShapeDtypeStruct((nt, D_MODEL), F32),
        compiler_params=_cparams(("arbitrary",)),
        name="moe_combine",
    )(*items_g, y, rank3, x, gf)


def _rel_bucket(dist):
    n = jnp.maximum(dist, 0)
    max_exact = N_BUCKETS // 2
    log_ratio = jnp.log(jnp.maximum(n, 1).astype(jnp.float32) / max_exact) / math.log(MAX_DISTANCE / max_exact)
    large = jnp.minimum(max_exact + (log_ratio * (N_BUCKETS - max_exact)).astype(jnp.int32), N_BUCKETS - 1)
    return jnp.where(n < max_exact, n, large)


def _bias_delta(rel_bias, dist):
    rb = rel_bias.astype(F32)
    d = jnp.dot(jax.nn.one_hot(_rel_bucket(dist), N_BUCKETS, dtype=F32), rb, precision=HIGHEST) - rb[N_BUCKETS - 1]
    d = jnp.where(((dist >= 0) & (dist < MAX_DISTANCE))[..., None], d, 0.0)
    return jnp.moveaxis(d, -1, 0)


def _group_rows(delta, rows):
    return delta.reshape(B_KV_HEADS, B_REP * rows, delta.shape[-1])


def kernel(x_prompt, x_sample, cache_k, cache_v, cache_kidx, state_wkv, state_shift, state_pool, page_table,
           meta_tokens, rel_bias, norm_mix, norm_ffn, norm_final, w_in, w_out,
           a_mu, a_w0, a_w2, a_a0, a_a2, a_g2, a_kk, a_ka, a_rk, a_ln_w, a_ln_b,
           c_w, c_scale, moe_w_router, moe_b_router, moe_w1, moe_b1, moe_w2, moe_b2):
    assert x_prompt.shape[0] == 1
    depth = w_in.shape[0]
    seq = x_prompt.shape[1]
    nb, dec_seq = x_sample.shape[:2]
    assert dec_seq <= SROWS
    t_p = seq + N_META
    n_s = nb * dec_seq
    n_qblk = -(-t_p // Q_TILE)
    nkp = KEY_CHUNK * ((n_qblk + 1 + TILES_PER_CHUNK - 1) // TILES_PER_CHUNK + 1)
    assert MOE_GROUP % TOK_TILE == 0
    nt = _round_up(max(t_p + n_s, Q_TILE * n_qblk), MOE_GROUP)
    n_pages = page_table.shape[1]
    past = n_pages * PAGE_SIZE
    topk_p = min(TOPK_MAX, seq // 4)
    topk_s = min(TOPK_MAX, (past + dec_seq) // 4)
    n_phys = cache_k.shape[1]

    x = jnp.concatenate([meta_tokens.astype(F32), x_prompt[0], x_sample.reshape(n_s, D_MODEL),
                         jnp.zeros((nt - t_p - n_s, D_MODEL), F32)], axis=0)

    ones_blk = jnp.kron(jnp.eye(A_HEADS, dtype=F32), jnp.ones((HEAD_DIM, HEAD_DIM), F32))
    qi_ = jnp.arange(Q_TILE)[:, None]
    cj_ = jnp.arange(2 * Q_TILE)[None, :]
    side = TAIL_WIDTH - 2 * Q_TILE
    delta_p = jnp.pad(_bias_delta(rel_bias, Q_TILE + qi_ - cj_), ((0, 0), (0, 0), (side, side)))
    ts_ = jnp.arange(SROWS)[:, None]
    cs_ = jnp.arange(PAGE_SIZE)[None, :]
    delta_s = jnp.stack([_group_rows(_bias_delta(rel_bias, ts_ + PAGE_SIZE - cs_), SROWS),
                         _group_rows(_bias_delta(rel_bias, ts_ - cs_), SROWS)])

    outs = {n: [] for n in ("kp", "vp", "kip", "wkvp", "shp", "plp", "ks", "vs", "kis", "wkvs", "shs", "pls")}
    for l in range(depth):
        wl = w_in[l]
        b0 = A_PROJ
        w_cat = jnp.concatenate([
            wl[:, :A_PROJ], wl[:, b0:b0 + B_WIDTH + 2 * KV_WIDTH + IDX_HEADS * IDX_DIM],
            wl[:, b0 + B_PROJ - IDX_DIM - IDX_HEADS:b0 + B_PROJ],
            jnp.zeros((D_MODEL, LANES - IDX_DIM - IDX_HEADS), F32), wl[:, b0 + B_PROJ:]], axis=1).astype(BF16)
        pa, q, k, v, qi, kw, pc = _inproj(x, norm_mix[l][None], w_cat)

        sl_p = slice(0, t_p)
        sl_s = slice(t_p, t_p + n_s)
        outs["kp"].append(k[sl_p].reshape(1, t_p, B_KV_HEADS, HEAD_DIM))
        outs["vp"].append(v[sl_p].reshape(1, t_p, B_KV_HEADS, HEAD_DIM))
        outs["kip"].append(kw[sl_p, :IDX_DIM][None])
        outs["ks"].append(k[sl_s].reshape(nb, dec_seq, B_KV_HEADS, HEAD_DIM))
        outs["vs"].append(v[sl_s].reshape(nb, dec_seq, B_KV_HEADS, HEAD_DIM))
        outs["kis"].append(kw[sl_s, :IDX_DIM].reshape(nb, dec_seq, IDX_DIM))
        outs["shp"].append(pa[t_p - 1:t_p])
        pa_s = pa[sl_s].reshape(nb, dec_seq, A_PROJ)
        outs["shs"].append(pa_s[:, -1])
        pc_s = pc[sl_s].reshape(nb, dec_seq, C_WIDTH)
        outs["plp"].append(pc[t_p - POOL_BUF:t_p][None])
        pool_full = jnp.concatenate([state_pool[l], pc_s], axis=1)
        outs["pls"].append(pool_full[:, -POOL_BUF:])

        rw = (a_mu[l], a_w0[l], a_w2[l], a_a0[l], a_a2[l], a_g2[l], a_kk[l], a_ka[l], a_rk[l], a_ln_w[l], a_ln_b[l])
        oa_p, st_p = _rwkv_chunked(pa[None], jnp.zeros((1, 1, A_PROJ), F32),
                                   jnp.zeros((1, A_HEADS, HEAD_DIM, HEAD_DIM), F32),
                                   rw, ones_blk, t_p, RWKV_BLOCK, RWKV_CHUNK)
        pa_s8 = jnp.pad(pa_s, ((0, 0), (0, SROWS - dec_seq), (0, 0)))
        oa_s, st_s = _rwkv_chunked(pa_s8, state_shift[l][:, None, :], state_wkv[l].astype(F32),
                                   rw, ones_blk, dec_seq, SROWS, SROWS)
        outs["wkvp"].append(st_p.astype(state_wkv.dtype))
        outs["wkvs"].append(st_s.astype(state_wkv.dtype))

        n_real = min(nt, nkp - KEY_FRONT_PAD)
        front = lambda a: jnp.pad(a[:n_real].astype(BF16), ((KEY_FRONT_PAD, nkp - KEY_FRONT_PAD - n_real), (0, 0)))
        ob_p = _dsa_prompt(q, qi, kw, front(k), front(v), front(kw[:, :IDX_DIM]), delta_p, n_qblk, topk_p)
        rows8 = lambda a: jnp.pad(a[sl_s].reshape(nb, dec_seq, -1), ((0, 0), (0, SROWS - dec_seq), (0, 0)))
        page8 = lambda a: jnp.pad(a[sl_s].reshape(nb, dec_seq, -1), ((0, 0), (0, PAGE_SIZE - dec_seq), (0, 0)))
        kv_page = lambda a: page8(a).reshape(nb, PAGE_SIZE, B_KV_HEADS, HEAD_DIM).transpose(0, 2, 3, 1)
        ob_s = _dsa_sample(page_table, rows8(q), rows8(qi), rows8(kw),
                           cache_kidx.transpose(0, 1, 3, 2), cache_k.transpose(0, 1, 3, 4, 2),
                           cache_v.transpose(0, 1, 3, 4, 2), l, page8(kw[:, :IDX_DIM]).transpose(0, 2, 1),
                           kv_page(k), kv_page(v), delta_s, dec_seq, topk_s)

        w_bd = jax.scipy.linalg.block_diag(*[c_w[l][gi] for gi in range(len(POOL_WINDOWS))]).astype(BF16)
        oc_p = _pool(pc[None], w_bd, c_scale[l][None], TOK_TILE, 0)
        pool_in = jnp.concatenate([jnp.zeros((nb, 1, C_WIDTH), F32), pool_full,
                                   jnp.zeros((nb, 2 * POOL_HALO - 1 - POOL_BUF - dec_seq, C_WIDTH), F32)], axis=1)
        oc_s = _pool(pool_in, w_bd, c_scale[l][None], POOL_HALO, 1)[:, POOL_HALO:POOL_HALO + dec_seq]

        def merge(p_rows, s_rows):
            w = p_rows.shape[-1]
            return jnp.concatenate([p_rows[:t_p], s_rows.reshape(n_s, w), jnp.zeros((nt - t_p - n_s, w), F32)], axis=0)

        oa = merge(oa_p[0], oa_s[:, :dec_seq])
        ob = merge(ob_p, ob_s[:, :dec_seq])
        oc = merge(oc_p[0], oc_s)
        x, h2, gm = _outproj(x, oa, ob, oc, w_out[l].astype(BF16), norm_ffn[l][None], moe_w_router[l],
                             moe_b_router[l][None])
        x = _moe_routed(h2, gm, x, moe_w1, moe_b1, moe_w2, moe_b2, norm_final[None], l, l == depth - 1)

    y_prompt = x[N_META:t_p][None]
    y_sample = x[t_p:t_p + n_s].reshape(nb, dec_seq, D_MODEL)
    st = lambda n: jnp.stack(outs[n])
    return (y_prompt, y_sample, st("kp"), st("vp"), st("kip"), st("wkvp"), st("shp"), st("plp"),
            st("ks"), st("vs"), st("kis"), st("wkvs"), st("shs"), st("pls"))
```

```python
import functools
import math

import jax
import jax.numpy as jnp
import numpy as np
from jax import lax
from jax.experimental import pallas as pl
from jax.experimental.pallas import tpu as pltpu

F32 = jnp.float32
BF16 = jnp.bfloat16
I32 = jnp.int32
HIGHEST = lax.Precision.HIGHEST

D_MODEL = 1024
N_META = 16
HEAD_DIM = 64
A_WIDTH = 384
A_HEADS = 6
LORA_W, LORA_A, LORA_G = 64, 64, 128
A_PROJ = 3 * A_WIDTH + LORA_W + LORA_A + LORA_G
GN_EPS = 64e-5
B_WIDTH = 384
B_HEADS = 6
B_KV_HEADS = 2
B_REP = 3
KV_WIDTH = 128
IDX_HEADS = 4
IDX_DIM = 64
IDX_SCALE = (IDX_HEADS * IDX_DIM) ** -0.5
TOPK_MAX = 256
B_PROJ = B_WIDTH + 2 * KV_WIDTH + IDX_HEADS * IDX_DIM + IDX_DIM + IDX_HEADS
C_WIDTH = 256
C_GROUP_DIM = 64
POOL_WINDOWS = (2, 4, 8, 16)
POOL_BUF = 15
N_BUCKETS = 32
MAX_DISTANCE = 128
N_EXPERTS = 32
TOP_K = 4
D_FF = 1024
SWIGLU_LIMIT = 7.0
SWIGLU_ALPHA = 1.702
RMS_EPS = 1e-5
PAGE_SIZE = 128

LANES = 128
SUBLANES = 8
Q_TILE = 128
KEY_CHUNK = 512
KEY_FRONT_PAD = 128
TILES_PER_CHUNK = KEY_CHUNK // Q_TILE
TAIL_WIDTH = KEY_CHUNK + Q_TILE
SCORE_CHUNKS = 4
TIE_CHUNKS = 2
FAR_WIDE = 4
TOP_M = 12
SAMPLE_PAGES = 32
TOK_TILE = 512
RWKV_BLOCK = 256
RWKV_CHUNK = 128
POOL_HALO = 16
VMEM_LIMIT = 56 * 1024 * 1024

INT_MIN = -(2 ** 31)
NEG_INF_KEY = np.int32(np.uint32(0xFF800000) ^ np.uint32(0x7FFFFFFF))
NEG_BIG = -1e30


def _round_up(x, m):
    return (x + m - 1) // m * m


def _cparams(sem):
    return pltpu.CompilerParams(dimension_semantics=sem, vmem_limit_bytes=VMEM_LIMIT)


def _rms(x, g):
    return x * lax.rsqrt(jnp.mean(x * x, axis=-1, keepdims=True) + RMS_EPS) * g


def _sigmoid(x):
    return 1.0 / (1.0 + jnp.exp(-x))


IN_COLS = (A_PROJ, B_WIDTH, KV_WIDTH, KV_WIDTH, IDX_HEADS * IDX_DIM, LANES, C_WIDTH)


def _inproj_kernel(x_ref, g_ref, w_ref, *out_refs):
    hb = _rms(x_ref[...], g_ref[...]).astype(BF16)
    off = 0
    for ref in out_refs:
        n = ref.shape[1]
        ref[...] = jnp.dot(hb, w_ref[:, off:off + n], preferred_element_type=F32)
        off += n


def _inproj(x, g, w):
    nt = x.shape[0]
    return pl.pallas_call(
        _inproj_kernel,
        grid=(nt // TOK_TILE,),
        in_specs=[pl.BlockSpec((TOK_TILE, D_MODEL), lambda i: (i, 0)),
                  pl.BlockSpec((1, D_MODEL), lambda i: (0, 0)),
                  pl.BlockSpec(w.shape, lambda i: (0, 0))],
        out_specs=[pl.BlockSpec((TOK_TILE, n), lambda i: (i, 0)) for n in IN_COLS],
        out_shape=[jax.ShapeDtypeStruct((nt, n), F32) for n in IN_COLS],
        compiler_params=_cparams(("parallel",)),
        name="inproj",
    )(x, g, w)


NN_DIMS = (((1,), (0,)), ((), ()))
NT_DIMS = (((1,), (1,)), ((), ()))
TN_DIMS = (((0,), (0,)), ((), ()))


def _dot3(a, b, dims=NN_DIMS):
    a_hi = a.astype(BF16)
    b_hi = b.astype(BF16)
    a_lo = (a - a_hi.astype(F32)).astype(BF16)
    b_lo = (b - b_hi.astype(F32)).astype(BF16)
    dg = functools.partial(lax.dot_general, dimension_numbers=dims, preferred_element_type=F32)
    return dg(a_hi, b_hi) + (dg(a_hi, b_lo) + dg(a_lo, b_hi))


def _rwkv_chunk_kernel(pa_ref, sh_ref, s0_ref, mu_ref, w0_ref, w2_ref, a0_ref, a2_ref, g2_ref, kk_ref, ka_ref,
                       rk_ref, lnw_ref, lnb_ref, ones_ref, tri_ref, o_ref, s_out_ref, carry_scr, s_scr,
                       *, seq_len, block, chunk):
    blk = pl.program_id(1)

    @pl.when(blk == 0)
    def _():
        carry_scr[...] = sh_ref[0]
        s_scr[...] = s0_ref[0]

    pa = pa_ref[0]
    row = lax.broadcasted_iota(I32, (block, 1), 0)
    prev = jnp.where(row == 0, carry_scr[...], pltpu.roll(pa, 1, axis=0))
    carry_scr[...] = pa[block - 1:block, :]
    xm = pa + (prev - pa) * mu_ref[...]
    r = xm[:, 0:A_WIDTH]
    k = xm[:, A_WIDTH:2 * A_WIDTH]
    v = xm[:, 2 * A_WIDTH:3 * A_WIDTH]
    lw = xm[:, 3 * A_WIDTH:3 * A_WIDTH + LORA_W]
    la = xm[:, 3 * A_WIDTH + LORA_W:3 * A_WIDTH + LORA_W + LORA_A]
    lg = xm[:, 3 * A_WIDTH + LORA_W + LORA_A:]

    def dot_hi(a, b):
        return jnp.dot(a, b, precision=HIGHEST, preferred_element_type=F32)

    z = -(w0_ref[...] + dot_hi(jnp.tanh(lw), w2_ref[...]))
    softplus = jnp.maximum(z, 0.0) + jnp.log(1.0 + jnp.exp(-jnp.abs(z)))
    log_w = -jnp.exp(-softplus - 0.5)
    a = _sigmoid(a0_ref[...] + dot_hi(la, a2_ref[...]))
    g = dot_hi(_sigmoid(lg), g2_ref[...])
    ones_blk = ones_ref[...]
    kk = k * kk_ref[...]
    kk = kk / jnp.maximum(jnp.sqrt(dot_hi(kk * kk, ones_blk)), 1e-12)
    k2 = k * (1.0 + (a - 1.0) * ka_ref[...])
    kka = kk * a

    valid = (blk * block + row) < seq_len
    log_w = jnp.where(valid, log_w, 0.0)
    kk_m, kka_m, k2_m, r_m, v_m = (jnp.where(valid, t, 0.0) for t in (kk, kka, k2, r, v))

    cum = dot_hi(tri_ref[0], log_w)
    cum_end = dot_hi(tri_ref[1], log_w)
    kap = kk_m * jnp.exp(cum - log_w)
    rt = r_m * jnp.exp(cum)
    inv_p = jnp.exp(-cum)
    kt = k2_m * inv_p
    bt = kka_m * inv_p
    to_end = jnp.exp(cum_end - cum)
    k_end = k2_m * to_end
    b_end = kka_m * to_end
    p_end = jnp.exp(cum_end)

    sub = min(16, chunk)
    ti = lax.broadcasted_iota(I32, (chunk, chunk), 0)
    tj = lax.broadcasted_iota(I32, (chunk, chunk), 1)
    heads = range(A_HEADS)
    o_chunks = []
    for c in range(block // chunk):
        rs = slice(c * chunk, (c + 1) * chunk)
        hsl = lambda t, h: t[rs, h * HEAD_DIM:(h + 1) * HEAD_DIM]
        lhs = [jnp.concatenate([hsl(kap, h), hsl(rt, h)], axis=0) for h in heads]
        rhs = [jnp.concatenate([hsl(kt, h), hsl(bt, h)], axis=0) for h in heads]
        vh = [hsl(v_m, h) for h in heads]
        gram = [_dot3(lhs[h], rhs[h], NT_DIMS) for h in heads]
        from_s = [_dot3(lhs[h], s_scr[h], NT_DIMS) for h in heads]
        a1 = [jnp.where(ti > tj, gram[h][:chunk, :chunk], 0.0) for h in heads]
        a2 = [jnp.where(ti > tj, gram[h][:chunk, chunk:], 0.0) for h in heads]
        a3 = [jnp.where(ti >= tj, gram[h][chunk:, :chunk], 0.0) for h in heads]
        a4 = [jnp.where(ti >= tj, gram[h][chunk:, chunk:], 0.0) for h in heads]
        u = [from_s[h][:chunk] + _dot3(a1[h], vh[h]) for h in heads]
        zs = [[] for _ in heads]
        for jb in range(chunk // sub):
            js = slice(jb * sub, (jb + 1) * sub)
            zj = []
            for h in heads:
                t = u[h][js]
                if jb > 0:
                    t = t - _dot3(a2[h][js, :jb * sub], jnp.concatenate(zs[h], axis=0))
                zj.append(t)
            for i in range(sub - 1):
                for h in heads:
                    col = jnp.broadcast_to(a2[h][js, jb * sub + i:jb * sub + i + 1], (sub, HEAD_DIM))
                    zj[h] = zj[h] - col * zj[h][i:i + 1, :]
            for h in heads:
                zs[h].append(zj[h])
        zh = [jnp.concatenate(zs[h], axis=0) for h in heads]
        vz = [jnp.concatenate([vh[h], zh[h]], axis=0) for h in heads]
        o_chunks.append(jnp.concatenate(
            [from_s[h][chunk:] + _dot3(jnp.concatenate([a3[h], -a4[h]], axis=1), vz[h]) for h in heads], axis=1))
        for h in heads:
            ends = jnp.concatenate([hsl(k_end, h), -hsl(b_end, h)], axis=0)
            s_scr[h] = s_scr[h] * hsl(p_end, h)[0:1, :] + _dot3(vz[h], ends, TN_DIMS)
    s_out_ref[0] = s_scr[...]

    o = jnp.concatenate(o_chunks, axis=0) if len(o_chunks) > 1 else o_chunks[0]
    inv = 1.0 / HEAD_DIM
    mean = dot_hi(o, ones_blk) * inv
    cen = o - mean
    var = dot_hi(cen * cen, ones_blk) * inv
    o = cen * lax.rsqrt(var + GN_EPS) * lnw_ref[...] + lnb_ref[...]
    bonus = dot_hi(r * k2 * rk_ref[...], ones_blk) * v
    o_ref[0] = (o + bonus) * g


def _rwkv_chunked(pa, shift_prev, s0, params, ones_blk, seq_len, block, chunk):
    n_seq, length, _ = pa.shape
    n_blk = -(-seq_len // block)
    idx = np.arange(block)
    same = (idx[:, None] // chunk) == (idx[None, :] // chunk)
    tri = jnp.asarray(np.stack([same & (idx[None, :] <= idx[:, None]), same]).astype(np.float32))
    row_spec = lambda n: pl.BlockSpec((1, n), lambda s, i: (0, 0))
    mat_spec = lambda a: pl.BlockSpec(a.shape, lambda s, i: (0,) * a.ndim)
    mu, w0, w2, a0, a2, g2, k_k, k_a, r_k, ln_w, ln_b = params
    kern = functools.partial(_rwkv_chunk_kernel, seq_len=seq_len, block=block, chunk=chunk)
    state_spec = pl.BlockSpec((1, A_HEADS, HEAD_DIM, HEAD_DIM), lambda s, i: (s, 0, 0, 0))
    return pl.pallas_call(
        kern,
        grid=(n_seq, n_blk),
        in_specs=[pl.BlockSpec((1, block, A_PROJ), lambda s, i: (s, i, 0)),
                  pl.BlockSpec((1, 1, A_PROJ), lambda s, i: (s, 0, 0)),
                  state_spec,
                  row_spec(A_PROJ), row_spec(A_WIDTH), mat_spec(w2), row_spec(A_WIDTH), mat_spec(a2),
                  mat_spec(g2), row_spec(A_WIDTH), row_spec(A_WIDTH), row_spec(A_WIDTH), row_spec(A_WIDTH),
                  row_spec(A_WIDTH), mat_spec(ones_blk), mat_spec(tri)],
        out_specs=[pl.BlockSpec((1, block, A_WIDTH), lambda s, i: (s, i, 0)), state_spec],
        out_shape=[jax.ShapeDtypeStruct((n_seq, length, A_WIDTH), F32),
                   jax.ShapeDtypeStruct((n_seq, A_HEADS, HEAD_DIM, HEAD_DIM), F32)],
        scratch_shapes=[pltpu.VMEM((1, A_PROJ), F32), pltpu.VMEM((A_HEADS, HEAD_DIM, HEAD_DIM), F32)],
        compiler_params=_cparams(("arbitrary", "arbitrary")),
        name="rwkv",
    )(pa, shift_prev, s0, mu[None], w0[None], w2, a0[None], a2, g2, k_k[None], k_a[None], r_k[None],
      ln_w[None], ln_b[None], ones_blk, tri)


def _sortable_key(s):
    bits = pltpu.bitcast(s + 0.0, I32)
    return bits ^ ((bits >> 31) & jnp.int32(0x7FFFFFFF))


def _index_scores(qi_b, wi, ki_b, transposed=False):
    s = None
    for h in range(IDX_HEADS):
        d = lax.dot_general(qi_b[:, h * IDX_DIM:(h + 1) * IDX_DIM], ki_b,
                            (((1,), (0 if transposed else 1,)), ((), ())), preferred_element_type=F32)
        t = jnp.maximum(d, 0.0) * wi[:, h:h + 1]
        s = t if s is None else s + t
    return s


def _topk_threshold(count_fn, count_tie_fn, rows, topk):
    kf = jnp.float32(topk)
    c0 = count_fn(jnp.zeros((rows, 1), I32), False)
    prefix0 = jnp.where(c0 >= kf, jnp.int32(0), jnp.int32(INT_MIN))

    def bit_step(i, prefix):
        cand = prefix | lax.shift_left(jnp.int32(1), jnp.int32(30) - i)
        return jnp.where(count_fn(cand, False) >= kf, cand, prefix)

    tau = lax.fori_loop(0, 31, bit_step, prefix0)
    c_gt = count_fn(tau, True)
    c_ge = count_fn(tau, False)
    need = kf - c_gt
    excess = jnp.max(c_ge - c_gt - need)

    def tie_search(_):
        def jbit(i, jl):
            cand = jl | lax.shift_left(jnp.int32(1), jnp.int32(14) - i)
            return jnp.where(count_tie_fn(tau, cand) < need, cand, jl)
        return lax.fori_loop(0, 15, jbit, jnp.zeros((rows, 1), I32))

    jlim = lax.cond(excess > 0.0, tie_search, lambda _: jnp.full((rows, 1), 2 ** 30, I32), 0)
    return tau, jlim


def _select(keys, aidx, tau, jlim):
    return ((keys > tau) | ((keys == tau) & (aidx <= jlim))) & (keys > jnp.int32(NEG_INF_KEY))


def _dsa_prompt_kernel(q_ref, qi_ref, kw_ref, kp_ref, vp_ref, kip_ref, dl_ref, o_ref, key_scr, cand_scr, *, topk):
    j = pl.program_id(0)
    rows = Q_TILE
    n_chunks = (j + 2 + TILES_PER_CHUNK - 1) // TILES_PER_CHUNK
    qi_b = qi_ref[...].astype(BF16)
    wi = kw_ref[:, IDX_DIM:IDX_DIM + IDX_HEADS] * IDX_SCALE
    qpos = j * Q_TILE + lax.broadcasted_iota(I32, (rows, 1), 0)
    lane_c = lax.broadcasted_iota(I32, (1, KEY_CHUNK), 1)

    def score_chunk(c, masked):
        a0 = pl.multiple_of(c * KEY_CHUNK, KEY_CHUNK)
        s = _index_scores(qi_b, wi, kip_ref[pl.ds(a0, KEY_CHUNK), :])
        if masked:
            kpos = a0 - KEY_FRONT_PAD + lane_c
            s = jnp.where((kpos <= qpos) & (kpos >= 0), s, -jnp.inf)
        key_scr[:, pl.ds(a0, KEY_CHUNK)] = _sortable_key(s)

    n_plain = jnp.clip((j * Q_TILE + KEY_FRONT_PAD - KEY_CHUNK) // KEY_CHUNK, 0, n_chunks - 1)
    score_chunk(0, True)

    def plain_group(p, x):
        for i in range(SCORE_CHUNKS):
            score_chunk(1 + SCORE_CHUNKS * p + i, False)
        return x

    lax.fori_loop(0, n_plain // SCORE_CHUNKS, plain_group, 0)
    lax.fori_loop(1 + SCORE_CHUNKS * (n_plain // SCORE_CHUNKS), n_chunks, lambda c, x: (score_chunk(c, True), x)[1], 0)

    key_scr[:, pl.ds(pl.multiple_of(n_chunks * KEY_CHUNK, KEY_CHUNK), KEY_CHUNK)] = jnp.full(
        (rows, KEY_CHUNK), NEG_INF_KEY, I32)

    n_fold = KEY_CHUNK // LANES

    def chunk_at(ref, c):
        return ref[:, pl.ds(pl.multiple_of(c * KEY_CHUNK, KEY_CHUNK), KEY_CHUNK)]

    def count32(cand, strict):
        def body(c, acc):
            kk = chunk_at(key_scr, c)
            hit = (kk > cand) if strict else (kk >= cand)
            m = jnp.where(hit, 1.0, 0.0)
            for i in range(n_fold):
                acc = acc + m[:, i * LANES:(i + 1) * LANES]
            return acc
        acc = lax.fori_loop(0, n_chunks, body, jnp.zeros((rows, LANES), F32))
        return jnp.sum(acc, axis=1, keepdims=True)

    kf = jnp.float32(topk)

    def kth_largest(count_ge):
        def bit_step(i, prefix):
            cand = prefix | lax.shift_left(jnp.int32(1), jnp.int32(30) - i)
            return jnp.where(count_ge(cand) >= kf, cand, prefix)
        nonneg = count_ge(jnp.zeros((rows, 1), I32)) >= kf
        return lax.fori_loop(0, 31, bit_step, jnp.where(nonneg, jnp.int32(0), jnp.int32(INT_MIN)))

    pair = 2 * SUBLANES
    for rp in range(rows // pair):
        r0 = rp * pair

        def insert_chunk(c, lists, r0=r0):
            a0 = pl.multiple_of(c * KEY_CHUNK, KEY_CHUNK)
            blk = key_scr[r0:r0 + pair, pl.ds(a0, KEY_CHUNK)]
            blk = pltpu.bitcast(blk ^ ((blk >> 31) & jnp.int32(0x7FFFFFFF)), F32)
            lists = list(lists)
            for half in range(2):
                for t in range(n_fold):
                    x = blk[half * SUBLANES:(half + 1) * SUBLANES, t * LANES:(t + 1) * LANES]
                    for i in range(TOP_M):
                        a = lists[half * TOP_M + i]
                        lists[half * TOP_M + i] = jnp.maximum(a, x)
                        x = jnp.minimum(a, x)
            return tuple(lists)

        lists = lax.fori_loop(0, n_chunks, insert_chunk,
                              tuple(jnp.full((SUBLANES, LANES), -jnp.inf, F32) for _ in range(2 * TOP_M)))
        for half in range(2):
            for i in range(TOP_M):
                cand_scr[r0 + half * SUBLANES:r0 + (half + 1) * SUBLANES, i * LANES:(i + 1) * LANES] = _sortable_key(
                    lists[half * TOP_M + i])

    def count_cand(cand):
        acc = jnp.zeros((rows, LANES), F32)
        for i in range(TOP_M):
            acc = acc + jnp.where(cand_scr[:, i * LANES:(i + 1) * LANES] >= cand, 1.0, 0.0)
        return jnp.sum(acc, axis=1, keepdims=True)

    tau_c = kth_largest(count_cand)
    gt_c = count32(tau_c, True)
    ge_c = count32(tau_c, False)
    wrong = jnp.max(jnp.where((gt_c < kf) & (ge_c >= kf), 0.0, 1.0))

    def full_search(_):
        t = kth_largest(lambda cand: count32(cand, False))
        return t, count32(t, True), count32(t, False)

    tau, c_gt, c_ge = lax.cond(wrong > 0.0, full_search, lambda _: (tau_c, gt_c, ge_c), 0)
    need = kf - c_gt
    excess = jnp.max(c_ge - c_gt - need)

    def tie_pass(_):
        ri = lax.broadcasted_iota(I32, (LANES, 2 * LANES), 0)
        ci = lax.broadcasted_iota(I32, (LANES, 2 * LANES), 1)
        tri = jnp.where((ri <= ci) | (ci >= LANES), 1.0, 0.0).astype(BF16)

        span = TIE_CHUNKS * KEY_CHUNK
        n_tile = span // LANES

        def body(c, carry):
            seen, jacc = carry
            a0 = pl.multiple_of(c * span, span)
            kk = key_scr[:, pl.ds(a0, span)]
            ties = [kk[:, i * LANES:(i + 1) * LANES] == tau for i in range(n_tile)]
            pres = [jnp.dot(jnp.where(t, 1.0, 0.0).astype(BF16), tri, preferred_element_type=F32) for t in ties]
            for i in range(n_tile):
                rank = jnp.where(ties[i], seen + pres[i][:, :LANES], 1e9)
                aidx = a0 + i * LANES + lax.broadcasted_iota(I32, (1, LANES), 1)
                jacc = jnp.maximum(jacc, jnp.where(rank <= need, aidx, -1))
                seen = seen + pres[i][:, LANES:]
            return seen, jacc
        _, jacc = lax.fori_loop(0, (n_chunks + TIE_CHUNKS - 1) // TIE_CHUNKS, body,
                                (jnp.zeros((rows, LANES), F32), jnp.full((rows, LANES), -1, I32)))
        return jnp.max(jacc, axis=1, keepdims=True)

    jlim = lax.cond(excess > 0.0, tie_pass, lambda _: jnp.full((rows, 1), 2 ** 30, I32), 0)
    jlim = jnp.where(tau == jnp.int32(NEG_INF_KEY), jnp.int32(-1), jlim)
    tau_b = jnp.broadcast_to(tau, (rows, LANES))
    tau_m1_b = tau_b - 1

    q = q_ref[...] * (HEAD_DIM ** -0.5)
    qh = [q[:, h * HEAD_DIM:(h + 1) * HEAD_DIM].astype(BF16) for h in range(B_HEADS)]

    def init_state():
        return (jnp.full((rows, 1), NEG_BIG, F32), jnp.zeros((rows, 1), F32), jnp.zeros((rows, HEAD_DIM), F32))

    def attend(a0, width, states, delta):
        kk = key_scr[:, pl.ds(a0, width)]
        aidx = a0 + lax.broadcasted_iota(I32, (1, width), 1)
        thr = jnp.where(aidx <= jlim, jnp.concatenate([tau_m1_b] * (width // LANES), axis=1),
                        jnp.concatenate([tau_b] * (width // LANES), axis=1))
        madd = jnp.where(kk > thr, 0.0, NEG_BIG)
        kts = [kp_ref[pl.ds(a0, width), g * HEAD_DIM:(g + 1) * HEAD_DIM] for g in range(B_KV_HEADS)]
        vts = [vp_ref[pl.ds(a0, width), g * HEAD_DIM:(g + 1) * HEAD_DIM] for g in range(B_KV_HEADS)]
        lgs = [lax.dot_general(qh[h], kts[h // B_REP], (((1,), (1,)), ((), ())), preferred_element_type=F32)
               for h in range(B_HEADS)]
        mid = []
        for h in range(B_HEADS):
            m, l, _ = states[h]
            lg = lgs[h] + madd
            if delta is not None:
                lg = lg + delta[h]
            m_new = jnp.maximum(m, jnp.max(lg, axis=1, keepdims=True))
            alpha = jnp.exp(m - m_new)
            p = jnp.exp(lg - m_new)
            mid.append((m_new, alpha, alpha * l + jnp.sum(p, axis=1, keepdims=True), p.astype(BF16)))
        out = []
        for h in range(B_HEADS):
            m_new, alpha, l, pb = mid[h]
            acc = alpha * states[h][2] + jnp.dot(pb, vts[h // B_REP], preferred_element_type=F32)
            out.append((m_new, l, acc))
        return tuple(out)

    def far_wide(c, states):
        return attend(pl.multiple_of(c * FAR_WIDE * KEY_CHUNK, FAR_WIDE * KEY_CHUNK), FAR_WIDE * KEY_CHUNK, states, None)

    def far_chunk(c, states):
        return attend(pl.multiple_of(c * KEY_CHUNK, KEY_CHUNK), KEY_CHUNK, states, None)

    n_far = j // TILES_PER_CHUNK
    states = lax.fori_loop(0, n_far // FAR_WIDE, far_wide, tuple(init_state() for _ in range(B_HEADS)))
    states = lax.fori_loop((n_far // FAR_WIDE) * FAR_WIDE, n_far, far_chunk, states)
    d0 = pl.multiple_of((TILES_PER_CHUNK - 1 - j % TILES_PER_CHUNK) * Q_TILE, Q_TILE)
    delta = [dl_ref[h, :, pl.ds(d0, TAIL_WIDTH)] for h in range(B_HEADS)]
    states = attend(pl.multiple_of(n_far * KEY_CHUNK, KEY_CHUNK), TAIL_WIDTH, states, delta)

    o_ref[...] = jnp.concatenate([acc / l for _, l, acc in states], axis=1)


def _dsa_prompt(q, qi, kw, kp, vp, kip, delta, n_qblk, topk):
    nkp = kp.shape[0]
    full = lambda a: pl.BlockSpec(a.shape, lambda j: (0,) * a.ndim)
    return pl.pallas_call(
        functools.partial(_dsa_prompt_kernel, topk=topk),
        grid=(n_qblk,),
        in_specs=[pl.BlockSpec((Q_TILE, B_WIDTH), lambda j: (j, 0)),
                  pl.BlockSpec((Q_TILE, IDX_HEADS * IDX_DIM), lambda j: (j, 0)),
                  pl.BlockSpec((Q_TILE, LANES), lambda j: (j, 0)),
                  full(kp), full(vp), full(kip), full(delta)],
        out_specs=pl.BlockSpec((Q_TILE, B_WIDTH), lambda j: (j, 0)),
        out_shape=jax.ShapeDtypeStruct((n_qblk * Q_TILE, B_WIDTH), F32),
        scratch_shapes=[pltpu.VMEM((Q_TILE, nkp), I32), pltpu.VMEM((Q_TILE, TOP_M * LANES), I32)],
        compiler_params=_cparams(("arbitrary",)),
        name="dsa_prompt",
    )(q, qi, kw, kp, vp, kip, delta)


SROWS = 8


def _dsa_sample_score_kernel(pt_ref, qi_ref, kw_ref, *refs, n_steps, pages, dec_seq):
    cki_refs, kin_ref, key_ref = refs[:pages], refs[pages], refs[pages + 1]
    p = pl.program_id(1)
    qi_b = qi_ref[0].astype(BF16)
    wi = kw_ref[0][:, IDX_DIM:IDX_DIM + IDX_HEADS] * IDX_SCALE
    t = lax.broadcasted_iota(I32, (SROWS, 1), 0)
    c = lax.broadcasted_iota(I32, (1, PAGE_SIZE), 1)

    @pl.when(p < n_steps)
    def _():
        ki_all = jnp.concatenate([r[0, 0].astype(BF16) for r in cki_refs], axis=1)
        s = _index_scores(qi_b, wi, ki_all, transposed=True)
        key_ref[0] = _sortable_key(jnp.where(t < dec_seq, s, -jnp.inf))

    @pl.when(p == n_steps)
    def _():
        s = _index_scores(qi_b, wi, kin_ref[0].astype(BF16), transposed=True)
        key_ref[0, :, 0:PAGE_SIZE] = _sortable_key(jnp.where((t < dec_seq) & (c <= t), s, -jnp.inf))
        if pages > 1:
            key_ref[0, :, PAGE_SIZE:] = jnp.full((SROWS, (pages - 1) * PAGE_SIZE), NEG_INF_KEY, I32)


def _dsa_sample_attend_kernel(pt_ref, q_ref, key_ref, *refs, n_steps, pages, topk):
    ck_refs, cv_refs = refs[:pages], refs[pages:2 * pages]
    kn_ref, vn_ref, dl_ref, o_ref, tau_scr, jl_scr, m_scr, l_scr, acc_scr = refs[2 * pages:]
    p = pl.program_id(1)
    width = pages * PAGE_SIZE

    @pl.when(p == 0)
    def _():
        keys = key_ref[0]
        aidx = lax.broadcasted_iota(I32, (1, keys.shape[1]), 1)

        def count_fn(cand, strict):
            hit = (keys > cand) if strict else (keys >= cand)
            return jnp.sum(jnp.where(hit, 1.0, 0.0), axis=1, keepdims=True)

        def count_tie_fn(tau, jl):
            return jnp.sum(jnp.where((keys == tau) & (aidx < jl), 1.0, 0.0), axis=1, keepdims=True)

        tau, jlim = _topk_threshold(count_fn, count_tie_fn, SROWS, topk)
        tau_scr[...] = tau
        jl_scr[...] = jlim
        m_scr[...] = jnp.full(m_scr.shape, NEG_BIG, F32)
        l_scr[...] = jnp.zeros(l_scr.shape, F32)
        acc_scr[...] = jnp.zeros(acc_scr.shape, F32)

    a0 = pl.multiple_of(p * width, width)
    kk = key_ref[0, :, pl.ds(a0, width)]
    aidx = a0 + lax.broadcasted_iota(I32, (1, width), 1)
    sel = _select(kk, aidx, tau_scr[...], jl_scr[...])
    sel3 = jnp.concatenate([sel] * B_REP, axis=0)
    q = q_ref[0] * (HEAD_DIM ** -0.5)

    def cached(ref):
        return lambda g: ref[0, 0, g]

    def fresh(ref):
        return lambda g: ref[0, g]

    def run(k_tiles, v_tiles, delta):
        n = len(k_tiles)
        sel_n = sel3[:, :n * PAGE_SIZE]
        for g in range(B_KV_HEADS):
            qg = jnp.concatenate([q[:, (g * B_REP + r) * HEAD_DIM:(g * B_REP + r + 1) * HEAD_DIM]
                                  for r in range(B_REP)], axis=0).astype(BF16)
            k_all = jnp.concatenate([kt(g).astype(BF16) for kt in k_tiles], axis=1)
            lg = jnp.dot(qg, k_all, preferred_element_type=F32)
            if delta is not None:
                pad = [jnp.zeros((lg.shape[0], (n - 1) * PAGE_SIZE), F32)] if n > 1 else []
                lg = lg + jnp.concatenate(pad + [delta[g]], axis=1)
            lg = jnp.where(sel_n, lg, NEG_BIG)
            m = m_scr[g]
            m_new = jnp.maximum(m, jnp.max(lg, axis=1, keepdims=True))
            alpha = jnp.exp(m - m_new)
            pr = jnp.where(sel_n, jnp.exp(lg - m_new), 0.0).astype(BF16)
            v_all = jnp.concatenate([vt(g).astype(BF16) for vt in v_tiles], axis=1)
            acc = alpha * acc_scr[g] + lax.dot_general(pr, v_all, NT_DIMS, preferred_element_type=F32)
            m_scr[g] = m_new
            l_scr[g] = alpha * l_scr[g] + jnp.sum(pr.astype(F32), axis=1, keepdims=True)
            acc_scr[g] = acc

    ck_tiles = [cached(r) for r in ck_refs]
    cv_tiles = [cached(r) for r in cv_refs]

    @pl.when(p < n_steps - 1)
    def _():
        run(ck_tiles, cv_tiles, None)

    @pl.when(p == n_steps - 1)
    def _():
        run(ck_tiles, cv_tiles, (dl_ref[0, 0], dl_ref[0, 1]))

    @pl.when(p == n_steps)
    def _():
        run([fresh(kn_ref)], [fresh(vn_ref)], (dl_ref[1, 0], dl_ref[1, 1]))
        pieces = [None] * B_HEADS
        for g in range(B_KV_HEADS):
            og = acc_scr[g] / l_scr[g]
            for r in range(B_REP):
                pieces[g * B_REP + r] = og[r * SROWS:(r + 1) * SROWS, :]
        o_ref[0] = jnp.concatenate(pieces, axis=1)


def _dsa_sample(page_table, q_s, qi_s, kw_s, cki, ck, cv, layer, ki_new, k_new, v_new, delta_s, dec_seq, topk):
    nb = q_s.shape[0]
    n_pages = page_table.shape[1]
    pages = math.gcd(n_pages, SAMPLE_PAGES)
    n_steps = n_pages // pages
    width = pages * PAGE_SIZE
    n_keys = (n_steps + 1) * width

    def page(i):
        return lambda b, p, pt: (layer, pt[b, jnp.minimum(p, n_steps - 1) * pages + i], 0, 0)

    def page_kv(i):
        return lambda b, p, pt: (layer, pt[b, jnp.minimum(p, n_steps - 1) * pages + i], 0, 0, 0)

    per_b = lambda b, p, pt: (b, 0, 0)
    keys = pl.pallas_call(
        functools.partial(_dsa_sample_score_kernel, n_steps=n_steps, pages=pages, dec_seq=dec_seq),
        grid_spec=pltpu.PrefetchScalarGridSpec(
            num_scalar_prefetch=1, grid=(nb, n_steps + 1),
            in_specs=[pl.BlockSpec((1, SROWS, IDX_HEADS * IDX_DIM), per_b),
                      pl.BlockSpec((1, SROWS, LANES), per_b)]
            + [pl.BlockSpec((1, 1, IDX_DIM, PAGE_SIZE), page(i)) for i in range(pages)]
            + [pl.BlockSpec((1, IDX_DIM, PAGE_SIZE), per_b)],
            out_specs=pl.BlockSpec((1, SROWS, width), lambda b, p, pt: (b, 0, p))),
        out_shape=jax.ShapeDtypeStruct((nb, SROWS, n_keys), I32),
        compiler_params=_cparams(("arbitrary", "arbitrary")),
        name="dsa_sample_score",
    )(page_table, qi_s, kw_s, *([cki] * pages), ki_new)
    grows = B_REP * SROWS
    return pl.pallas_call(
        functools.partial(_dsa_sample_attend_kernel, n_steps=n_steps, pages=pages, topk=topk),
        grid_spec=pltpu.PrefetchScalarGridSpec(
            num_scalar_prefetch=1, grid=(nb, n_steps + 1),
            in_specs=[pl.BlockSpec((1, SROWS, B_WIDTH), per_b),
                      pl.BlockSpec((1, SROWS, n_keys), per_b)]
            + [pl.BlockSpec((1, 1, B_KV_HEADS, HEAD_DIM, PAGE_SIZE), page_kv(i)) for i in range(pages)] * 2
            + [pl.BlockSpec((1, B_KV_HEADS, HEAD_DIM, PAGE_SIZE), lambda b, p, pt: (b, 0, 0, 0)),
               pl.BlockSpec((1, B_KV_HEADS, HEAD_DIM, PAGE_SIZE), lambda b, p, pt: (b, 0, 0, 0)),
               pl.BlockSpec(delta_s.shape, lambda b, p, pt: (0, 0, 0, 0))],
            out_specs=pl.BlockSpec((1, SROWS, B_WIDTH), per_b),
            scratch_shapes=[pltpu.VMEM((SROWS, 1), I32), pltpu.VMEM((SROWS, 1), I32),
                            pltpu.VMEM((B_KV_HEADS, grows, 1), F32), pltpu.VMEM((B_KV_HEADS, grows, 1), F32),
                            pltpu.VMEM((B_KV_HEADS, grows, HEAD_DIM), F32)]),
        out_shape=jax.ShapeDtypeStruct((nb, SROWS, B_WIDTH), F32),
        compiler_params=_cparams(("arbitrary", "arbitrary")),
        name="dsa_sample_attend",
    )(page_table, q_s, keys, *([ck] * pages), *([cv] * pages), k_new, v_new, delta_s)


def _pool_kernel(c_ref, halo_ref, w_ref, scale_ref, o_ref, *, block, offset):
    i = pl.program_id(1)
    c = c_ref[0]
    halo = jnp.where(i == 0, 0.0, halo_ref[0])
    full = jnp.concatenate([halo, c], axis=0)
    sums = [full]
    for sh in (1, 2, 4, 8):
        sums.append(sums[-1] + pltpu.roll(sums[-1], sh, axis=0))
    lane = lax.broadcasted_iota(I32, (block, C_WIDTH), 1)
    pos = i * block + lax.broadcasted_iota(I32, (block, C_WIDTH), 0) - offset
    win = jnp.zeros((block, C_WIDTH), F32)
    cnt = jnp.ones((block, C_WIDTH), F32)
    for gi, w in enumerate(POOL_WINDOWS):
        in_g = (lane >= gi * C_GROUP_DIM) & (lane < (gi + 1) * C_GROUP_DIM)
        win = jnp.where(in_g, sums[gi + 1][POOL_HALO:, :], win)
        cnt = jnp.where(in_g, jnp.clip(pos + 1, 1, w).astype(F32), cnt)
    d = win / cnt - c
    o_ref[0] = jnp.dot(d.astype(BF16), w_ref[...], preferred_element_type=F32) * scale_ref[...]


def _pool(c, w_bd, scale, block, offset):
    n_seq, length, _ = c.shape
    ratio = block // POOL_HALO
    return pl.pallas_call(
        functools.partial(_pool_kernel, block=block, offset=offset),
        grid=(n_seq, length // block),
        in_specs=[pl.BlockSpec((1, block, C_WIDTH), lambda s, i: (s, i, 0)),
                  pl.BlockSpec((1, POOL_HALO, C_WIDTH), lambda s, i: (s, jnp.maximum(i * ratio - 1, 0), 0)),
                  pl.BlockSpec((C_WIDTH, C_WIDTH), lambda s, i: (0, 0)),
                  pl.BlockSpec((1, C_WIDTH), lambda s, i: (0, 0))],
        out_specs=pl.BlockSpec((1, block, C_WIDTH), lambda s, i: (s, i, 0)),
        out_shape=jax.ShapeDtypeStruct(c.shape, F32),
        compiler_params=_cparams(("parallel", "parallel")),
        name="pool",
    )(c, c, w_bd, scale)


def _outproj_kernel(x_ref, oa_ref, ob_ref, oc_ref, wo_ref, g_ref, wr_ref, br_ref, xo_ref, h_ref, gm_ref, *, n_real):
    mix = jnp.dot(oa_ref[...].astype(BF16), wo_ref[0:A_WIDTH, :], preferred_element_type=F32)
    mix += jnp.dot(ob_ref[...].astype(BF16), wo_ref[A_WIDTH:A_WIDTH + B_WIDTH, :], preferred_element_type=F32)
    mix += jnp.dot(oc_ref[...].astype(BF16), wo_ref[A_WIDTH + B_WIDTH:, :], preferred_element_type=F32)
    x = x_ref[...] + mix
    xo_ref[...] = x
    h = _rms(x, g_ref[...])
    h_ref[...] = h.astype(BF16)
    logits = jnp.dot(h, wr_ref[...], precision=HIGHEST, preferred_element_type=F32) + br_ref[...]
    lane = lax.broadcasted_iota(I32, logits.shape, 1)
    work = logits
    vals, hots = [], []
    for _ in range(TOP_K):
        m = jnp.max(work, axis=1, keepdims=True)
        idx = jnp.min(jnp.where(work == m, lane, N_EXPERTS), axis=1, keepdims=True)
        hot = lane == idx
        vals.append(m)
        hots.append(hot)
        work = jnp.where(hot, -jnp.inf, work)
    es = [jnp.exp(v - vals[0]) for v in vals]
    den = es[0] + es[1] + es[2] + es[3]
    gm = jnp.zeros(logits.shape, F32)
    for e, hot in zip(es, hots):
        gm = jnp.where(hot, e / den, gm)
    row = pl.program_id(0) * TOK_TILE + lax.broadcasted_iota(I32, logits.shape, 0)
    gm_ref[...] = jnp.where(row < n_real, gm, 0.0)


def _outproj(x, oa, ob, oc, wo, g, wr, br, n_real):
    nt = x.shape[0]
    tile = lambda n: pl.BlockSpec((TOK_TILE, n), lambda i: (i, 0))
    full = lambda a: pl.BlockSpec(a.shape, lambda i: (0, 0))
    return pl.pallas_call(
        functools.partial(_outproj_kernel, n_real=n_real),
        grid=(nt // TOK_TILE,),
        in_specs=[tile(D_MODEL), tile(A_WIDTH), tile(B_WIDTH), tile(C_WIDTH), full(wo), full(g), full(wr), full(br)],
        out_specs=[tile(D_MODEL), tile(D_MODEL), tile(N_EXPERTS)],
        out_shape=[jax.ShapeDtypeStruct((nt, D_MODEL), F32), jax.ShapeDtypeStruct((nt, D_MODEL), BF16),
                   jax.ShapeDtypeStruct((nt, N_EXPERTS), F32)],
        compiler_params=_cparams(("parallel",)),
        name="outproj",
    )(x, oa, ob, oc, wo, g, wr, br)


MOE_GROUP = 1536
MOE_WINDOW = 240


def _moe_rank_kernel(gm_ref, su_ref, eye_ref, rank_ref, gate_ref, cnt_ref, carry_scr, *, tiles_per_group):
    i = pl.program_id(0)

    @pl.when(i % tiles_per_group == 0)
    def _():
        carry_scr[...] = jnp.zeros_like(carry_scr)

    gm = gm_ref[...]
    sel = jnp.where(gm > 0.0, 1.0, 0.0).astype(BF16)
    before = lax.dot_general(sel, su_ref[...], TN_DIMS, preferred_element_type=F32)
    sel_t = lax.dot_general(sel, eye_ref[...], TN_DIMS, preferred_element_type=F32)
    gate_ref[...] = _dot3(gm, eye_ref[...].astype(F32), TN_DIMS)
    carry = carry_scr[...]
    rank_ref[...] = jnp.where(sel_t > 0.5, carry[:, 0:1] + before, -1.0)
    carry = carry + jnp.sum(sel_t, axis=1, keepdims=True)
    carry_scr[...] = carry
    cnt_ref[0] = carry


def _moe_rank(gm):
    nt = gm.shape[0]
    tile = TOK_TILE
    tpg = MOE_GROUP // tile
    idx = np.arange(tile)
    su = jnp.asarray((idx[:, None] < idx[None, :]).astype(np.float32), BF16)
    eye = jnp.asarray(np.eye(tile, dtype=np.float32), BF16)
    return pl.pallas_call(
        functools.partial(_moe_rank_kernel, tiles_per_group=tpg),
        grid=(nt // tile,),
        in_specs=[pl.BlockSpec((tile, N_EXPERTS), lambda i: (i, 0)),
                  pl.BlockSpec((tile, tile), lambda i: (0, 0)), pl.BlockSpec((tile, tile), lambda i: (0, 0))],
        out_specs=[pl.BlockSpec((N_EXPERTS, tile), lambda i: (0, i)), pl.BlockSpec((N_EXPERTS, tile), lambda i: (0, i)),
                   pl.BlockSpec((1, N_EXPERTS, LANES), lambda i: (i // tpg, 0, 0))],
        out_shape=[jax.ShapeDtypeStruct((N_EXPERTS, nt), F32), jax.ShapeDtypeStruct((N_EXPERTS, nt), F32),
                   jax.ShapeDtypeStruct((nt // MOE_GROUP, N_EXPERTS, LANES), F32)],
        scratch_shapes=[pltpu.VMEM((N_EXPERTS, LANES), F32)],
        compiler_params=_cparams(("arbitrary",)),
        name="moe_rank",
    )(gm, su, eye)


def _one_hot_rows(rank_row, first_row):
    want = (first_row + lax.broadcasted_iota(I32, (MOE_WINDOW, 1), 0)).astype(F32)
    return jnp.where(rank_row == want, 1.0, 0.0)


def _moe_expert_kernel(ie_ref, ij_ref, io_ref, is_ref, iv_ref, h_ref, rank_ref, gate_ref, w1_ref, b1_ref, w2_ref,
                       b2_ref, y_ref, w1b_scr, w2b_scr):
    i = pl.program_id(0)

    @pl.when((i == 0) | (ie_ref[i] != ie_ref[jnp.maximum(i - 1, 0)]))
    def _():
        w1b_scr[...] = w1_ref[0, 0].astype(BF16)
        w2b_scr[...] = w2_ref[0, 0].astype(BF16)

    @pl.when(iv_ref[i] > 0)
    def _():
        hot = _one_hot_rows(rank_ref[0], is_ref[i] * MOE_WINDOW)
        rows = jnp.dot(hot.astype(BF16), h_ref[...], preferred_element_type=F32).astype(BF16)
        gate = jnp.sum(hot * gate_ref[0], axis=1, keepdims=True)
        u = jnp.dot(rows, w1b_scr[...], preferred_element_type=F32) + b1_ref[0, 0]
        glu = jnp.minimum(u[:, :D_FF], SWIGLU_LIMIT)
        lin = jnp.clip(u[:, D_FF:], -SWIGLU_LIMIT, SWIGLU_LIMIT)
        act = (glu * _sigmoid(SWIGLU_ALPHA * glu) * (lin + 1.0)).astype(BF16)
        out = jnp.dot(act, w2b_scr[...], preferred_element_type=F32) + b2_ref[0, 0]
        y_ref[...] = (out * gate).astype(BF16)


def _moe_combine_kernel(ie_ref, ij_ref, io_ref, is_ref, iv_ref, y_ref, rank_ref, x_ref, gf_ref, o_ref, acc_scr,
                        *, n_items, final):
    i = pl.program_id(0)
    valid = iv_ref[i] > 0
    nxt = jnp.minimum(i + 1, n_items - 1)

    @pl.when((i == 0) | (ij_ref[i] != ij_ref[jnp.maximum(i - 1, 0)]))
    def _():
        acc_scr[...] = jnp.zeros_like(acc_scr)

    @pl.when(valid)
    def _():
        hot = _one_hot_rows(rank_ref[0], is_ref[i] * MOE_WINDOW)
        acc_scr[...] += lax.dot_general(hot.astype(BF16), y_ref[...], TN_DIMS, preferred_element_type=F32)

    @pl.when(valid & ((i == n_items - 1) | (ij_ref[nxt] != ij_ref[i]) | (iv_ref[nxt] == 0)))
    def _():
        x = x_ref[...] + acc_scr[...]
        if final:
            x = _rms(x, gf_ref[...])
        o_ref[...] = x


def _moe_items(cnt, order_by_group, n_items):
    n_g, n_e = cnt.shape
    max_win = -(-MOE_GROUP // MOE_WINDOW)
    c = cnt.T
    n_win = (c + MOE_WINDOW - 1) // MOE_WINDOW
    seg_off = (jnp.cumsum(n_win.reshape(-1)) - n_win.reshape(-1)).reshape(n_e, n_g)
    e_id, g_id, s_id = jnp.meshgrid(jnp.arange(n_e), jnp.arange(n_g), jnp.arange(max_win), indexing="ij")
    ok = s_id < n_win[:, :, None]
    off = seg_off[:, :, None] + s_id
    if order_by_group:
        e_id, g_id, s_id, ok, off = (jnp.swapaxes(t, 0, 1) for t in (e_id, g_id, s_id, ok, off))
    flat = [t.reshape(-1).astype(I32) for t in (e_id, g_id, off, s_id, ok)]
    pick = jnp.argsort(1 - flat[4], stable=True)[:n_items]
    arrs = [t[pick] for t in flat]
    last = jnp.maximum(jnp.sum(arrs[4]) - 1, 0)
    e_a, g_a, o_a, s_a, v_a = arrs
    keep = v_a > 0
    return tuple(jnp.where(keep, t, t[last]) for t in (e_a, g_a, o_a, s_a)) + (v_a,)


def _moe_routed(h, gm, x, w1, b1, w2, b2, gf, layer, final):
    nt = x.shape[0]
    n_g = nt // MOE_GROUP
    rank_t, gate_t, cnt3 = _moe_rank(gm)
    cnt = cnt3[:, :, 0].astype(I32)
    n_seg = n_g * N_EXPERTS
    n_all = n_seg + -(-(n_g * MOE_GROUP * TOP_K) // MOE_WINDOW)
    run = functools.partial(_moe_run, h, x, w1, b1, w2, b2, gf, rank_t[:, None, :], gate_t[:, None, :], cnt,
                            layer, final)
    return lax.cond(jnp.max(cnt) <= MOE_WINDOW, lambda: run(n_seg), lambda: run(n_all))


def _moe_run(h, x, w1, b1, w2, b2, gf, rank3, gate3, cnt, layer, final, n_items):
    nt = x.shape[0]
    items_e = _moe_items(cnt, False, n_items)
    win_spec = pl.BlockSpec((MOE_WINDOW, D_MODEL), lambda i, ie, ij, io, is_, iv: (io[i], 0))
    per_eg = pl.BlockSpec((1, 1, MOE_GROUP), lambda i, ie, ij, io, is_, iv: (ie[i], 0, ij[i]))
    grp = lambda n: pl.BlockSpec((MOE_GROUP, n), lambda i, ie, ij, io, is_, iv: (ij[i], 0))
    wspec = lambda a, b: pl.BlockSpec((1, 1, a, b), lambda i, ie, ij, io, is_, iv: (layer, ie[i], 0, 0))
    y = pl.pallas_call(
        _moe_expert_kernel,
        grid_spec=pltpu.PrefetchScalarGridSpec(
            num_scalar_prefetch=5, grid=(n_items,),
            in_specs=[grp(D_MODEL), per_eg, per_eg, wspec(D_MODEL, 2 * D_FF), wspec(1, 2 * D_FF),
                      wspec(D_FF, D_MODEL), wspec(1, D_MODEL)],
            out_specs=win_spec,
            scratch_shapes=[pltpu.VMEM((D_MODEL, 2 * D_FF), BF16), pltpu.VMEM((D_FF, D_MODEL), BF16)]),
        out_shape=jax.ShapeDtypeStruct((n_items * MOE_WINDOW, D_MODEL), BF16),
        compiler_params=_cparams(("arbitrary",)),
        name="moe_expert",
    )(*items_e, h, rank3, gate3, w1, b1[:, :, None, :], w2, b2[:, :, None, :])
    items_g = _moe_items(cnt, True, n_items)
    return pl.pallas_call(
        functools.partial(_moe_combine_kernel, n_items=n_items, final=final),
        grid_spec=pltpu.PrefetchScalarGridSpec(
            num_scalar_prefetch=5, grid=(n_items,),
            in_specs=[win_spec, per_eg, grp(D_MODEL),
                      pl.BlockSpec((1, D_MODEL), lambda i, ie, ij, io, is_, iv: (0, 0))],
            out_specs=grp(D_MODEL),
            scratch_shapes=[pltpu.VMEM((MOE_GROUP, D_MODEL), F32)]),
        out_shape=jax.ShapeDtypeStruct((nt, D_MODEL), F32),
        compiler_params=_cparams(("arbitrary",)),
        name="moe_combine",
    )(*items_g, y, rank3, x, gf)


def _rel_bucket(dist):
    n = jnp.maximum(dist, 0)
    max_exact = N_BUCKETS // 2
    log_ratio = jnp.log(jnp.maximum(n, 1).astype(jnp.float32) / max_exact) / math.log(MAX_DISTANCE / max_exact)
    large = jnp.minimum(max_exact + (log_ratio * (N_BUCKETS - max_exact)).astype(jnp.int32), N_BUCKETS - 1)
    return jnp.where(n < max_exact, n, large)


def _bias_delta(rel_bias, dist):
    rb = rel_bias.astype(F32)
    d = jnp.dot(jax.nn.one_hot(_rel_bucket(dist), N_BUCKETS, dtype=F32), rb, precision=HIGHEST) - rb[N_BUCKETS - 1]
    d = jnp.where(((dist >= 0) & (dist < MAX_DISTANCE))[..., None], d, 0.0)
    return jnp.moveaxis(d, -1, 0)


def _group_rows(delta, rows):
    return delta.reshape(B_KV_HEADS, B_REP * rows, delta.shape[-1])


def kernel(x_prompt, x_sample, cache_k, cache_v, cache_kidx, state_wkv, state_shift, state_pool, page_table,
           meta_tokens, rel_bias, norm_mix, norm_ffn, norm_final, w_in, w_out,
           a_mu, a_w0, a_w2, a_a0, a_a2, a_g2, a_kk, a_ka, a_rk, a_ln_w, a_ln_b,
           c_w, c_scale, moe_w_router, moe_b_router, moe_w1, moe_b1, moe_w2, moe_b2):
    assert x_prompt.shape[0] == 1
    depth = w_in.shape[0]
    seq = x_prompt.shape[1]
    nb, dec_seq = x_sample.shape[:2]
    assert dec_seq <= SROWS
    t_p = seq + N_META
    n_s = nb * dec_seq
    n_qblk = -(-t_p // Q_TILE)
    nkp = KEY_CHUNK * ((n_qblk + 1 + TILES_PER_CHUNK - 1) // TILES_PER_CHUNK + 1)
    assert MOE_GROUP % TOK_TILE == 0
    nt = _round_up(max(t_p + n_s, Q_TILE * n_qblk), MOE_GROUP)
    n_pages = page_table.shape[1]
    past = n_pages * PAGE_SIZE
    topk_p = min(TOPK_MAX, seq // 4)
    topk_s = min(TOPK_MAX, (past + dec_seq) // 4)
    n_phys = cache_k.shape[1]

    x = jnp.concatenate([meta_tokens.astype(F32), x_prompt[0], x_sample.reshape(n_s, D_MODEL),
                         jnp.zeros((nt - t_p - n_s, D_MODEL), F32)], axis=0)

    ones_blk = jnp.kron(jnp.eye(A_HEADS, dtype=F32), jnp.ones((HEAD_DIM, HEAD_DIM), F32))
    qi_ = jnp.arange(Q_TILE)[:, None]
    cj_ = jnp.arange(2 * Q_TILE)[None, :]
    side = TAIL_WIDTH - 2 * Q_TILE
    delta_p = jnp.pad(_bias_delta(rel_bias, Q_TILE + qi_ - cj_), ((0, 0), (0, 0), (side, side)))
    ts_ = jnp.arange(SROWS)[:, None]
    cs_ = jnp.arange(PAGE_SIZE)[None, :]
    delta_s = jnp.stack([_group_rows(_bias_delta(rel_bias, ts_ + PAGE_SIZE - cs_), SROWS),
                         _group_rows(_bias_delta(rel_bias, ts_ - cs_), SROWS)])

    outs = {n: [] for n in ("kp", "vp", "kip", "wkvp", "shp", "plp", "ks", "vs", "kis", "wkvs", "shs", "pls")}
    for l in range(depth):
        wl = w_in[l]
        b0 = A_PROJ
        w_cat = jnp.concatenate([
            wl[:, :A_PROJ], wl[:, b0:b0 + B_WIDTH + 2 * KV_WIDTH + IDX_HEADS * IDX_DIM],
            wl[:, b0 + B_PROJ - IDX_DIM - IDX_HEADS:b0 + B_PROJ],
            jnp.zeros((D_MODEL, LANES - IDX_DIM - IDX_HEADS), F32), wl[:, b0 + B_PROJ:]], axis=1).astype(BF16)
        pa, q, k, v, qi, kw, pc = _inproj(x, norm_mix[l][None], w_cat)

        sl_p = slice(0, t_p)
        sl_s = slice(t_p, t_p + n_s)
        outs["kp"].append(k[sl_p].reshape(1, t_p, B_KV_HEADS, HEAD_DIM))
        outs["vp"].append(v[sl_p].reshape(1, t_p, B_KV_HEADS, HEAD_DIM))
        outs["kip"].append(kw[sl_p, :IDX_DIM][None])
        outs["ks"].append(k[sl_s].reshape(nb, dec_seq, B_KV_HEADS, HEAD_DIM))
        outs["vs"].append(v[sl_s].reshape(nb, dec_seq, B_KV_HEADS, HEAD_DIM))
        outs["kis"].append(kw[sl_s, :IDX_DIM].reshape(nb, dec_seq, IDX_DIM))
        outs["shp"].append(pa[t_p - 1:t_p])
        pa_s = pa[sl_s].reshape(nb, dec_seq, A_PROJ)
        outs["shs"].append(pa_s[:, -1])
        pc_s = pc[sl_s].reshape(nb, dec_seq, C_WIDTH)
        outs["plp"].append(pc[t_p - POOL_BUF:t_p][None])
        pool_full = jnp.concatenate([state_pool[l], pc_s], axis=1)
        outs["pls"].append(pool_full[:, -POOL_BUF:])

        rw = (a_mu[l], a_w0[l], a_w2[l], a_a0[l], a_a2[l], a_g2[l], a_kk[l], a_ka[l], a_rk[l], a_ln_w[l], a_ln_b[l])
        oa_p, st_p = _rwkv_chunked(pa[None], jnp.zeros((1, 1, A_PROJ), F32),
                                   jnp.zeros((1, A_HEADS, HEAD_DIM, HEAD_DIM), F32),
                                   rw, ones_blk, t_p, RWKV_BLOCK, RWKV_CHUNK)
        pa_s8 = jnp.pad(pa_s, ((0, 0), (0, SROWS - dec_seq), (0, 0)))
        oa_s, st_s = _rwkv_chunked(pa_s8, state_shift[l][:, None, :], state_wkv[l].astype(F32),
                                   rw, ones_blk, dec_seq, SROWS, SROWS)
        outs["wkvp"].append(st_p.astype(state_wkv.dtype))
        outs["wkvs"].append(st_s.astype(state_wkv.dtype))

        n_real = min(nt, nkp - KEY_FRONT_PAD)
        front = lambda a: jnp.pad(a[:n_real].astype(BF16), ((KEY_FRONT_PAD, nkp - KEY_FRONT_PAD - n_real), (0, 0)))
        ob_p = _dsa_prompt(q, qi, kw, front(k), front(v), front(kw[:, :IDX_DIM]), delta_p, n_qblk, topk_p)
        rows8 = lambda a: jnp.pad(a[sl_s].reshape(nb, dec_seq, -1), ((0, 0), (0, SROWS - dec_seq), (0, 0)))
        page8 = lambda a: jnp.pad(a[sl_s].reshape(nb, dec_seq, -1), ((0, 0), (0, PAGE_SIZE - dec_seq), (0, 0)))
        kv_page = lambda a: page8(a).reshape(nb, PAGE_SIZE, B_KV_HEADS, HEAD_DIM).transpose(0, 2, 3, 1)
        ob_s = _dsa_sample(page_table, rows8(q), rows8(qi), rows8(kw),
                           cache_kidx.transpose(0, 1, 3, 2), cache_k.transpose(0, 1, 3, 4, 2),
                           cache_v.transpose(0, 1, 3, 4, 2), l, page8(kw[:, :IDX_DIM]).transpose(0, 2, 1),
                           kv_page(k), kv_page(v), delta_s, dec_seq, topk_s)

        w_bd = jax.scipy.linalg.block_diag(*[c_w[l][gi] for gi in range(len(POOL_WINDOWS))]).astype(BF16)
        oc_p = _pool(pc[None], w_bd, c_scale[l][None], TOK_TILE, 0)
        pool_in = jnp.concatenate([jnp.zeros((nb, 1, C_WIDTH), F32), pool_full,
                                   jnp.zeros((nb, 2 * POOL_HALO - 1 - POOL_BUF - dec_seq, C_WIDTH), F32)], axis=1)
        oc_s = _pool(pool_in, w_bd, c_scale[l][None], POOL_HALO, 1)[:, POOL_HALO:POOL_HALO + dec_seq]

        def merge(p_rows, s_rows):
            w = p_rows.shape[-1]
            return jnp.concatenate([p_rows[:t_p], s_rows.reshape(n_s, w), jnp.zeros((nt - t_p - n_s, w), F32)], axis=0)

        oa = merge(oa_p[0], oa_s[:, :dec_seq])
        ob = merge(ob_p, ob_s[:, :dec_seq])
        oc = merge(oc_p[0], oc_s)
        x, h2, gm = _outproj(x, oa, ob, oc, w_out[l].astype(BF16), norm_ffn[l][None], moe_w_router[l],
                             moe_b_router[l][None], t_p + n_s)
        x = _moe_routed(h2, gm, x, moe_w1, moe_b1, moe_w2, moe_b2, norm_final[None], l, l == depth - 1)

    y_prompt = x[N_META:t_p][None]
    y_sample = x[t_p:t_p + n_s].reshape(nb, dec_seq, D_MODEL)
    st = lambda n: jnp.stack(outs[n])
    return (y_prompt, y_sample, st("kp"), st("vp"), st("kip"), st("wkvp"), st("shp"), st("plp"),
            st("ks"), st("vs"), st("kis"), st("wkvs"), st("shs"), st("pls"))
```
